```python
import jax, jax.numpy as jnp
from jax import lax
import numpy as np

D_MODEL = 2048
BATCH = 1
SEQ = 8192
DEPTH = 2

CHUNK = 64
Q_BLOCK = 128
ROPE_THETA = 10000.0
EPS = 1e-6

SB_HEADS = 4
SB_HD = 128
RET_HEADS = 4
RET_DK = 128
RET_DV = 256
DIFF_HEADS = 4
DIFF_HD = 64
D_FF = 5632
CONV_W = 3

SB_W = SB_HEADS * SB_HD
RET_QK = RET_HEADS * RET_DK
RET_V = RET_HEADS * RET_DV
DIFF_QK = DIFF_HEADS * 2 * DIFF_HD
DIFF_V = DIFF_HEADS * 2 * DIFF_HD
IN_SPLITS = (SB_W, SB_W, SB_W, RET_QK, RET_QK, RET_V, RET_V, DIFF_QK, DIFF_QK, DIFF_V)
D_IN = sum(IN_SPLITS)
IN_IDX = [int(i) for i in np.cumsum(IN_SPLITS)[:-1]]
BRANCH_W = SB_W + RET_V + DIFF_V

kernel_name = "hybrid_sb_retention_diffattn_convffn_adaln"


def rms_norm(x, g):
    xf = x.astype(jnp.float32)
    y = xf * lax.rsqrt(jnp.mean(xf * xf, axis=-1, keepdims=True) + EPS)
    return (y * g.astype(jnp.float32)).astype(x.dtype)


def rope(x, pos):
    d = x.shape[-1]
    inv = ROPE_THETA ** (-jnp.arange(0, d, 2, dtype=jnp.float32) / d)
    ang = pos.astype(jnp.float32)[..., None] * inv
    ang = ang.reshape(ang.shape[:2] + (1,) * (x.ndim - 3) + (d // 2,))
    cos, sin = jnp.cos(ang), jnp.sin(ang)
    x1, x2 = jnp.split(x.astype(jnp.float32), 2, axis=-1)
    return jnp.concatenate([x1 * cos - x2 * sin, x1 * sin + x2 * cos], axis=-1).astype(x.dtype)


def stick_breaking_attention(q, k, v):
    B, S, H, d = q.shape
    nb = S // Q_BLOCK
    scale = d ** -0.5
    qb = q.reshape(B, nb, Q_BLOCK, H, d).transpose(1, 0, 3, 2, 4)
    kpos = jnp.arange(S)

    def block(args):
        qi, i = args
        qpos = i * Q_BLOCK + jnp.arange(Q_BLOCK)
        z = jnp.einsum('bhqd,bshd->bhqs', qi, k).astype(jnp.float32) * scale
        mask = kpos[None, :] < qpos[:, None]
        log_beta = jax.nn.log_sigmoid(z)
        log_fail = jnp.where(mask, jax.nn.log_sigmoid(-z), 0.0)
        between = lax.cumsum(log_fail, axis=3, reverse=True) - log_fail
        w = jnp.where(mask, jnp.exp(log_beta + between), 0.0)
        return jnp.einsum('bhqs,bshd->bqhd', w.astype(v.dtype), v)

    out = lax.map(block, (qb, jnp.arange(nb)))
    return out.transpose(1, 0, 2, 3, 4).reshape(B, S, H * d)


def retention_chunkwise(q, k, v, log_gamma):
    B, S, H, dk = q.shape
    dv = v.shape[-1]
    N = S // CHUNK
    k = k * (dk ** -0.5)
    to_chunks = lambda t: t.reshape(B, N, CHUNK, H, t.shape[-1]).transpose(1, 0, 3, 2, 4)
    qc, kc, vc = to_chunks(q), to_chunks(k), to_chunks(v)
    j = jnp.arange(CHUNK, dtype=jnp.float32)
    intra_decay = jnp.exp(jnp.abs(j[:, None] - j[None, :]) * log_gamma[:, None, None])
    xi = jnp.exp((j + 1.0) * log_gamma[:, None])[..., None]
    zeta = jnp.exp((CHUNK - 1.0 - j) * log_gamma[:, None])[..., None]
    chunk_decay = jnp.exp(CHUNK * log_gamma)[:, None, None]

    def step(state, xs):
        qi, ki, vi = xs
        scores = jnp.einsum('bhqd,bhkd->bhqk', qi, ki) * intra_decay
        intra = jnp.einsum('bhqk,bhke->bhqe', scores, vi)
        cross = jnp.einsum('bhqd,bhde->bhqe', qi, state) * xi
        state = state * chunk_decay + jnp.einsum('bhkd,bhke->bhde', ki * zeta, vi)
        return state, intra + cross

    state0 = jnp.zeros((B, H, dk, dv), jnp.float32)
    _, out = lax.scan(step, state0, (qc, kc, vc))
    return out.transpose(1, 0, 3, 2, 4).reshape(B, S, H, dv)


def head_group_norm(y, g, b):
    B, S, H, dv = y.shape
    yf = y.astype(jnp.float32)
    mu = jnp.mean(yf, axis=-1, keepdims=True)
    var = jnp.mean(jnp.square(yf - mu), axis=-1, keepdims=True)
    yn = ((yf - mu) * lax.rsqrt(var + EPS)).reshape(B, S, H * dv)
    return yn * g.astype(jnp.float32) + b.astype(jnp.float32)


def differential_attention(q, k, v, lam):
    B, S, H, _, dh = q.shape
    nb = S // Q_BLOCK
    scale = dh ** -0.5
    qb = q.reshape(B, nb, Q_BLOCK, H, 2, dh).transpose(1, 0, 3, 4, 2, 5)
    kchunk = jnp.arange(S) // CHUNK

    def block(args):
        qi, i = args
        qchunk = (i * Q_BLOCK + jnp.arange(Q_BLOCK)) // CHUNK
        mask = kchunk[None, :] <= qchunk[:, None]
        s = jnp.einsum('bhmqd,bshmd->bhmqs', qi, k).astype(jnp.float32) * scale
        p = jax.nn.softmax(jnp.where(mask, s, -jnp.inf), axis=-1)
        a = p[:, :, 0] - lam.astype(jnp.float32) * p[:, :, 1]
        return jnp.einsum('bhqs,bshe->bqhe', a.astype(v.dtype), v)

    out = lax.map(block, (qb, jnp.arange(nb)))
    return out.transpose(1, 0, 2, 3, 4).reshape(B, S, H, 2 * dh)


def causal_depthwise_conv(u, w, b):
    C = u.shape[-1]
    y = lax.conv_general_dilated(u, w[:, None, :].astype(u.dtype), window_strides=(1,),
                                 padding=[(CONV_W - 1, 0)],
                                 dimension_numbers=('NWC', 'WIO', 'NWC'),
                                 feature_group_count=C)
    return y + b


def setup_inputs(seed: int = 0) -> dict:
    key = jax.random.key(seed)
    ks = jax.random.split(key, 32)
    f32 = jnp.float32
    nrm = lambda k, shape, s: jax.random.normal(k, shape, f32) * s
    x = nrm(ks[0], (BATCH, SEQ, D_MODEL), 1.0)
    c = nrm(ks[1], (BATCH, D_MODEL), 1.0)
    positions = jnp.broadcast_to(jnp.arange(SEQ, dtype=jnp.int32), (BATCH, SEQ))
    w_ada = nrm(ks[2], (DEPTH, D_MODEL, 6 * D_MODEL), 0.3 * D_MODEL ** -0.5)
    b_ada = nrm(ks[3], (DEPTH, 6 * D_MODEL), 0.02)
    norm_mix = 1.0 + nrm(ks[4], (DEPTH, D_MODEL), 0.02)
    w_in = nrm(ks[5], (DEPTH, D_MODEL, D_IN), D_MODEL ** -0.5)
    w_gate = nrm(ks[6], (DEPTH, D_MODEL, 3 * D_MODEL), D_MODEL ** -0.5)
    ret_gn_g = 1.0 + nrm(ks[7], (DEPTH, RET_V), 0.02)
    ret_gn_b = nrm(ks[8], (DEPTH, RET_V), 0.02)
    diff_qn = 1.0 + nrm(ks[9], (DEPTH, DIFF_HD), 0.02)
    diff_kn = 1.0 + nrm(ks[10], (DEPTH, DIFF_HD), 0.02)
    diff_on = 1.0 + nrm(ks[11], (DEPTH, 2 * DIFF_HD), 0.02)
    lam_q1 = nrm(ks[12], (DEPTH, DIFF_HD), 0.1)
    lam_k1 = nrm(ks[13], (DEPTH, DIFF_HD), 0.1)
    lam_q2 = nrm(ks[14], (DEPTH, DIFF_HD), 0.1)
    lam_k2 = nrm(ks[15], (DEPTH, DIFF_HD), 0.1)
    w_branch = jnp.concatenate([
        nrm(ks[16], (DEPTH, SB_W, D_MODEL), SB_W ** -0.5),
        nrm(ks[17], (DEPTH, RET_V, D_MODEL), RET_V ** -0.5),
        nrm(ks[18], (DEPTH, DIFF_V, D_MODEL), DIFF_V ** -0.5)], axis=1)
    w_out = nrm(ks[19], (DEPTH, D_MODEL, D_MODEL), D_MODEL ** -0.5)
    norm_ffn = 1.0 + nrm(ks[20], (DEPTH, D_MODEL), 0.02)
    w_up = nrm(ks[21], (DEPTH, D_MODEL, 2 * D_FF), D_MODEL ** -0.5)
    conv_w = nrm(ks[22], (DEPTH, CONV_W, 2 * D_FF), CONV_W ** -0.5)
    conv_b = nrm(ks[23], (DEPTH, 2 * D_FF), 0.02)
    w_down = nrm(ks[24], (DEPTH, D_FF, D_MODEL), D_FF ** -0.5)
    return {"x": x, "c": c, "positions": positions, "w_ada": w_ada, "b_ada": b_ada,
            "norm_mix": norm_mix, "w_in": w_in, "w_gate": w_gate,
            "ret_gn_g": ret_gn_g, "ret_gn_b": ret_gn_b,
            "diff_qn": diff_qn, "diff_kn": diff_kn, "diff_on": diff_on,
            "lam_q1": lam_q1, "lam_k1": lam_k1, "lam_q2": lam_q2, "lam_k2": lam_k2,
            "w_branch": w_branch, "w_out": w_out, "norm_ffn": norm_ffn,
            "w_up": w_up, "conv_w": conv_w, "conv_b": conv_b, "w_down": w_down}


def reference(x, c, positions, w_ada, b_ada, norm_mix, w_in, w_gate, ret_gn_g, ret_gn_b,
              diff_qn, diff_kn, diff_on, lam_q1, lam_k1, lam_q2, lam_k2,
              w_branch, w_out, norm_ffn, w_up, conv_w, conv_b, w_down):
    B, S, _ = x.shape
    log_gamma = jnp.log(1.0 - 2.0 ** (-5.0 - jnp.arange(RET_HEADS, dtype=jnp.float32)))
    c_act = jax.nn.silu(c)
    for l in range(DEPTH):
        mod = c_act @ w_ada[l] + b_ada[l]
        shift1, scale1, gate1, shift2, scale2, gate2 = [m[:, None, :] for m in jnp.split(mod, 6, axis=-1)]

        h = rms_norm(x, norm_mix[l]) * (1.0 + scale1) + shift1
        (sb_q, sb_k, sb_v, r_q, r_k, r_v, r_g, d_q, d_k, d_v) = jnp.split(h @ w_in[l], IN_IDX, axis=-1)

        y_sb = stick_breaking_attention(sb_q.reshape(B, S, SB_HEADS, SB_HD),
                                        sb_k.reshape(B, S, SB_HEADS, SB_HD),
                                        sb_v.reshape(B, S, SB_HEADS, SB_HD))

        rq = rope(r_q.reshape(B, S, RET_HEADS, RET_DK), positions)
        rk = rope(r_k.reshape(B, S, RET_HEADS, RET_DK), positions)
        y_ret = retention_chunkwise(rq, rk, r_v.reshape(B, S, RET_HEADS, RET_DV), log_gamma)
        y_ret = (head_group_norm(y_ret, ret_gn_g[l], ret_gn_b[l]) * jax.nn.silu(r_g.astype(jnp.float32))).astype(x.dtype)

        lam_init = 0.8 - 0.6 * float(np.exp(-0.3 * l))
        lam = (jnp.exp(jnp.sum(lam_q1[l] * lam_k1[l])) - jnp.exp(jnp.sum(lam_q2[l] * lam_k2[l])) + lam_init)
        dq = rope(rms_norm(d_q.reshape(B, S, DIFF_HEADS, 2, DIFF_HD), diff_qn[l]), positions)
        dk = rope(rms_norm(d_k.reshape(B, S, DIFF_HEADS, 2, DIFF_HD), diff_kn[l]), positions)
        y_diff = differential_attention(dq, dk, d_v.reshape(B, S, DIFF_HEADS, 2 * DIFF_HD), lam)
        y_diff = (rms_norm(y_diff, diff_on[l]) * (1.0 - lam_init)).reshape(B, S, DIFF_V)

        g_sb, g_ret, g_diff = jnp.split(jax.nn.sigmoid(h @ w_gate[l]), 3, axis=-1)
        wb = w_branch[l]
        merged = (g_sb * (y_sb @ wb[:SB_W])
                  + g_ret * (y_ret @ wb[SB_W:SB_W + RET_V])
                  + g_diff * (y_diff @ wb[SB_W + RET_V:]))
        x = x + gate1 * (merged @ w_out[l])

        h2 = rms_norm(x, norm_ffn[l]) * (1.0 + scale2) + shift2
        u = causal_depthwise_conv(h2 @ w_up[l], conv_w[l], conv_b[l])
        u_gate, u_val = jnp.split(u, 2, axis=-1)
        x = x + gate2 * ((jax.nn.silu(u_gate) * u_val) @ w_down[l])
    return x
```

```python
import functools

import jax
import jax.numpy as jnp
import numpy as np
from jax import lax
from jax.experimental import pallas as pl
from jax.experimental.pallas import tpu as pltpu

F32 = jnp.float32
BF16 = jnp.bfloat16

CHUNK = 64
CHUNK_SHIFT = CHUNK.bit_length() - 1
assert 1 << CHUNK_SHIFT == CHUNK
ROPE_THETA = 10000.0
EPS = 1e-6
SB_HEADS, SB_HD = 4, 128
RET_HEADS, RET_DK, RET_DV = 4, 128, 256
DIFF_HEADS, DIFF_HD = 4, 64
CONV_W = 3

SB_W = SB_HEADS * SB_HD
RET_QK = RET_HEADS * RET_DK
RET_V = RET_HEADS * RET_DV
DIFF_QK = DIFF_HEADS * 2 * DIFF_HD
DIFF_V = DIFF_HEADS * 2 * DIFF_HD
OFF_SB_Q = 0
OFF_SB_K = OFF_SB_Q + SB_W
OFF_SB_V = OFF_SB_K + SB_W
OFF_R_Q = OFF_SB_V + SB_W
OFF_R_K = OFF_R_Q + RET_QK
OFF_R_V = OFF_R_K + RET_QK
OFF_R_G = OFF_R_V + RET_V
OFF_D_Q = OFF_R_G + RET_V
OFF_D_K = OFF_D_Q + DIFF_QK
OFF_D_V = OFF_D_K + DIFF_QK
D_IN = OFF_D_V + DIFF_V

V7X_VMEM_BYTES = 64 * 1024 * 1024
VMEM_LIMIT = V7X_VMEM_BYTES - 8 * 1024 * 1024
LANES = 128
SUBLANES = 8


def _cparams(n_axes, vmem=VMEM_LIMIT):
    return pltpu.CompilerParams(dimension_semantics=("arbitrary",) * n_axes,
                                vmem_limit_bytes=vmem)


def _tile(n, t):
    t = min(n, t)
    assert n % t == 0, (n, t)
    return t


def _adaln_kernel(c_ref, w_ref, b_ref, o_ref):
    c = c_ref[...]
    ca = c * jax.nn.sigmoid(c)
    o_ref[0] = jnp.sum(w_ref[0] * ca, axis=0, keepdims=True) + b_ref[0]


def _adaln(c, w_ada, b_ada):
    depth, d, n = w_ada.shape
    tn = _tile(n, 1024)
    return pl.pallas_call(
        _adaln_kernel,
        grid=(depth, n // tn),
        in_specs=[pl.BlockSpec((d, 1), lambda l, j: (0, 0)),
                  pl.BlockSpec((1, d, tn), lambda l, j: (l, 0, j)),
                  pl.BlockSpec((1, 1, tn), lambda l, j: (l, 0, j))],
        out_specs=pl.BlockSpec((1, 1, tn), lambda l, j: (l, 0, j)),
        out_shape=jax.ShapeDtypeStruct((depth, 1, n), F32),
        compiler_params=_cparams(2),
        name="adaln",
    )(c.reshape(d, 1), w_ada, b_ada.reshape(depth, 1, n))


def _norm_mod_kernel(x_ref, g_ref, scale_ref, shift_ref, o_ref):
    x = x_ref[...]
    y = x * lax.rsqrt(jnp.mean(x * x, axis=-1, keepdims=True) + EPS)
    y = y * g_ref[...]
    o_ref[...] = (y * (1.0 + scale_ref[...]) + shift_ref[...]).astype(o_ref.dtype)


def _norm_mod(x, g, scale, shift):
    s, d = x.shape
    tm = _tile(s, 512)
    row = pl.BlockSpec((1, d), lambda i: (0, 0))
    return pl.pallas_call(
        _norm_mod_kernel,
        grid=(s // tm,),
        in_specs=[pl.BlockSpec((tm, d), lambda i: (i, 0)), row, row, row],
        out_specs=pl.BlockSpec((tm, d), lambda i: (i, 0)),
        out_shape=jax.ShapeDtypeStruct((s, d), BF16),
        compiler_params=_cparams(1),
        name="norm_mod",
    )(x, g.reshape(1, d), scale, shift)


def _proj_kernel(a_ref, w_ref, o_ref, wbf_ref, *, act):
    @pl.when(pl.program_id(1) == 0)
    def _():
        wbf_ref[...] = w_ref[...].astype(BF16)

    r = jnp.dot(a_ref[...], wbf_ref[...], preferred_element_type=F32)
    if act == "sigmoid":
        r = jax.nn.sigmoid(r)
    o_ref[...] = r.astype(o_ref.dtype)


def _proj(a, w, l, act=None, name="proj"):
    m, k = a.shape
    n = w.shape[2]
    tm, tn = _tile(m, 1024), _tile(n, 1024)
    return pl.pallas_call(
        functools.partial(_proj_kernel, act=act),
        grid=(n // tn, m // tm),
        in_specs=[pl.BlockSpec((tm, k), lambda j, i: (i, 0)),
                  pl.BlockSpec((None, k, tn), lambda j, i: (l, 0, j))],
        out_specs=pl.BlockSpec((tm, tn), lambda j, i: (i, j)),
        out_shape=jax.ShapeDtypeStruct((m, n), BF16),
        scratch_shapes=[pltpu.VMEM((k, tn), BF16)],
        compiler_params=_cparams(2),
        name=name,
    )(a, w)


def _prep_kernel(pos_ref, inv_r_ref, inv_d_ref, qn_ref, kn_ref,
                 rq_ref, rk_ref, dq_ref, dk_ref,
                 orq_ref, ork_ref, odq_ref, odk_ref):
    pos = pos_ref[...].astype(F32)
    tm = pos.shape[0]
    lane = lax.broadcasted_iota(jnp.int32, (tm, LANES), 1)

    ang = pos * inv_r_ref[...]
    cos_r = jnp.cos(ang)
    sin_r = jnp.where(lane < RET_DK // 2, -jnp.sin(ang), jnp.sin(ang))
    k_scale = RET_DK ** -0.5
    for h in range(RET_HEADS):
        sl = slice(h * RET_DK, (h + 1) * RET_DK)
        xq = rq_ref[:, sl].astype(F32)
        xk = rk_ref[:, sl].astype(F32)
        orq_ref[:, sl] = (xq * cos_r + pltpu.roll(xq, RET_DK // 2, 1) * sin_r).astype(orq_ref.dtype)
        ork_ref[:, sl] = ((xk * cos_r + pltpu.roll(xk, RET_DK // 2, 1) * sin_r) * k_scale).astype(ork_ref.dtype)

    ang = pos * inv_d_ref[...]
    first_half = (lane & (DIFF_HD - 1)) < DIFF_HD // 2
    cos_d = jnp.cos(ang)
    sin_d = jnp.where(first_half, -jnp.sin(ang), jnp.sin(ang))
    low_map = lane < DIFF_HD
    q_scale = DIFF_HD ** -0.5

    def qk_norm_rope(x, gain):
        x2 = x * x
        ms_lo = jnp.sum(jnp.where(low_map, x2, 0.0), axis=-1, keepdims=True) * (1.0 / DIFF_HD)
        ms_hi = jnp.sum(jnp.where(low_map, 0.0, x2), axis=-1, keepdims=True) * (1.0 / DIFF_HD)
        r = jnp.where(low_map, lax.rsqrt(ms_lo + EPS), lax.rsqrt(ms_hi + EPS))
        y = x * r * gain
        partner = jnp.where(first_half, pltpu.roll(y, LANES - DIFF_HD // 2, 1),
                            pltpu.roll(y, DIFF_HD // 2, 1))
        return y * cos_d + partner * sin_d

    for h in range(DIFF_HEADS):
        sl = slice(h * 2 * DIFF_HD, (h + 1) * 2 * DIFF_HD)
        odq_ref[:, sl] = (qk_norm_rope(dq_ref[:, sl].astype(F32), qn_ref[...]) * q_scale).astype(odq_ref.dtype)
        odk_ref[:, sl] = qk_norm_rope(dk_ref[:, sl].astype(F32), kn_ref[...]).astype(odk_ref.dtype)


def _rope_inv(d, reps):
    inv = ROPE_THETA ** (-jnp.arange(0, d, 2, dtype=F32) / d)
    return jnp.tile(inv, 2 * reps).reshape(1, 2 * reps * (d // 2))


def _prep(qkv, positions, qn, kn):
    s = qkv.shape[0]
    tm = _tile(s, 512)
    w = 512
    assert RET_QK == w and DIFF_QK == w
    col = lambda off: pl.BlockSpec((tm, w), lambda i, off=off: (i, off // w))
    row = pl.BlockSpec((1, LANES), lambda i: (0, 0))
    out_spec = pl.BlockSpec((tm, w), lambda i: (i, 0))
    out = jax.ShapeDtypeStruct((s, w), BF16)
    return pl.pallas_call(
        _prep_kernel,
        grid=(s // tm,),
        in_specs=[pl.BlockSpec((tm, 1), lambda i: (i, 0)), row, row, row, row,
                  col(OFF_R_Q), col(OFF_R_K), col(OFF_D_Q), col(OFF_D_K)],
        out_specs=[out_spec] * 4,
        out_shape=[out] * 4,
        compiler_params=_cparams(1),
        name="rope_prep",
    )(positions.reshape(s, 1), _rope_inv(RET_DK, 1), _rope_inv(DIFF_HD, 2),
      jnp.tile(qn, 2).reshape(1, LANES), jnp.tile(kn, 2).reshape(1, LANES),
      qkv, qkv, qkv, qkv)


def _sb_kernel(q_ref, k_ref, v_ref, o_ref, acc_ref, csum_ref, *, tq, tk, scale):
    i = pl.program_id(1)
    q = q_ref[...]
    n_diag = tq // tk
    acc_ref[...] = jnp.zeros_like(acc_ref)
    csum_ref[...] = jnp.zeros_like(csum_ref)
    tri = jnp.where(lax.broadcasted_iota(jnp.int32, (tk, tk), 0)
                    >= lax.broadcasted_iota(jnp.int32, (tk, tk), 1), 1.0, 0.0).astype(BF16)

    def block(kb, masked):
        ks = pl.multiple_of(kb * tk, tk)
        k = k_ref[pl.ds(ks, tk), :]
        v = v_ref[pl.ds(ks, tk), :]
        z = lax.dot_general(q, k, (((1,), (1,)), ((), ())), preferred_element_type=F32) * scale
        sp = jnp.log(1.0 + jnp.exp(-jnp.abs(z)))
        log_beta = jnp.minimum(z, 0.0) - sp
        log_fail = log_beta - z
        if masked:
            qpos = i * tq + lax.broadcasted_iota(jnp.int32, (tq, tk), 0)
            kpos = ks + lax.broadcasted_iota(jnp.int32, (tq, tk), 1)
            mask = kpos < qpos
            log_fail = jnp.where(mask, log_fail, 0.0)
        lf_hi = log_fail.astype(BF16)
        lf_lo = (log_fail - lf_hi.astype(F32)).astype(BF16)
        incl = (jnp.dot(lf_hi, tri, preferred_element_type=F32)
                + jnp.dot(lf_lo, tri, preferred_element_type=F32))
        between = incl - log_fail + csum_ref[...]
        w = jnp.exp(log_beta + between)
        if masked:
            w = jnp.where(mask, w, 0.0)
        acc_ref[...] += jnp.dot(w.astype(BF16), v, preferred_element_type=F32)
        csum_ref[...] += incl[:, 0:1]

    for d in range(n_diag):
        block((i + 1) * n_diag - 1 - d, True)

    def body(t, carry):
        block(i * n_diag - 1 - t, False)
        return carry

    lax.fori_loop(0, i * n_diag, body, 0)
    o_ref[...] = acc_ref[...].astype(o_ref.dtype)


def _sb_attention(qkv):
    s = qkv.shape[0]
    d = SB_HD
    tq = _tile(s, 512)
    tk = _tile(tq, 256)
    return pl.pallas_call(
        functools.partial(_sb_kernel, tq=tq, tk=tk, scale=d ** -0.5),
        grid=(SB_HEADS, s // tq),
        in_specs=[pl.BlockSpec((tq, d), lambda h, i: (i, OFF_SB_Q // d + h)),
                  pl.BlockSpec((s, d), lambda h, i: (0, OFF_SB_K // d + h)),
                  pl.BlockSpec((s, d), lambda h, i: (0, OFF_SB_V // d + h))],
        out_specs=pl.BlockSpec((tq, d), lambda h, i: (i, h)),
        out_shape=jax.ShapeDtypeStruct((s, SB_W), BF16),
        scratch_shapes=[pltpu.VMEM((tq, d), F32), pltpu.VMEM((tq, 1), F32)],
        compiler_params=_cparams(2),
        name="sb_attention",
    )(qkv, qkv, qkv)


def _ret_kernel(lg_ref, q_ref, k_ref, v_ref, g_ref, gn_g_ref, gn_b_ref, o_ref, state_ref, *, tl):
    h = pl.program_id(0)

    @pl.when(pl.program_id(1) == 0)
    def _():
        state_ref[...] = jnp.zeros_like(state_ref)

    lg = lg_ref[h]
    q = q_ref[...]
    k = k_ref[...]
    v = v_ref[...]
    n = lax.broadcasted_iota(jnp.int32, (tl, tl), 0)
    m = lax.broadcasted_iota(jnp.int32, (tl, tl), 1)
    decay = jnp.where((m >> CHUNK_SHIFT) <= (n >> CHUNK_SHIFT),
                      jnp.exp(jnp.abs(n - m).astype(F32) * lg), 0.0)
    scores = lax.dot_general(q, k, (((1,), (1,)), ((), ())), preferred_element_type=F32) * decay
    intra = jnp.dot(scores.astype(BF16), v, preferred_element_type=F32)
    r = lax.broadcasted_iota(jnp.int32, (tl, 1), 0).astype(F32)
    xi = jnp.exp((r + 1.0) * lg)
    state = state_ref[...]
    cross = jnp.dot(q, state.astype(BF16), preferred_element_type=F32) * xi
    zeta = jnp.exp((tl - 1.0 - r) * lg)
    kz = (k.astype(F32) * zeta).astype(BF16)
    block_decay = jnp.exp(jnp.full((1, 1), tl, F32) * lg)
    state_ref[...] = state * block_decay + lax.dot_general(
        kz, v, (((0,), (0,)), ((), ())), preferred_element_type=F32)

    y = intra + cross
    mu = jnp.mean(y, axis=-1, keepdims=True)
    yc = y - mu
    var = jnp.mean(yc * yc, axis=-1, keepdims=True)
    yn = yc * lax.rsqrt(var + EPS) * gn_g_ref[...] + gn_b_ref[...]
    gate = g_ref[...].astype(F32)
    o_ref[...] = (yn * (gate * jax.nn.sigmoid(gate))).astype(o_ref.dtype)


def _retention(qkv, rq, rk, gn_g, gn_b):
    s = qkv.shape[0]
    tl = _tile(s, 256)
    log_gamma = jnp.log(1.0 - 2.0 ** (-5.0 - jnp.arange(RET_HEADS, dtype=F32)))
    qk_spec = pl.BlockSpec((tl, RET_DK), lambda h, b: (b, h))
    gn_spec = pl.BlockSpec((1, RET_DV), lambda h, b: (0, h))
    return pl.pallas_call(
        functools.partial(_ret_kernel, tl=tl),
        grid=(RET_HEADS, s // tl),
        in_specs=[pl.BlockSpec(memory_space=pltpu.SMEM), qk_spec, qk_spec,
                  pl.BlockSpec((tl, RET_DV), lambda h, b: (b, OFF_R_V // RET_DV + h)),
                  pl.BlockSpec((tl, RET_DV), lambda h, b: (b, OFF_R_G // RET_DV + h)),
                  gn_spec, gn_spec],
        out_specs=pl.BlockSpec((tl, RET_DV), lambda h, b: (b, h)),
        out_shape=jax.ShapeDtypeStruct((s, RET_V), BF16),
        scratch_shapes=[pltpu.VMEM((RET_DK, RET_DV), F32)],
        compiler_params=_cparams(2),
        name="retention",
    )(log_gamma, rq, rk, qkv, qkv, gn_g.reshape(1, RET_V), gn_b.reshape(1, RET_V))


def _diff_kernel(lq1_ref, lk1_ref, lq2_ref, lk2_ref, on_ref, q_ref, k_ref, v_ref, o_ref,
                 q2_ref, m_ref, l_ref, acc_ref, *, tq, tk, lam_init):
    i = pl.program_id(1)
    n_diag = tq // tk
    q = q_ref[...]
    lane = lax.broadcasted_iota(jnp.int32, q.shape, 1)
    zero = jnp.zeros_like(q)
    q2_ref[0:tq, :] = jnp.where(lane < DIFF_HD, q, zero)
    q2_ref[tq:2 * tq, :] = jnp.where(lane < DIFF_HD, zero, q)
    m_ref[...] = jnp.full_like(m_ref, -jnp.inf)
    l_ref[...] = jnp.zeros_like(l_ref)
    acc_ref[...] = jnp.zeros_like(acc_ref)

    def block(kb, masked):
        ks = pl.multiple_of(kb * tk, tk)
        k = k_ref[pl.ds(ks, tk), :]
        v = v_ref[pl.ds(ks, tk), :]
        s = lax.dot_general(q2_ref[...], k, (((1,), (1,)), ((), ())), preferred_element_type=F32)
        if masked:
            row = lax.broadcasted_iota(jnp.int32, (2 * tq, tk), 0)
            qchunk = (i * tq + jnp.where(row >= tq, row - tq, row)) >> CHUNK_SHIFT
            kchunk = (ks + lax.broadcasted_iota(jnp.int32, (2 * tq, tk), 1)) >> CHUNK_SHIFT
            s = jnp.where(kchunk <= qchunk, s, -jnp.inf)
        m_old = m_ref[...]
        m_new = jnp.maximum(m_old, jnp.max(s, axis=-1, keepdims=True))
        alpha = jnp.exp(m_old - m_new)
        p = jnp.exp(s - m_new)
        l_ref[...] = alpha * l_ref[...] + jnp.sum(p, axis=-1, keepdims=True)
        acc_ref[...] = alpha * acc_ref[...] + jnp.dot(p.astype(BF16), v, preferred_element_type=F32)
        m_ref[...] = m_new

    def body(kb, carry):
        block(kb, False)
        return carry

    lax.fori_loop(0, i * n_diag, body, 0)
    for d in range(n_diag):
        block(i * n_diag + d, True)

    lam = (jnp.exp(jnp.sum(lq1_ref[...] * lk1_ref[...], axis=-1, keepdims=True))
           - jnp.exp(jnp.sum(lq2_ref[...] * lk2_ref[...], axis=-1, keepdims=True)) + lam_init)
    o = acc_ref[...] / l_ref[...]
    a = o[0:tq] - lam * o[tq:2 * tq]
    y = a * lax.rsqrt(jnp.mean(a * a, axis=-1, keepdims=True) + EPS) * on_ref[...]
    o_ref[...] = (y * (1.0 - lam_init)).astype(o_ref.dtype)


def _diff_attention(qkv, dq, dk, lq1, lk1, lq2, lk2, on, lam_init):
    s = qkv.shape[0]
    dv = 2 * DIFF_HD
    tq = _tile(s, 256)
    tk = _tile(tq, 256)
    lam_spec = pl.BlockSpec((1, DIFF_HD), lambda h, i: (0, 0))
    return pl.pallas_call(
        functools.partial(_diff_kernel, tq=tq, tk=tk, lam_init=lam_init),
        grid=(DIFF_HEADS, s // tq),
        in_specs=[lam_spec, lam_spec, lam_spec, lam_spec,
                  pl.BlockSpec((1, dv), lambda h, i: (0, 0)),
                  pl.BlockSpec((tq, dv), lambda h, i: (i, h)),
                  pl.BlockSpec((s, dv), lambda h, i: (0, h)),
                  pl.BlockSpec((s, dv), lambda h, i: (0, OFF_D_V // dv + h))],
        out_specs=pl.BlockSpec((tq, dv), lambda h, i: (i, h)),
        out_shape=jax.ShapeDtypeStruct((s, DIFF_V), BF16),
        scratch_shapes=[pltpu.VMEM((2 * tq, dv), BF16), pltpu.VMEM((2 * tq, 1), F32),
                        pltpu.VMEM((2 * tq, 1), F32), pltpu.VMEM((2 * tq, dv), F32)],
        compiler_params=_cparams(2),
        name="diff_attention",
    )(lq1.reshape(1, DIFF_HD), lk1.reshape(1, DIFF_HD), lq2.reshape(1, DIFF_HD),
      lk2.reshape(1, DIFF_HD), on.reshape(1, dv), dq, dk, qkv)


def _merge_kernel(ysb_ref, yret_ref, ydiff_ref, gsb_ref, gret_ref, gdiff_ref, w_ref, o_ref, wbf_ref):
    @pl.when(pl.program_id(1) == 0)
    def _():
        wbf_ref[...] = w_ref[...].astype(BF16)

    r_sb = jnp.dot(ysb_ref[...], wbf_ref[0:SB_W, :], preferred_element_type=F32)
    r_ret = jnp.dot(yret_ref[...], wbf_ref[SB_W:SB_W + RET_V, :], preferred_element_type=F32)
    r_diff = jnp.dot(ydiff_ref[...], wbf_ref[SB_W + RET_V:SB_W + RET_V + DIFF_V, :],
                     preferred_element_type=F32)
    merged = (gsb_ref[...].astype(F32) * r_sb + gret_ref[...].astype(F32) * r_ret
              + gdiff_ref[...].astype(F32) * r_diff)
    o_ref[...] = merged.astype(o_ref.dtype)


def _merge(y_sb, y_ret, y_diff, gates, w_branch, l):
    s = y_sb.shape[0]
    _, kw, n = w_branch.shape
    tm, tn = _tile(s, 1024), _tile(n, 1024)
    nj = n // tn
    gate = lambda b: pl.BlockSpec((tm, tn), lambda j, i, b=b: (i, b * nj + j))
    return pl.pallas_call(
        _merge_kernel,
        grid=(nj, s // tm),
        in_specs=[pl.BlockSpec((tm, SB_W), lambda j, i: (i, 0)),
                  pl.BlockSpec((tm, RET_V), lambda j, i: (i, 0)),
                  pl.BlockSpec((tm, DIFF_V), lambda j, i: (i, 0)),
                  gate(0), gate(1), gate(2),
                  pl.BlockSpec((None, kw, tn), lambda j, i: (l, 0, j))],
        out_specs=pl.BlockSpec((tm, tn), lambda j, i: (i, j)),
        out_shape=jax.ShapeDtypeStruct((s, n), BF16),
        scratch_shapes=[pltpu.VMEM((kw, tn), BF16)],
        compiler_params=_cparams(2),
        name="merge",
    )(y_sb, y_ret, y_diff, gates, gates, gates, w_branch)


def _residual_kernel(a_ref, w_ref, x_ref, gate_ref, o_ref, wbf_ref):
    @pl.when(pl.program_id(1) == 0)
    def _():
        wbf_ref[...] = w_ref[...].astype(BF16)

    r = jnp.dot(a_ref[...], wbf_ref[...], preferred_element_type=F32)
    o_ref[...] = x_ref[...] + gate_ref[...] * r


def _residual_proj(a, w, l, x, gate, name):
    m, k = a.shape
    n = w.shape[2]
    tm, tn = _tile(m, 512), _tile(n, 512)
    return pl.pallas_call(
        _residual_kernel,
        grid=(n // tn, m // tm),
        in_specs=[pl.BlockSpec((tm, k), lambda j, i: (i, 0)),
                  pl.BlockSpec((None, k, tn), lambda j, i: (l, 0, j)),
                  pl.BlockSpec((tm, tn), lambda j, i: (i, j)),
                  pl.BlockSpec((1, tn), lambda j, i: (0, j))],
        out_specs=pl.BlockSpec((tm, tn), lambda j, i: (i, j)),
        out_shape=jax.ShapeDtypeStruct((m, n), F32),
        scratch_shapes=[pltpu.VMEM((k, tn), BF16)],
        compiler_params=_cparams(2),
        name=name,
    )(a, w, x, gate)


def _ffn_up_kernel(a_ref, wg_ref, wv_ref, cwg_ref, cwv_ref, cbg_ref, cbv_ref, o_ref,
                   wg_bf_ref, wv_bf_ref, ug_ref, uv_ref, *, tm):
    i = pl.program_id(1)
    halo = SUBLANES

    @pl.when(i == 0)
    def _():
        wg_bf_ref[...] = wg_ref[...].astype(BF16)
        wv_bf_ref[...] = wv_ref[...].astype(BF16)
        ug_ref[0:halo, :] = jnp.zeros((halo, ug_ref.shape[1]), F32)
        uv_ref[0:halo, :] = jnp.zeros((halo, uv_ref.shape[1]), F32)

    a = a_ref[...]

    def conv(u_ref, wbf_ref, cw_ref, cb_ref):
        u_ref[halo:halo + tm, :] = jnp.dot(a, wbf_ref[...], preferred_element_type=F32)
        y = cb_ref[...]
        for t in range(CONV_W):
            shift = CONV_W - 1 - t
            y = y + cw_ref[t:t + 1, :] * u_ref[halo - shift:halo - shift + tm, :]
        u_ref[0:halo, :] = u_ref[tm:tm + halo, :]
        return y

    yg = conv(ug_ref, wg_bf_ref, cwg_ref, cbg_ref)
    yv = conv(uv_ref, wv_bf_ref, cwv_ref, cbv_ref)
    o_ref[...] = (yg * jax.nn.sigmoid(yg) * yv).astype(o_ref.dtype)


def _ffn_up(h2, w_up, conv_w, conv_b, l):
    m, k = h2.shape
    f = w_up.shape[2] // 2
    tm, tn = _tile(m, 1024), _tile(f, 512)
    nj = f // tn
    wspec = lambda half: pl.BlockSpec((None, k, tn), lambda j, i, half=half: (l, 0, half * nj + j))
    cwspec = lambda half: pl.BlockSpec((None, CONV_W, tn), lambda j, i, half=half: (l, 0, half * nj + j))
    cbspec = lambda half: pl.BlockSpec((None, 1, tn), lambda j, i, half=half: (l, 0, half * nj + j))
    return pl.pallas_call(
        functools.partial(_ffn_up_kernel, tm=tm),
        grid=(nj, m // tm),
        in_specs=[pl.BlockSpec((tm, k), lambda j, i: (i, 0)),
                  wspec(0), wspec(1), cwspec(0), cwspec(1), cbspec(0), cbspec(1)],
        out_specs=pl.BlockSpec((tm, tn), lambda j, i: (i, j)),
        out_shape=jax.ShapeDtypeStruct((m, f), BF16),
        scratch_shapes=[pltpu.VMEM((k, tn), BF16), pltpu.VMEM((k, tn), BF16),
                        pltpu.VMEM((tm + SUBLANES, tn), F32), pltpu.VMEM((tm + SUBLANES, tn), F32)],
        compiler_params=_cparams(2),
        name="ffn_up",
    )(h2, w_up, w_up, conv_w, conv_w, conv_b[:, None, :], conv_b[:, None, :])


def kernel(x, c, positions, w_ada, b_ada, norm_mix, w_in, w_gate, ret_gn_g, ret_gn_b, diff_qn, diff_kn, diff_on, lam_q1, lam_k1, lam_q2, lam_k2, w_branch, w_out, norm_ffn, w_up, conv_w, conv_b, w_down):
    batch, s, d = x.shape
    depth = w_ada.shape[0]
    x2d = x.reshape(batch * s, d)
    outs = []
    for b in range(batch):
        xb = x2d[b * s:(b + 1) * s]
        pos = positions[b]
        mod = _adaln(c[b:b + 1], w_ada, b_ada)
        for l in range(depth):
            shift1, scale1, gate1, shift2, scale2, gate2 = [
                mod[l, :, t * d:(t + 1) * d] for t in range(6)]
            h = _norm_mod(xb, norm_mix[l], scale1, shift1)
            qkv = _proj(h, w_in, l, name="in_proj")
            gates = _proj(h, w_gate, l, act="sigmoid", name="gate_proj")
            rq, rk, dq, dk = _prep(qkv, pos, diff_qn[l], diff_kn[l])
            y_sb = _sb_attention(qkv)
            y_ret = _retention(qkv, rq, rk, ret_gn_g[l], ret_gn_b[l])
            lam_init = 0.8 - 0.6 * float(np.exp(-0.3 * l))
            y_diff = _diff_attention(qkv, dq, dk, lam_q1[l], lam_k1[l], lam_q2[l], lam_k2[l],
                                     diff_on[l], lam_init)
            merged = _merge(y_sb, y_ret, y_diff, gates, w_branch, l)
            xb = _residual_proj(merged, w_out, l, xb, gate1, name="out_proj")
            h2 = _norm_mod(xb, norm_ffn[l], scale2, shift2)
            act = _ffn_up(h2, w_up, conv_w, conv_b, l)
            xb = _residual_proj(act, w_down, l, xb, gate2, name="ffn_down")
        outs.append(xb)
    return jnp.concatenate(outs, axis=0).reshape(batch, s, d)
```

```python
import functools

import jax
import jax.numpy as jnp
import numpy as np
from jax import lax
from jax.experimental import pallas as pl
from jax.experimental.pallas import tpu as pltpu

F32 = jnp.float32
BF16 = jnp.bfloat16

CHUNK = 64
CHUNK_SHIFT = CHUNK.bit_length() - 1
assert 1 << CHUNK_SHIFT == CHUNK
ROPE_THETA = 10000.0
EPS = 1e-6
SB_HEADS, SB_HD = 4, 128
RET_HEADS, RET_DK, RET_DV = 4, 128, 256
DIFF_HEADS, DIFF_HD = 4, 64
CONV_W = 3

SB_W = SB_HEADS * SB_HD
RET_QK = RET_HEADS * RET_DK
RET_V = RET_HEADS * RET_DV
DIFF_QK = DIFF_HEADS * 2 * DIFF_HD
DIFF_V = DIFF_HEADS * 2 * DIFF_HD
OFF_SB_Q = 0
OFF_SB_K = OFF_SB_Q + SB_W
OFF_SB_V = OFF_SB_K + SB_W
OFF_R_Q = OFF_SB_V + SB_W
OFF_R_K = OFF_R_Q + RET_QK
OFF_R_V = OFF_R_K + RET_QK
OFF_R_G = OFF_R_V + RET_V
OFF_D_Q = OFF_R_G + RET_V
OFF_D_K = OFF_D_Q + DIFF_QK
OFF_D_V = OFF_D_K + DIFF_QK
D_IN = OFF_D_V + DIFF_V

V7X_VMEM_BYTES = 64 * 1024 * 1024
VMEM_LIMIT = V7X_VMEM_BYTES - 8 * 1024 * 1024
LANES = 128
ATT_TK = 512
ATT_STRIP = 256
SUBLANES = 8


def _cparams(n_axes, vmem=VMEM_LIMIT):
    return pltpu.CompilerParams(dimension_semantics=("arbitrary",) * n_axes,
                                vmem_limit_bytes=vmem)


def _tile(n, t):
    t = min(n, t)
    assert n % t == 0, (n, t)
    return t


def _adaln_kernel(c_ref, w_ref, b_ref, o_ref):
    c = c_ref[...]
    ca = c * jax.nn.sigmoid(c)
    o_ref[0] = jnp.sum(w_ref[0] * ca, axis=0, keepdims=True) + b_ref[0]


def _adaln(c, w_ada, b_ada):
    depth, d, n = w_ada.shape
    tn = _tile(n, 1024)
    return pl.pallas_call(
        _adaln_kernel,
        grid=(depth, n // tn),
        in_specs=[pl.BlockSpec((d, 1), lambda l, j: (0, 0)),
                  pl.BlockSpec((1, d, tn), lambda l, j: (l, 0, j)),
                  pl.BlockSpec((1, 1, tn), lambda l, j: (l, 0, j))],
        out_specs=pl.BlockSpec((1, 1, tn), lambda l, j: (l, 0, j)),
        out_shape=jax.ShapeDtypeStruct((depth, 1, n), F32),
        compiler_params=_cparams(2),
        name="adaln",
    )(c.reshape(d, 1), w_ada, b_ada.reshape(depth, 1, n))


def _norm_mod_kernel(x_ref, g_ref, scale_ref, shift_ref, o_ref):
    x = x_ref[...]
    y = x * lax.rsqrt(jnp.mean(x * x, axis=-1, keepdims=True) + EPS)
    y = y * g_ref[...]
    o_ref[...] = (y * (1.0 + scale_ref[...]) + shift_ref[...]).astype(o_ref.dtype)


def _norm_mod(x, g, scale, shift):
    s, d = x.shape
    tm = _tile(s, 512)
    row = pl.BlockSpec((1, d), lambda i: (0, 0))
    return pl.pallas_call(
        _norm_mod_kernel,
        grid=(s // tm,),
        in_specs=[pl.BlockSpec((tm, d), lambda i: (i, 0)), row, row, row],
        out_specs=pl.BlockSpec((tm, d), lambda i: (i, 0)),
        out_shape=jax.ShapeDtypeStruct((s, d), BF16),
        compiler_params=_cparams(1),
        name="norm_mod",
    )(x, g.reshape(1, d), scale, shift)


def _proj_kernel(a_ref, w_ref, o_ref, wbf_ref, *, act):
    @pl.when(pl.program_id(1) == 0)
    def _():
        wbf_ref[...] = w_ref[...].astype(BF16)

    r = jnp.dot(a_ref[...], wbf_ref[...], preferred_element_type=F32)
    if act == "sigmoid":
        r = jax.nn.sigmoid(r)
    o_ref[...] = r.astype(o_ref.dtype)


def _proj(a, w, l, act=None, name="proj"):
    m, k = a.shape
    n = w.shape[2]
    tm, tn = _tile(m, 1024), _tile(n, 1024)
    return pl.pallas_call(
        functools.partial(_proj_kernel, act=act),
        grid=(n // tn, m // tm),
        in_specs=[pl.BlockSpec((tm, k), lambda j, i: (i, 0)),
                  pl.BlockSpec((None, k, tn), lambda j, i: (l, 0, j))],
        out_specs=pl.BlockSpec((tm, tn), lambda j, i: (i, j)),
        out_shape=jax.ShapeDtypeStruct((m, n), BF16),
        scratch_shapes=[pltpu.VMEM((k, tn), BF16)],
        compiler_params=_cparams(2),
        name=name,
    )(a, w)


def _prep_kernel(pos_ref, inv_r_ref, inv_d_ref, qn_ref, kn_ref,
                 rq_ref, rk_ref, dq_ref, dk_ref, sbv_ref, dv_ref,
                 orq_ref, ork_ref, odq_ref, odk_ref, osbvt_ref, odvt_ref):
    pos = pos_ref[...].astype(F32)
    tm = pos.shape[0]
    lane = lax.broadcasted_iota(jnp.int32, (tm, LANES), 1)

    for v_ref, ovt_ref in ((sbv_ref, osbvt_ref), (dv_ref, odvt_ref)):
        for h in range(ovt_ref.shape[0]):
            vt = v_ref[:, h * LANES:(h + 1) * LANES].astype(F32).T
            for cb in range(tm // ATT_TK):
                ovt_ref[h, cb] = vt[:, cb * ATT_TK:(cb + 1) * ATT_TK].astype(ovt_ref.dtype)

    ang = pos * inv_r_ref[...]
    cos_r = jnp.cos(ang)
    sin_r = jnp.where(lane < RET_DK // 2, -jnp.sin(ang), jnp.sin(ang))
    k_scale = RET_DK ** -0.5
    for h in range(RET_HEADS):
        sl = slice(h * RET_DK, (h + 1) * RET_DK)
        xq = rq_ref[:, sl].astype(F32)
        xk = rk_ref[:, sl].astype(F32)
        orq_ref[:, sl] = (xq * cos_r + pltpu.roll(xq, RET_DK // 2, 1) * sin_r).astype(orq_ref.dtype)
        ork_ref[:, sl] = ((xk * cos_r + pltpu.roll(xk, RET_DK // 2, 1) * sin_r) * k_scale).astype(ork_ref.dtype)

    ang = pos * inv_d_ref[...]
    first_half = (lane & (DIFF_HD - 1)) < DIFF_HD // 2
    cos_d = jnp.cos(ang)
    sin_d = jnp.where(first_half, -jnp.sin(ang), jnp.sin(ang))
    low_map = lane < DIFF_HD
    q_scale = DIFF_HD ** -0.5

    def qk_norm_rope(x, gain):
        x2 = x * x
        ms_lo = jnp.sum(jnp.where(low_map, x2, 0.0), axis=-1, keepdims=True) * (1.0 / DIFF_HD)
        ms_hi = jnp.sum(jnp.where(low_map, 0.0, x2), axis=-1, keepdims=True) * (1.0 / DIFF_HD)
        r = jnp.where(low_map, lax.rsqrt(ms_lo + EPS), lax.rsqrt(ms_hi + EPS))
        y = x * r * gain
        partner = jnp.where(first_half, pltpu.roll(y, LANES - DIFF_HD // 2, 1),
                            pltpu.roll(y, DIFF_HD // 2, 1))
        return y * cos_d + partner * sin_d

    for h in range(DIFF_HEADS):
        sl = slice(h * 2 * DIFF_HD, (h + 1) * 2 * DIFF_HD)
        odq_ref[:, sl] = (qk_norm_rope(dq_ref[:, sl].astype(F32), qn_ref[...]) * q_scale).astype(odq_ref.dtype)
        odk_ref[:, sl] = qk_norm_rope(dk_ref[:, sl].astype(F32), kn_ref[...]).astype(odk_ref.dtype)


def _rope_inv(d, reps):
    inv = ROPE_THETA ** (-jnp.arange(0, d, 2, dtype=F32) / d)
    return jnp.tile(inv, 2 * reps).reshape(1, 2 * reps * (d // 2))


def _prep(qkv, positions, qn, kn):
    s = qkv.shape[0]
    tm = _tile(s, 512)
    w = 512
    assert RET_QK == w and DIFF_QK == w
    col = lambda off: pl.BlockSpec((tm, w), lambda i, off=off: (i, off // w))
    row = pl.BlockSpec((1, LANES), lambda i: (0, 0))
    out_spec = pl.BlockSpec((tm, w), lambda i: (i, 0))
    out = jax.ShapeDtypeStruct((s, w), BF16)
    assert SB_HD == LANES and 2 * DIFF_HD == LANES and SB_W == w and DIFF_V == w
    nkb = tm // ATT_TK
    vt_spec = pl.BlockSpec((SB_HEADS, nkb, LANES, ATT_TK), lambda i: (0, i, 0, 0))
    vt_out = jax.ShapeDtypeStruct((SB_HEADS, s // ATT_TK, LANES, ATT_TK), BF16)
    return pl.pallas_call(
        _prep_kernel,
        grid=(s // tm,),
        in_specs=[pl.BlockSpec((tm, 1), lambda i: (i, 0)), row, row, row, row,
                  col(OFF_R_Q), col(OFF_R_K), col(OFF_D_Q), col(OFF_D_K),
                  col(OFF_SB_V), col(OFF_D_V)],
        out_specs=[out_spec] * 4 + [vt_spec] * 2,
        out_shape=[out] * 4 + [vt_out] * 2,
        compiler_params=_cparams(1),
        name="rope_prep",
    )(positions.reshape(s, 1), _rope_inv(RET_DK, 1), _rope_inv(DIFF_HD, 2),
      jnp.tile(qn, 2).reshape(1, LANES), jnp.tile(kn, 2).reshape(1, LANES),
      qkv, qkv, qkv, qkv, qkv, qkv)


def _sb_kernel(q_ref, k_ref, vt_ref, o_ref, acc_ref, csum_ref, z_ref, lb_ref, later_ref, lf0_ref,
               w_ref, *, tq, tk, scale):
    i = pl.program_id(1)
    st = ATT_STRIP
    n_strip, n_half = tq // st, tk // st
    assert tq == tk
    acc_ref[...] = jnp.zeros_like(acc_ref)
    csum_ref[...] = jnp.zeros_like(csum_ref)
    row = lax.broadcasted_iota(jnp.int32, (st, st), 0)
    col = lax.broadcasted_iota(jnp.int32, (st, st), 1)
    tri = jnp.where(col > row, 1.0, 0.0).astype(BF16)

    def scores(c, kb, hf):
        ks = pl.multiple_of(kb * tk, tk)
        k = k_ref[pl.ds(ks + hf * st, st), :]
        return lax.dot_general(k, q_ref[c * st:(c + 1) * st, :], (((1,), (1,)), ((), ())),
                               preferred_element_type=F32)

    def log_probs(raw, masked):
        z = raw * scale
        sp = jnp.log(1.0 + jnp.exp(-jnp.abs(z)))
        log_beta = jnp.minimum(z, 0.0) - sp
        log_fail = log_beta - z
        if masked:
            log_fail = jnp.where(row < col, log_fail, 0.0)
        return log_beta, log_fail.astype(BF16), log_fail[0:1, :]

    def later_sum(lf_bf):
        return jnp.dot(tri, lf_bf, preferred_element_type=F32)

    def weights(c, hf, log_beta, later, lf0, masked):
        lanes = slice(c * st, (c + 1) * st)
        csum = csum_ref[:, lanes]
        w = jnp.exp(log_beta + later + csum)
        if masked:
            w = jnp.where(row < col, w, 0.0)
        w_ref[hf * st:(hf + 1) * st, :] = w.astype(BF16)
        csum_ref[:, lanes] = csum + later[0:1, :] + lf0

    def accumulate(c, kb, n_live):
        lanes = slice(c * st, (c + 1) * st)
        acc_ref[:, lanes] += jnp.dot(vt_ref[kb, :, 0:n_live * st], w_ref[0:n_live * st, :],
                                     preferred_element_type=F32)

    for c in range(n_strip):
        for hf in reversed(range(c + 1)):
            log_beta, lf_bf, lf0 = log_probs(scores(c, i, hf), hf == c)
            weights(c, hf, log_beta, later_sum(lf_bf), lf0, hf == c)
        accumulate(c, i, c + 1)

    steps = [(c, hf) for c in range(n_strip) for hf in reversed(range(n_half))]
    assert len(steps) % 2 == 0 and len(steps) >= 2

    def stage23(p, raw):
        log_beta, lf_bf, lf0 = log_probs(raw, False)
        lb_ref[p], lf0_ref[p] = log_beta, lf0
        later_ref[p] = later_sum(lf_bf)

    first = jnp.maximum(i - 1, 0)
    (c0, hf0), (c1, hf1) = steps[0], steps[1]
    stage23(0, scores(c0, first, hf0))
    z_ref[1] = scores(c1, first, hf1)

    def body(t, carry):
        kb = i - 1 - t
        nxt_kb = jnp.maximum(kb - 1, 0)
        for n, (c, hf) in enumerate(steps):
            p = n % 2
            c2, hf2 = steps[(n + 2) % len(steps)]
            z_ref[p] = scores(c2, kb if n + 2 < len(steps) else nxt_kb, hf2)
            stage23(1 - p, z_ref[1 - p])
            weights(c, hf, lb_ref[p], later_ref[p], lf0_ref[p], False)
            if hf == 0:
                accumulate(c, kb, n_half)
        return carry

    assert len(steps) % 2 == 0
    lax.fori_loop(0, i, body, 0)
    o_ref[...] = acc_ref[...].T.astype(o_ref.dtype)


def _sb_attention(qkv, vt):
    s = qkv.shape[0]
    d = SB_HD
    tq = tk = ATT_TK
    return pl.pallas_call(
        functools.partial(_sb_kernel, tq=tq, tk=tk, scale=d ** -0.5),
        grid=(SB_HEADS, s // tq),
        in_specs=[pl.BlockSpec((tq, d), lambda h, i: (i, OFF_SB_Q // d + h)),
                  pl.BlockSpec((s, d), lambda h, i: (0, OFF_SB_K // d + h)),
                  pl.BlockSpec((None, s // tk, d, tk), lambda h, i: (h, 0, 0, 0))],
        out_specs=pl.BlockSpec((tq, d), lambda h, i: (i, h)),
        out_shape=jax.ShapeDtypeStruct((s, SB_W), BF16),
        scratch_shapes=[pltpu.VMEM((d, tq), F32), pltpu.VMEM((1, tq), F32),
                        pltpu.VMEM((2, ATT_STRIP, ATT_STRIP), F32),
                        pltpu.VMEM((2, ATT_STRIP, ATT_STRIP), F32),
                        pltpu.VMEM((2, ATT_STRIP, ATT_STRIP), F32),
                        pltpu.VMEM((2, 1, ATT_STRIP), F32),
                        pltpu.VMEM((tk, ATT_STRIP), BF16)],
        compiler_params=_cparams(2),
        name="sb_attention",
    )(qkv, qkv, vt)


def _ret_kernel(lg_ref, q_ref, k_ref, v_ref, g_ref, gn_g_ref, gn_b_ref, o_ref, state_ref, *, tl):
    h = pl.program_id(0)

    @pl.when(pl.program_id(1) == 0)
    def _():
        state_ref[...] = jnp.zeros_like(state_ref)

    lg = lg_ref[h]
    q = q_ref[...]
    k = k_ref[...]
    v = v_ref[...]
    n = lax.broadcasted_iota(jnp.int32, (tl, tl), 0)
    m = lax.broadcasted_iota(jnp.int32, (tl, tl), 1)
    decay = jnp.where((m >> CHUNK_SHIFT) <= (n >> CHUNK_SHIFT),
                      jnp.exp(jnp.abs(n - m).astype(F32) * lg), 0.0)
    scores = lax.dot_general(q, k, (((1,), (1,)), ((), ())), preferred_element_type=F32) * decay
    intra = jnp.dot(scores.astype(BF16), v, preferred_element_type=F32)
    r = lax.broadcasted_iota(jnp.int32, (tl, 1), 0).astype(F32)
    xi = jnp.exp((r + 1.0) * lg)
    state = state_ref[...]
    cross = jnp.dot(q, state.astype(BF16), preferred_element_type=F32) * xi
    zeta = jnp.exp((tl - 1.0 - r) * lg)
    kz = (k.astype(F32) * zeta).astype(BF16)
    block_decay = jnp.exp(jnp.full((1, 1), tl, F32) * lg)
    state_ref[...] = state * block_decay + lax.dot_general(
        kz, v, (((0,), (0,)), ((), ())), preferred_element_type=F32)

    y = intra + cross
    mu = jnp.mean(y, axis=-1, keepdims=True)
    yc = y - mu
    var = jnp.mean(yc * yc, axis=-1, keepdims=True)
    yn = yc * lax.rsqrt(var + EPS) * gn_g_ref[...] + gn_b_ref[...]
    gate = g_ref[...].astype(F32)
    o_ref[...] = (yn * (gate * jax.nn.sigmoid(gate))).astype(o_ref.dtype)


def _retention(qkv, rq, rk, gn_g, gn_b):
    s = qkv.shape[0]
    tl = _tile(s, 256)
    log_gamma = jnp.log(1.0 - 2.0 ** (-5.0 - jnp.arange(RET_HEADS, dtype=F32)))
    qk_spec = pl.BlockSpec((tl, RET_DK), lambda h, b: (b, h))
    gn_spec = pl.BlockSpec((1, RET_DV), lambda h, b: (0, h))
    return pl.pallas_call(
        functools.partial(_ret_kernel, tl=tl),
        grid=(RET_HEADS, s // tl),
        in_specs=[pl.BlockSpec(memory_space=pltpu.SMEM), qk_spec, qk_spec,
                  pl.BlockSpec((tl, RET_DV), lambda h, b: (b, OFF_R_V // RET_DV + h)),
                  pl.BlockSpec((tl, RET_DV), lambda h, b: (b, OFF_R_G // RET_DV + h)),
                  gn_spec, gn_spec],
        out_specs=pl.BlockSpec((tl, RET_DV), lambda h, b: (b, h)),
        out_shape=jax.ShapeDtypeStruct((s, RET_V), BF16),
        scratch_shapes=[pltpu.VMEM((RET_DK, RET_DV), F32)],
        compiler_params=_cparams(2),
        name="retention",
    )(log_gamma, rq, rk, qkv, qkv, gn_g.reshape(1, RET_V), gn_b.reshape(1, RET_V))


def _diff_kernel(lq1_ref, lk1_ref, lq2_ref, lk2_ref, on_ref, q_ref, k_ref, vt_ref, o_ref,
                 q2_ref, m_ref, l_ref, acc_ref, s0_ref, s1_ref, *, tq, tk, lam_init):
    i = pl.program_id(1)
    st = ATT_STRIP
    assert tq == tk and tq == 2 * st
    q = q_ref[...]
    lane = lax.broadcasted_iota(jnp.int32, q.shape, 1)
    zero = jnp.zeros_like(q)
    q2_ref[0:tq, :] = jnp.where(lane < DIFF_HD, q, zero)
    q2_ref[tq:2 * tq, :] = jnp.where(lane < DIFF_HD, zero, q)
    m_ref[...] = jnp.full_like(m_ref, -jnp.inf)
    l_ref[...] = jnp.zeros_like(l_ref)
    acc_ref[...] = jnp.zeros_like(acc_ref)

    n_strip = 2 * tq // st
    s_bufs = (s0_ref, s1_ref)

    def scores(c, kb, k_lo, nk):
        ks = pl.multiple_of(kb * tk, tk)
        k = k_ref[pl.ds(ks + k_lo, nk), :]
        return lax.dot_general(k, q2_ref[c * st:(c + 1) * st, :], (((1,), (1,)), ((), ())),
                               preferred_element_type=F32)

    def softmax_pv(c, kb, k_lo, nk, s, masked):
        lanes = slice(c * st, (c + 1) * st)
        if masked:
            kchunk = lax.broadcasted_iota(jnp.int32, (nk, st), 0) >> CHUNK_SHIFT
            qchunk = lax.broadcasted_iota(jnp.int32, (nk, st), 1) >> CHUNK_SHIFT
            s = jnp.where(kchunk <= qchunk, s, -jnp.inf)
        m_old = m_ref[:, lanes]
        m_new = jnp.maximum(m_old, jnp.max(s, axis=0, keepdims=True))
        alpha = jnp.exp(m_old - m_new)
        p = jnp.exp(s - m_new)
        l_ref[:, lanes] = alpha * l_ref[:, lanes] + jnp.sum(p, axis=0, keepdims=True)
        acc_ref[:, lanes] = alpha * acc_ref[:, lanes] + jnp.dot(
            vt_ref[kb, :, k_lo:k_lo + nk], p.astype(BF16), preferred_element_type=F32)
        m_ref[:, lanes] = m_new

    s_bufs[0][...] = scores(0, 0, 0, tk)

    def body(kb, carry):
        for c in range(n_strip):
            nxt_kb, nxt_c = (kb, c + 1) if c + 1 < n_strip else (kb + 1, 0)
            s_bufs[(c + 1) % 2][...] = scores(nxt_c, nxt_kb, 0, tk)
            softmax_pv(c, kb, 0, tk, s_bufs[c % 2][...], False)
        return carry

    lax.fori_loop(0, i, body, 0)
    for c in range(n_strip):
        for hf in range(c % 2 + 1):
            softmax_pv(c, i, hf * st, st, scores(c, i, hf * st, st), hf == c % 2)

    lam = (jnp.exp(jnp.sum(lq1_ref[...] * lk1_ref[...], axis=-1, keepdims=True))
           - jnp.exp(jnp.sum(lq2_ref[...] * lk2_ref[...], axis=-1, keepdims=True)) + lam_init)
    o = acc_ref[...] / l_ref[...]
    a = o[:, 0:tq] - lam * o[:, tq:2 * tq]
    a = a * lax.rsqrt(jnp.mean(a * a, axis=0, keepdims=True) + EPS)
    o_ref[...] = (a.T * on_ref[...] * (1.0 - lam_init)).astype(o_ref.dtype)


def _diff_attention(dq, dk, vt, lq1, lk1, lq2, lk2, on, lam_init):
    s = dq.shape[0]
    dv = 2 * DIFF_HD
    tq = tk = ATT_TK
    lam_spec = pl.BlockSpec((1, DIFF_HD), lambda h, i: (0, 0))
    return pl.pallas_call(
        functools.partial(_diff_kernel, tq=tq, tk=tk, lam_init=lam_init),
        grid=(DIFF_HEADS, s // tq),
        in_specs=[lam_spec, lam_spec, lam_spec, lam_spec,
                  pl.BlockSpec((1, dv), lambda h, i: (0, 0)),
                  pl.BlockSpec((tq, dv), lambda h, i: (i, h)),
                  pl.BlockSpec((s, dv), lambda h, i: (0, h)),
                  pl.BlockSpec((None, s // tk, dv, tk), lambda h, i: (h, 0, 0, 0))],
        out_specs=pl.BlockSpec((tq, dv), lambda h, i: (i, h)),
        out_shape=jax.ShapeDtypeStruct((s, DIFF_V), BF16),
        scratch_shapes=[pltpu.VMEM((2 * tq, dv), BF16), pltpu.VMEM((1, 2 * tq), F32),
                        pltpu.VMEM((1, 2 * tq), F32), pltpu.VMEM((dv, 2 * tq), F32),
                        pltpu.VMEM((tk, ATT_STRIP), F32), pltpu.VMEM((tk, ATT_STRIP), F32)],
        compiler_params=_cparams(2),
        name="diff_attention",
    )(lq1.reshape(1, DIFF_HD), lk1.reshape(1, DIFF_HD), lq2.reshape(1, DIFF_HD),
      lk2.reshape(1, DIFF_HD), on.reshape(1, dv), dq, dk, vt)


def _merge_kernel(ysb_ref, yret_ref, ydiff_ref, gsb_ref, gret_ref, gdiff_ref, w_ref, o_ref, wbf_ref):
    @pl.when(pl.program_id(1) == 0)
    def _():
        wbf_ref[...] = w_ref[...].astype(BF16)

    r_sb = jnp.dot(ysb_ref[...], wbf_ref[0:SB_W, :], preferred_element_type=F32)
    r_ret = jnp.dot(yret_ref[...], wbf_ref[SB_W:SB_W + RET_V, :], preferred_element_type=F32)
    r_diff = jnp.dot(ydiff_ref[...], wbf_ref[SB_W + RET_V:SB_W + RET_V + DIFF_V, :],
                     preferred_element_type=F32)
    merged = (gsb_ref[...].astype(F32) * r_sb + gret_ref[...].astype(F32) * r_ret
              + gdiff_ref[...].astype(F32) * r_diff)
    o_ref[...] = merged.astype(o_ref.dtype)


def _merge(y_sb, y_ret, y_diff, gates, w_branch, l):
    s = y_sb.shape[0]
    _, kw, n = w_branch.shape
    tm, tn = _tile(s, 1024), _tile(n, 1024)
    nj = n // tn
    gate = lambda b: pl.BlockSpec((tm, tn), lambda j, i, b=b: (i, b * nj + j))
    return pl.pallas_call(
        _merge_kernel,
        grid=(nj, s // tm),
        in_specs=[pl.BlockSpec((tm, SB_W), lambda j, i: (i, 0)),
                  pl.BlockSpec((tm, RET_V), lambda j, i: (i, 0)),
                  pl.BlockSpec((tm, DIFF_V), lambda j, i: (i, 0)),
                  gate(0), gate(1), gate(2),
                  pl.BlockSpec((None, kw, tn), lambda j, i: (l, 0, j))],
        out_specs=pl.BlockSpec((tm, tn), lambda j, i: (i, j)),
        out_shape=jax.ShapeDtypeStruct((s, n), BF16),
        scratch_shapes=[pltpu.VMEM((kw, tn), BF16)],
        compiler_params=_cparams(2),
        name="merge",
    )(y_sb, y_ret, y_diff, gates, gates, gates, w_branch)


def _residual_kernel(a_ref, w_ref, x_ref, gate_ref, o_ref, wbf_ref):
    @pl.when(pl.program_id(1) == 0)
    def _():
        wbf_ref[...] = w_ref[...].astype(BF16)

    r = jnp.dot(a_ref[...], wbf_ref[...], preferred_element_type=F32)
    o_ref[...] = x_ref[...] + gate_ref[...] * r


def _residual_proj(a, w, l, x, gate, name):
    m, k = a.shape
    n = w.shape[2]
    t = next(t for t in (1024, 512, 256, 128)
             if 14 * k * t + 20 * t * t <= VMEM_LIMIT - 4 * 1024 * 1024)
    tm, tn = _tile(m, t), _tile(n, t)
    return pl.pallas_call(
        _residual_kernel,
        grid=(n // tn, m // tm),
        in_specs=[pl.BlockSpec((tm, k), lambda j, i: (i, 0)),
                  pl.BlockSpec((None, k, tn), lambda j, i: (l, 0, j)),
                  pl.BlockSpec((tm, tn), lambda j, i: (i, j)),
                  pl.BlockSpec((1, tn), lambda j, i: (0, j))],
        out_specs=pl.BlockSpec((tm, tn), lambda j, i: (i, j)),
        out_shape=jax.ShapeDtypeStruct((m, n), F32),
        scratch_shapes=[pltpu.VMEM((k, tn), BF16)],
        compiler_params=_cparams(2),
        name=name,
    )(a, w, x, gate)


def _ffn_up_kernel(a_ref, wg_ref, wv_ref, cwg_ref, cwv_ref, cbg_ref, cbv_ref, o_ref,
                   wg_bf_ref, wv_bf_ref, ug_ref, uv_ref, *, tm):
    i = pl.program_id(1)
    halo = SUBLANES

    @pl.when(i == 0)
    def _():
        wg_bf_ref[...] = wg_ref[...].astype(BF16)
        wv_bf_ref[...] = wv_ref[...].astype(BF16)
        ug_ref[0:halo, :] = jnp.zeros((halo, ug_ref.shape[1]), F32)
        uv_ref[0:halo, :] = jnp.zeros((halo, uv_ref.shape[1]), F32)

    a = a_ref[...]

    def conv(u_ref, wbf_ref, cw_ref, cb_ref):
        u_ref[halo:halo + tm, :] = jnp.dot(a, wbf_ref[...], preferred_element_type=F32)
        y = cb_ref[...]
        for t in range(CONV_W):
            shift = CONV_W - 1 - t
            y = y + cw_ref[t:t + 1, :] * u_ref[halo - shift:halo - shift + tm, :]
        u_ref[0:halo, :] = u_ref[tm:tm + halo, :]
        return y

    yg = conv(ug_ref, wg_bf_ref, cwg_ref, cbg_ref)
    yv = conv(uv_ref, wv_bf_ref, cwv_ref, cbv_ref)
    o_ref[...] = (yg * jax.nn.sigmoid(yg) * yv).astype(o_ref.dtype)


def _ffn_up(h2, w_up, conv_w, conv_b, l):
    m, k = h2.shape
    f = w_up.shape[2] // 2
    tm, tn = _tile(m, 1024), _tile(f, 512)
    nj = f // tn
    wspec = lambda half: pl.BlockSpec((None, k, tn), lambda j, i, half=half: (l, 0, half * nj + j))
    cwspec = lambda half: pl.BlockSpec((None, CONV_W, tn), lambda j, i, half=half: (l, 0, half * nj + j))
    cbspec = lambda half: pl.BlockSpec((None, 1, tn), lambda j, i, half=half: (l, 0, half * nj + j))
    return pl.pallas_call(
        functools.partial(_ffn_up_kernel, tm=tm),
        grid=(nj, m // tm),
        in_specs=[pl.BlockSpec((tm, k), lambda j, i: (i, 0)),
                  wspec(0), wspec(1), cwspec(0), cwspec(1), cbspec(0), cbspec(1)],
        out_specs=pl.BlockSpec((tm, tn), lambda j, i: (i, j)),
        out_shape=jax.ShapeDtypeStruct((m, f), BF16),
        scratch_shapes=[pltpu.VMEM((k, tn), BF16), pltpu.VMEM((k, tn), BF16),
                        pltpu.VMEM((tm + SUBLANES, tn), F32), pltpu.VMEM((tm + SUBLANES, tn), F32)],
        compiler_params=_cparams(2),
        name="ffn_up",
    )(h2, w_up, w_up, conv_w, conv_w, conv_b[:, None, :], conv_b[:, None, :])


def kernel(x, c, positions, w_ada, b_ada, norm_mix, w_in, w_gate, ret_gn_g, ret_gn_b, diff_qn, diff_kn, diff_on, lam_q1, lam_k1, lam_q2, lam_k2, w_branch, w_out, norm_ffn, w_up, conv_w, conv_b, w_down):
    batch, s, d = x.shape
    depth = w_ada.shape[0]
    x2d = x.reshape(batch * s, d)
    outs = []
    for b in range(batch):
        xb = x2d[b * s:(b + 1) * s]
        pos = positions[b]
        mod = _adaln(c[b:b + 1], w_ada, b_ada)
        for l in range(depth):
            shift1, scale1, gate1, shift2, scale2, gate2 = [
                mod[l, :, t * d:(t + 1) * d] for t in range(6)]
            h = _norm_mod(xb, norm_mix[l], scale1, shift1)
            qkv = _proj(h, w_in, l, name="in_proj")
            gates = _proj(h, w_gate, l, act="sigmoid", name="gate_proj")
            rq, rk, dq, dk, sb_vt, d_vt = _prep(qkv, pos, diff_qn[l], diff_kn[l])
            y_sb = _sb_attention(qkv, sb_vt)
            y_ret = _retention(qkv, rq, rk, ret_gn_g[l], ret_gn_b[l])
            lam_init = 0.8 - 0.6 * float(np.exp(-0.3 * l))
            y_diff = _diff_attention(dq, dk, d_vt, lam_q1[l], lam_k1[l], lam_q2[l], lam_k2[l],
                                     diff_on[l], lam_init)
            merged = _merge(y_sb, y_ret, y_diff, gates, w_branch, l)
            xb = _residual_proj(merged, w_out, l, xb, gate1, name="out_proj")
            h2 = _norm_mod(xb, norm_ffn[l], scale2, shift2)
            act = _ffn_up(h2, w_up, conv_w, conv_b, l)
            xb = _residual_proj(act, w_down, l, xb, gate2, name="ffn_down")
        outs.append(xb)
    return jnp.concatenate(outs, axis=0).reshape(batch, s, d)
```

```python
import functools

import jax
import jax.numpy as jnp
import numpy as np
from jax import lax
from jax.experimental import pallas as pl
from jax.experimental.pallas import tpu as pltpu

F32 = jnp.float32
BF16 = jnp.bfloat16

CHUNK = 64
CHUNK_SHIFT = CHUNK.bit_length() - 1
assert 1 << CHUNK_SHIFT == CHUNK
ROPE_THETA = 10000.0
EPS = 1e-6
SB_HEADS, SB_HD = 4, 128
RET_HEADS, RET_DK, RET_DV = 4, 128, 256
DIFF_HEADS, DIFF_HD = 4, 64
CONV_W = 3

SB_W = SB_HEADS * SB_HD
RET_QK = RET_HEADS * RET_DK
RET_V = RET_HEADS * RET_DV
DIFF_QK = DIFF_HEADS * 2 * DIFF_HD
DIFF_V = DIFF_HEADS * 2 * DIFF_HD
OFF_SB_Q = 0
OFF_SB_K = OFF_SB_Q + SB_W
OFF_SB_V = OFF_SB_K + SB_W
OFF_R_Q = OFF_SB_V + SB_W
OFF_R_K = OFF_R_Q + RET_QK
OFF_R_V = OFF_R_K + RET_QK
OFF_R_G = OFF_R_V + RET_V
OFF_D_Q = OFF_R_G + RET_V
OFF_D_K = OFF_D_Q + DIFF_QK
OFF_D_V = OFF_D_K + DIFF_QK
D_IN = OFF_D_V + DIFF_V

V7X_VMEM_BYTES = 64 * 1024 * 1024
VMEM_LIMIT = V7X_VMEM_BYTES - 8 * 1024 * 1024
LANES = 128
ATT_TK = 512
ATT_STRIP = 256
SUBLANES = 8


def _cparams(n_axes, vmem=VMEM_LIMIT):
    return pltpu.CompilerParams(dimension_semantics=("arbitrary",) * n_axes,
                                vmem_limit_bytes=vmem)


def _tile(n, t):
    t = min(n, t)
    assert n % t == 0, (n, t)
    return t


def _adaln_kernel(c_ref, w_ref, b_ref, o_ref):
    c = c_ref[...]
    ca = c * jax.nn.sigmoid(c)
    o_ref[0] = jnp.sum(w_ref[0] * ca, axis=0, keepdims=True) + b_ref[0]


def _adaln(c, w_ada, b_ada):
    depth, d, n = w_ada.shape
    tn = _tile(n, 1024)
    return pl.pallas_call(
        _adaln_kernel,
        grid=(depth, n // tn),
        in_specs=[pl.BlockSpec((d, 1), lambda l, j: (0, 0)),
                  pl.BlockSpec((1, d, tn), lambda l, j: (l, 0, j)),
                  pl.BlockSpec((1, 1, tn), lambda l, j: (l, 0, j))],
        out_specs=pl.BlockSpec((1, 1, tn), lambda l, j: (l, 0, j)),
        out_shape=jax.ShapeDtypeStruct((depth, 1, n), F32),
        compiler_params=_cparams(2),
        name="adaln",
    )(c.reshape(d, 1), w_ada, b_ada.reshape(depth, 1, n))


def _norm_mod_kernel(x_ref, g_ref, scale_ref, shift_ref, o_ref):
    x = x_ref[...]
    y = x * lax.rsqrt(jnp.mean(x * x, axis=-1, keepdims=True) + EPS)
    y = y * g_ref[...]
    o_ref[...] = (y * (1.0 + scale_ref[...]) + shift_ref[...]).astype(o_ref.dtype)


def _norm_mod(x, g, scale, shift):
    s, d = x.shape
    tm = _tile(s, 512)
    row = pl.BlockSpec((1, d), lambda i: (0, 0))
    return pl.pallas_call(
        _norm_mod_kernel,
        grid=(s // tm,),
        in_specs=[pl.BlockSpec((tm, d), lambda i: (i, 0)), row, row, row],
        out_specs=pl.BlockSpec((tm, d), lambda i: (i, 0)),
        out_shape=jax.ShapeDtypeStruct((s, d), BF16),
        compiler_params=_cparams(1),
        name="norm_mod",
    )(x, g.reshape(1, d), scale, shift)


def _proj_kernel(a_ref, w_ref, o_ref, wbf_ref, *, act):
    @pl.when(pl.program_id(1) == 0)
    def _():
        wbf_ref[...] = w_ref[...].astype(BF16)

    r = jnp.dot(a_ref[...], wbf_ref[...], preferred_element_type=F32)
    if act == "sigmoid":
        r = jax.nn.sigmoid(r)
    o_ref[...] = r.astype(o_ref.dtype)


def _proj(a, w, l, act=None, name="proj"):
    m, k = a.shape
    n = w.shape[2]
    tm, tn = _tile(m, 1024), _tile(n, 1024)
    return pl.pallas_call(
        functools.partial(_proj_kernel, act=act),
        grid=(n // tn, m // tm),
        in_specs=[pl.BlockSpec((tm, k), lambda j, i: (i, 0)),
                  pl.BlockSpec((None, k, tn), lambda j, i: (l, 0, j))],
        out_specs=pl.BlockSpec((tm, tn), lambda j, i: (i, j)),
        out_shape=jax.ShapeDtypeStruct((m, n), BF16),
        scratch_shapes=[pltpu.VMEM((k, tn), BF16)],
        compiler_params=_cparams(2),
        name=name,
    )(a, w)


def _rope_tables_kernel(pos_ref, inv_ref, cos_r_ref, sin_r_ref, cos_d_ref, sin_d_ref):
    pos = pos_ref[...].astype(F32)
    lane = lax.broadcasted_iota(jnp.int32, (pos.shape[0], LANES), 1)
    ang = pos * inv_ref[...]
    c, s = jnp.cos(ang), jnp.sin(ang)
    half_r, half_d = RET_DK // 2, DIFF_HD // 2

    def ret_table(t):
        return jnp.where(lane < half_r, t, pltpu.roll(t, half_r, 1))

    def diff_table(t):
        d = pltpu.roll(t, LANES - half_r, 1)
        e = jnp.where(lane < half_d, d, pltpu.roll(d, half_d, 1))
        return jnp.where(lane < 2 * half_d, e, pltpu.roll(e, 2 * half_d, 1))

    cos_r_ref[...] = ret_table(c)
    sin_r_ref[...] = jnp.where(lane < half_r, -1.0, 1.0) * ret_table(s)
    cos_d_ref[...] = diff_table(c)
    sin_d_ref[...] = jnp.where((lane & (DIFF_HD - 1)) < half_d, -1.0, 1.0) * diff_table(s)


def _rope_tables(positions):
    s = positions.shape[0]
    tm = _tile(s, 512)
    assert RET_DK == LANES and 2 * DIFF_HD == LANES
    inv_r = ROPE_THETA ** (-jnp.arange(0, RET_DK, 2, dtype=F32) / RET_DK)
    inv_d = ROPE_THETA ** (-jnp.arange(0, DIFF_HD, 2, dtype=F32) / DIFF_HD)
    pad = jnp.zeros((LANES - RET_DK // 2 - DIFF_HD // 2,), F32)
    inv = jnp.concatenate([inv_r, inv_d, pad]).reshape(1, LANES)
    table = pl.BlockSpec((tm, LANES), lambda i: (i, 0))
    return pl.pallas_call(
        _rope_tables_kernel,
        grid=(s // tm,),
        in_specs=[pl.BlockSpec((tm, 1), lambda i: (i, 0)), pl.BlockSpec((1, LANES), lambda i: (0, 0))],
        out_specs=[table] * 4,
        out_shape=[jax.ShapeDtypeStruct((s, LANES), F32)] * 4,
        compiler_params=_cparams(1),
        name="rope_tables",
    )(positions.reshape(s, 1), inv)


def _prep_kernel(cos_r_ref, sin_r_ref, cos_d_ref, sin_d_ref, qn_ref, kn_ref,
                 rq_ref, rk_ref, dq_ref, dk_ref, sbv_ref, dv_ref,
                 orq_ref, ork_ref, odq_ref, odk_ref, osbvt_ref, odvt_ref):
    tm = rq_ref.shape[0]
    lane = lax.broadcasted_iota(jnp.int32, (tm, LANES), 1)

    for v_ref, ovt_ref in ((sbv_ref, osbvt_ref), (dv_ref, odvt_ref)):
        for h in range(ovt_ref.shape[0]):
            vt = v_ref[:, h * LANES:(h + 1) * LANES].astype(F32).T
            for cb in range(tm // ATT_TK):
                ovt_ref[h, cb] = vt[:, cb * ATT_TK:(cb + 1) * ATT_TK].astype(ovt_ref.dtype)

    cos_r, sin_r = cos_r_ref[...], sin_r_ref[...]
    k_scale = RET_DK ** -0.5
    for h in range(RET_HEADS):
        sl = slice(h * RET_DK, (h + 1) * RET_DK)
        xq = rq_ref[:, sl].astype(F32)
        xk = rk_ref[:, sl].astype(F32)
        orq_ref[:, sl] = (xq * cos_r + pltpu.roll(xq, RET_DK // 2, 1) * sin_r).astype(orq_ref.dtype)
        ork_ref[:, sl] = ((xk * cos_r + pltpu.roll(xk, RET_DK // 2, 1) * sin_r) * k_scale).astype(ork_ref.dtype)

    first_half = (lane & (DIFF_HD - 1)) < DIFF_HD // 2
    cos_d, sin_d = cos_d_ref[...], sin_d_ref[...]
    low_map = lane < DIFF_HD
    q_scale = DIFF_HD ** -0.5

    def qk_norm_rope(x, gain):
        x2 = x * x
        ms_lo = jnp.sum(jnp.where(low_map, x2, 0.0), axis=-1, keepdims=True) * (1.0 / DIFF_HD)
        ms_hi = jnp.sum(jnp.where(low_map, 0.0, x2), axis=-1, keepdims=True) * (1.0 / DIFF_HD)
        r = jnp.where(low_map, lax.rsqrt(ms_lo + EPS), lax.rsqrt(ms_hi + EPS))
        y = x * r * gain
        partner = jnp.where(first_half, pltpu.roll(y, LANES - DIFF_HD // 2, 1),
                            pltpu.roll(y, DIFF_HD // 2, 1))
        return y * cos_d + partner * sin_d

    for h in range(DIFF_HEADS):
        sl = slice(h * 2 * DIFF_HD, (h + 1) * 2 * DIFF_HD)
        odq_ref[:, sl] = (qk_norm_rope(dq_ref[:, sl].astype(F32), qn_ref[...]) * q_scale).astype(odq_ref.dtype)
        odk_ref[:, sl] = qk_norm_rope(dk_ref[:, sl].astype(F32), kn_ref[...]).astype(odk_ref.dtype)


def _prep(qkv, tables, qn, kn):
    s = qkv.shape[0]
    tm = _tile(s, 512)
    w = 512
    assert RET_QK == w and DIFF_QK == w
    col = lambda off: pl.BlockSpec((tm, w), lambda i, off=off: (i, off // w))
    row = pl.BlockSpec((1, LANES), lambda i: (0, 0))
    table = pl.BlockSpec((tm, LANES), lambda i: (i, 0))
    out_spec = pl.BlockSpec((tm, w), lambda i: (i, 0))
    out = jax.ShapeDtypeStruct((s, w), BF16)
    assert SB_HD == LANES and 2 * DIFF_HD == LANES and SB_W == w and DIFF_V == w
    nkb = tm // ATT_TK
    vt_spec = pl.BlockSpec((SB_HEADS, nkb, LANES, ATT_TK), lambda i: (0, i, 0, 0))
    vt_out = jax.ShapeDtypeStruct((SB_HEADS, s // ATT_TK, LANES, ATT_TK), BF16)
    return pl.pallas_call(
        _prep_kernel,
        grid=(s // tm,),
        in_specs=[table, table, table, table, row, row,
                  col(OFF_R_Q), col(OFF_R_K), col(OFF_D_Q), col(OFF_D_K),
                  col(OFF_SB_V), col(OFF_D_V)],
        out_specs=[out_spec] * 4 + [vt_spec] * 2,
        out_shape=[out] * 4 + [vt_out] * 2,
        compiler_params=_cparams(1),
        name="rope_prep",
    )(*tables, jnp.tile(qn, 2).reshape(1, LANES), jnp.tile(kn, 2).reshape(1, LANES),
      qkv, qkv, qkv, qkv, qkv, qkv)


def _sb_kernel(q_ref, k_ref, vt_ref, o_ref, acc_ref, csum_ref, z_ref, lb_ref, later_ref, lf0_ref,
               w_ref, *, tq, tk, scale):
    i = pl.program_id(1)
    st = ATT_STRIP
    n_strip, n_half = tq // st, tk // st
    assert tq == tk
    acc_ref[...] = jnp.zeros_like(acc_ref)
    csum_ref[...] = jnp.zeros_like(csum_ref)
    row = lax.broadcasted_iota(jnp.int32, (st, st), 0)
    col = lax.broadcasted_iota(jnp.int32, (st, st), 1)
    tri = jnp.where(col > row, 1.0, 0.0).astype(BF16)

    def scores(c, kb, hf):
        ks = pl.multiple_of(kb * tk, tk)
        k = k_ref[pl.ds(ks + hf * st, st), :]
        return lax.dot_general(k, q_ref[c * st:(c + 1) * st, :], (((1,), (1,)), ((), ())),
                               preferred_element_type=F32)

    def log_probs(raw, masked):
        z = raw * scale
        sp = jnp.log(1.0 + jnp.exp(-jnp.abs(z)))
        log_beta = jnp.minimum(z, 0.0) - sp
        log_fail = log_beta - z
        if masked:
            log_fail = jnp.where(row < col, log_fail, 0.0)
        return log_beta, log_fail.astype(BF16), log_fail[0:1, :]

    def later_sum(lf_bf):
        return jnp.dot(tri, lf_bf, preferred_element_type=F32)

    def weights(c, hf, log_beta, later, lf0, masked):
        lanes = slice(c * st, (c + 1) * st)
        csum = csum_ref[:, lanes]
        w = jnp.exp(log_beta + later + csum)
        if masked:
            w = jnp.where(row < col, w, 0.0)
        w_ref[hf * st:(hf + 1) * st, :] = w.astype(BF16)
        csum_ref[:, lanes] = csum + later[0:1, :] + lf0

    def accumulate(c, kb, n_live):
        lanes = slice(c * st, (c + 1) * st)
        acc_ref[:, lanes] += jnp.dot(vt_ref[kb, :, 0:n_live * st], w_ref[0:n_live * st, :],
                                     preferred_element_type=F32)

    diag_steps = [(c, hf, hf == c, c + 1 if hf == 0 else None)
                  for c in range(n_strip) for hf in reversed(range(c + 1))]
    block_steps = [(c, hf, False, n_half if hf == 0 else None)
                   for c in range(n_strip) for hf in reversed(range(n_half))]
    assert len(block_steps) % 2 == 0

    def stage23(p, raw, masked):
        log_beta, lf_bf, lf0 = log_probs(raw, masked)
        lb_ref[p], lf0_ref[p] = log_beta, lf0
        later_ref[p] = later_sum(lf_bf)

    def run(section, parity):
        for n in range(len(section) - 2):
            p = (parity + n) % 2
            ((c, hf, masked, n_live), kb), ((_, _, masked1, _), _), ((c2, hf2, _, _), kb2) = section[n:n + 3]
            z_ref[p] = scores(c2, kb2, hf2)
            stage23(1 - p, z_ref[1 - p], masked1)
            weights(c, hf, lb_ref[p], later_ref[p], lf0_ref[p], masked)
            if n_live:
                accumulate(c, kb, n_live)

    def at(steps, kb):
        return [(step, kb) for step in steps]

    (c0, hf0, masked0, _), (c1, hf1, _, _) = diag_steps[:2]
    stage23(0, scores(c0, i, hf0), masked0)
    z_ref[1] = scores(c1, i, hf1)
    run(at(diag_steps, i) + at(block_steps[:2], jnp.maximum(i - 1, 0)), 0)

    def body(t, carry):
        kb = i - 1 - 2 * t
        run(at(block_steps, kb) + at(block_steps, kb - 1)
            + at(block_steps[:2], jnp.maximum(kb - 2, 0)), len(diag_steps))
        return carry

    lax.fori_loop(0, i // 2, body, 0)

    @pl.when(i % 2 == 1)
    def _():
        run(at(block_steps, 0) + at(block_steps[:2], 0), len(diag_steps))

    o_ref[...] = acc_ref[...].T.astype(o_ref.dtype)


def _sb_attention(qkv, vt):
    s = qkv.shape[0]
    d = SB_HD
    tq = tk = ATT_TK
    return pl.pallas_call(
        functools.partial(_sb_kernel, tq=tq, tk=tk, scale=d ** -0.5),
        grid=(SB_HEADS, s // tq),
        in_specs=[pl.BlockSpec((tq, d), lambda h, i: (i, OFF_SB_Q // d + h)),
                  pl.BlockSpec((s, d), lambda h, i: (0, OFF_SB_K // d + h)),
                  pl.BlockSpec((None, s // tk, d, tk), lambda h, i: (h, 0, 0, 0))],
        out_specs=pl.BlockSpec((tq, d), lambda h, i: (i, h)),
        out_shape=jax.ShapeDtypeStruct((s, SB_W), BF16),
        scratch_shapes=[pltpu.VMEM((d, tq), F32), pltpu.VMEM((1, tq), F32),
                        pltpu.VMEM((2, ATT_STRIP, ATT_STRIP), F32),
                        pltpu.VMEM((2, ATT_STRIP, ATT_STRIP), F32),
                        pltpu.VMEM((2, ATT_STRIP, ATT_STRIP), F32),
                        pltpu.VMEM((2, 1, ATT_STRIP), F32),
                        pltpu.VMEM((tk, ATT_STRIP), BF16)],
        compiler_params=_cparams(2),
        name="sb_attention",
    )(qkv, qkv, vt)


def _ret_kernel(lg_ref, q_ref, k_ref, v0_ref, v1_ref, g0_ref, g1_ref, gn_g_ref, gn_b_ref, o_ref,
                state_ref, decay_ref, *, tl):
    heads = range(RET_HEADS)
    per_block = RET_HEADS // 2
    r = lax.broadcasted_iota(jnp.int32, (tl, 1), 0).astype(F32)

    def wide(refs, h):
        c0 = (h % per_block) * RET_DV
        return refs[h // per_block][:, c0:c0 + RET_DV]

    @pl.when(pl.program_id(0) == 0)
    def _():
        state_ref[...] = jnp.zeros_like(state_ref)
        n = lax.broadcasted_iota(jnp.int32, (tl, tl), 0)
        m = lax.broadcasted_iota(jnp.int32, (tl, tl), 1)
        dist = jnp.abs(n - m).astype(F32)
        visible = (m >> CHUNK_SHIFT) <= (n >> CHUNK_SHIFT)
        for h in heads:
            decay_ref[h] = jnp.where(visible, jnp.exp(dist * lg_ref[h]), 0.0)

    qs = [q_ref[:, h * RET_DK:(h + 1) * RET_DK] for h in heads]
    ks = [k_ref[:, h * RET_DK:(h + 1) * RET_DK] for h in heads]
    vs = [wide((v0_ref, v1_ref), h) for h in heads]
    scores = [lax.dot_general(qs[h], ks[h], (((1,), (1,)), ((), ())), preferred_element_type=F32)
              for h in heads]
    cross = [jnp.dot(qs[h], state_ref[h].astype(BF16), preferred_element_type=F32) for h in heads]
    pushed = []
    for h in heads:
        zeta = jnp.exp((tl - 1.0 - r) * lg_ref[h])
        kz = (ks[h].astype(F32) * zeta).astype(BF16)
        pushed.append(lax.dot_general(kz, vs[h], (((0,), (0,)), ((), ())),
                                      preferred_element_type=F32))
    intra = [jnp.dot((scores[h] * decay_ref[h]).astype(BF16), vs[h], preferred_element_type=F32)
             for h in heads]
    for h in heads:
        lg = lg_ref[h]
        block_decay = jnp.exp(jnp.full((1, 1), tl, F32) * lg)
        state_ref[h] = state_ref[h] * block_decay + pushed[h]
        y = intra[h] + cross[h] * jnp.exp((r + 1.0) * lg)
        mu = jnp.mean(y, axis=-1, keepdims=True)
        yc = y - mu
        var = jnp.mean(yc * yc, axis=-1, keepdims=True)
        cols = slice(h * RET_DV, (h + 1) * RET_DV)
        yn = yc * lax.rsqrt(var + EPS) * gn_g_ref[:, cols] + gn_b_ref[:, cols]
        gate = wide((g0_ref, g1_ref), h).astype(F32)
        o_ref[:, cols] = (yn * (gate * jax.nn.sigmoid(gate))).astype(o_ref.dtype)


def _retention(qkv, rq, rk, gn_g, gn_b):
    s = qkv.shape[0]
    tl = _tile(s, 256)
    log_gamma = jnp.log(1.0 - 2.0 ** (-5.0 - jnp.arange(RET_HEADS, dtype=F32)))
    half = RET_V // 2
    assert RET_QK == half and OFF_R_V % half == 0 and OFF_R_G % half == 0 and RET_HEADS % 2 == 0
    col = lambda off: pl.BlockSpec((tl, half), lambda b, off=off: (b, off // half))
    row = pl.BlockSpec((1, RET_V), lambda b: (0, 0))
    return pl.pallas_call(
        functools.partial(_ret_kernel, tl=tl),
        grid=(s // tl,),
        in_specs=[pl.BlockSpec(memory_space=pltpu.SMEM), col(0), col(0),
                  col(OFF_R_V), col(OFF_R_V + half), col(OFF_R_G), col(OFF_R_G + half), row, row],
        out_specs=pl.BlockSpec((tl, RET_V), lambda b: (b, 0)),
        out_shape=jax.ShapeDtypeStruct((s, RET_V), BF16),
        scratch_shapes=[pltpu.VMEM((RET_HEADS, RET_DK, RET_DV), F32),
                        pltpu.VMEM((RET_HEADS, tl, tl), F32)],
        compiler_params=_cparams(1),
        name="retention",
    )(log_gamma, rq, rk, qkv, qkv, qkv, qkv, gn_g.reshape(1, RET_V), gn_b.reshape(1, RET_V))


def _diff_kernel(lq1_ref, lk1_ref, lq2_ref, lk2_ref, on_ref, q_ref, k_ref, vt_ref, o_ref,
                 q2_ref, m_ref, l_ref, acc_ref, s0_ref, s1_ref, *, tq, tk, lam_init):
    i = pl.program_id(1)
    st = ATT_STRIP
    assert tq == tk and tq == 2 * st
    q = q_ref[...]
    lane = lax.broadcasted_iota(jnp.int32, q.shape, 1)
    zero = jnp.zeros_like(q)
    q2_ref[0:tq, :] = jnp.where(lane < DIFF_HD, q, zero)
    q2_ref[tq:2 * tq, :] = jnp.where(lane < DIFF_HD, zero, q)
    m_ref[...] = jnp.full_like(m_ref, -jnp.inf)
    l_ref[...] = jnp.zeros_like(l_ref)
    acc_ref[...] = jnp.zeros_like(acc_ref)

    n_strip = 2 * tq // st
    s_bufs = (s0_ref, s1_ref)

    def scores(c, kb, nk):
        ks = pl.multiple_of(kb * tk, tk)
        qc = q2_ref[c * st:(c + 1) * st, :]
        parts = [lax.dot_general(k_ref[pl.ds(ks + r, st), :], qc, (((1,), (1,)), ((), ())),
                                 preferred_element_type=F32) for r in range(0, nk, st)]
        return parts[0] if len(parts) == 1 else jnp.concatenate(parts, axis=0)

    def softmax_pv(c, kb, nk, s, diagonal):
        lanes = slice(c * st, (c + 1) * st)
        if diagonal:
            kchunk = lax.broadcasted_iota(jnp.int32, (nk, st), 0) >> CHUNK_SHIFT
            qchunk = ((c % 2) * st + lax.broadcasted_iota(jnp.int32, (nk, st), 1)) >> CHUNK_SHIFT
            s = jnp.where(kchunk <= qchunk, s, -jnp.inf)
        m_old = m_ref[:, lanes]
        m_new = jnp.maximum(m_old, jnp.max(s, axis=0, keepdims=True))
        alpha = jnp.exp(m_old - m_new)
        p = jnp.exp(s - m_new)
        l_ref[:, lanes] = alpha * l_ref[:, lanes] + jnp.sum(p, axis=0, keepdims=True)
        acc_ref[:, lanes] = alpha * acc_ref[:, lanes] + jnp.dot(
            vt_ref[kb, :, 0:nk], p.astype(BF16), preferred_element_type=F32)
        m_ref[:, lanes] = m_new

    def key_block(kb, diagonal):
        for c in range(n_strip):
            nk = (c % 2 + 1) * st if diagonal else tk
            if c + 1 < n_strip:
                s_bufs[(c + 1) % 2][...] = scores(c + 1, kb, tk)
            elif not diagonal:
                s_bufs[0][...] = scores(0, kb + 1, tk)
            softmax_pv(c, kb, nk, s_bufs[c % 2][0:nk, :], diagonal)

    assert n_strip % 2 == 0
    s_bufs[0][...] = scores(0, 0, tk)

    def body(t, carry):
        key_block(2 * t, False)
        key_block(2 * t + 1, False)
        return carry

    lax.fori_loop(0, i // 2, body, 0)

    @pl.when(i % 2 == 1)
    def _():
        key_block(i - 1, False)

    key_block(i, True)

    lam = (jnp.exp(jnp.sum(lq1_ref[...] * lk1_ref[...], axis=-1, keepdims=True))
           - jnp.exp(jnp.sum(lq2_ref[...] * lk2_ref[...], axis=-1, keepdims=True)) + lam_init)
    o = acc_ref[...] / l_ref[...]
    a = o[:, 0:tq] - lam * o[:, tq:2 * tq]
    a = a * lax.rsqrt(jnp.mean(a * a, axis=0, keepdims=True) + EPS)
    o_ref[...] = (a.T * on_ref[...] * (1.0 - lam_init)).astype(o_ref.dtype)


def _diff_attention(dq, dk, vt, lq1, lk1, lq2, lk2, on, lam_init):
    s = dq.shape[0]
    dv = 2 * DIFF_HD
    tq = tk = ATT_TK
    lam_spec = pl.BlockSpec((1, DIFF_HD), lambda h, i: (0, 0))
    return pl.pallas_call(
        functools.partial(_diff_kernel, tq=tq, tk=tk, lam_init=lam_init),
        grid=(DIFF_HEADS, s // tq),
        in_specs=[lam_spec, lam_spec, lam_spec, lam_spec,
                  pl.BlockSpec((1, dv), lambda h, i: (0, 0)),
                  pl.BlockSpec((tq, dv), lambda h, i: (i, h)),
                  pl.BlockSpec((s, dv), lambda h, i: (0, h)),
                  pl.BlockSpec((None, s // tk, dv, tk), lambda h, i: (h, 0, 0, 0))],
        out_specs=pl.BlockSpec((tq, dv), lambda h, i: (i, h)),
        out_shape=jax.ShapeDtypeStruct((s, DIFF_V), BF16),
        scratch_shapes=[pltpu.VMEM((2 * tq, dv), BF16), pltpu.VMEM((1, 2 * tq), F32),
                        pltpu.VMEM((1, 2 * tq), F32), pltpu.VMEM((dv, 2 * tq), F32),
                        pltpu.VMEM((tk, ATT_STRIP), F32), pltpu.VMEM((tk, ATT_STRIP), F32)],
        compiler_params=_cparams(2),
        name="diff_attention",
    )(lq1.reshape(1, DIFF_HD), lk1.reshape(1, DIFF_HD), lq2.reshape(1, DIFF_HD),
      lk2.reshape(1, DIFF_HD), on.reshape(1, dv), dq, dk, vt)


def _merge_kernel(ysb_ref, yret_ref, ydiff_ref, gsb_ref, gret_ref, gdiff_ref, w_ref, o_ref, wbf_ref):
    @pl.when(pl.program_id(1) == 0)
    def _():
        wbf_ref[...] = w_ref[...].astype(BF16)

    r_sb = jnp.dot(ysb_ref[...], wbf_ref[0:SB_W, :], preferred_element_type=F32)
    r_ret = jnp.dot(yret_ref[...], wbf_ref[SB_W:SB_W + RET_V, :], preferred_element_type=F32)
    r_diff = jnp.dot(ydiff_ref[...], wbf_ref[SB_W + RET_V:SB_W + RET_V + DIFF_V, :],
                     preferred_element_type=F32)
    merged = (gsb_ref[...].astype(F32) * r_sb + gret_ref[...].astype(F32) * r_ret
              + gdiff_ref[...].astype(F32) * r_diff)
    o_ref[...] = merged.astype(o_ref.dtype)


def _merge(y_sb, y_ret, y_diff, gates, w_branch, l):
    s = y_sb.shape[0]
    _, kw, n = w_branch.shape
    tm, tn = _tile(s, 1024), _tile(n, 1024)
    nj = n // tn
    gate = lambda b: pl.BlockSpec((tm, tn), lambda j, i, b=b: (i, b * nj + j))
    return pl.pallas_call(
        _merge_kernel,
        grid=(nj, s // tm),
        in_specs=[pl.BlockSpec((tm, SB_W), lambda j, i: (i, 0)),
                  pl.BlockSpec((tm, RET_V), lambda j, i: (i, 0)),
                  pl.BlockSpec((tm, DIFF_V), lambda j, i: (i, 0)),
                  gate(0), gate(1), gate(2),
                  pl.BlockSpec((None, kw, tn), lambda j, i: (l, 0, j))],
        out_specs=pl.BlockSpec((tm, tn), lambda j, i: (i, j)),
        out_shape=jax.ShapeDtypeStruct((s, n), BF16),
        scratch_shapes=[pltpu.VMEM((kw, tn), BF16)],
        compiler_params=_cparams(2),
        name="merge",
    )(y_sb, y_ret, y_diff, gates, gates, gates, w_branch)


def _residual_kernel(a_ref, w_ref, x_ref, gate_ref, o_ref, wbf_ref):
    @pl.when(pl.program_id(1) == 0)
    def _():
        wbf_ref[...] = w_ref[...].astype(BF16)

    r = jnp.dot(a_ref[...], wbf_ref[...], preferred_element_type=F32)
    o_ref[...] = x_ref[...] + gate_ref[...] * r


def _residual_proj(a, w, l, x, gate, name):
    m, k = a.shape
    n = w.shape[2]
    t = next(t for t in (1024, 512, 256, 128)
             if 14 * k * t + 20 * t * t <= VMEM_LIMIT - 4 * 1024 * 1024)
    tm, tn = _tile(m, t), _tile(n, t)
    return pl.pallas_call(
        _residual_kernel,
        grid=(n // tn, m // tm),
        in_specs=[pl.BlockSpec((tm, k), lambda j, i: (i, 0)),
                  pl.BlockSpec((None, k, tn), lambda j, i: (l, 0, j)),
                  pl.BlockSpec((tm, tn), lambda j, i: (i, j)),
                  pl.BlockSpec((1, tn), lambda j, i: (0, j))],
        out_specs=pl.BlockSpec((tm, tn), lambda j, i: (i, j)),
        out_shape=jax.ShapeDtypeStruct((m, n), F32),
        scratch_shapes=[pltpu.VMEM((k, tn), BF16)],
        compiler_params=_cparams(2),
        name=name,
    )(a, w, x, gate)


def _ffn_up_kernel(a_ref, wg_ref, wv_ref, cwg_ref, cwv_ref, cbg_ref, cbv_ref, o_ref,
                   wg_bf_ref, wv_bf_ref, ug_ref, uv_ref, *, tm):
    i = pl.program_id(1)
    halo = SUBLANES

    @pl.when(i == 0)
    def _():
        wg_bf_ref[...] = wg_ref[...].astype(BF16)
        wv_bf_ref[...] = wv_ref[...].astype(BF16)
        ug_ref[0:halo, :] = jnp.zeros((halo, ug_ref.shape[1]), F32)
        uv_ref[0:halo, :] = jnp.zeros((halo, uv_ref.shape[1]), F32)

    a = a_ref[...]

    def conv(u_ref, wbf_ref, cw_ref, cb_ref):
        u_ref[halo:halo + tm, :] = jnp.dot(a, wbf_ref[...], preferred_element_type=F32)
        y = cb_ref[...]
        for t in range(CONV_W):
            shift = CONV_W - 1 - t
            y = y + cw_ref[t:t + 1, :] * u_ref[halo - shift:halo - shift + tm, :]
        u_ref[0:halo, :] = u_ref[tm:tm + halo, :]
        return y

    yg = conv(ug_ref, wg_bf_ref, cwg_ref, cbg_ref)
    yv = conv(uv_ref, wv_bf_ref, cwv_ref, cbv_ref)
    o_ref[...] = (yg * jax.nn.sigmoid(yg) * yv).astype(o_ref.dtype)


def _ffn_up(h2, w_up, conv_w, conv_b, l):
    m, k = h2.shape
    f = w_up.shape[2] // 2
    tm, tn = _tile(m, 1024), _tile(f, 512)
    nj = f // tn
    wspec = lambda half: pl.BlockSpec((None, k, tn), lambda j, i, half=half: (l, 0, half * nj + j))
    cwspec = lambda half: pl.BlockSpec((None, CONV_W, tn), lambda j, i, half=half: (l, 0, half * nj + j))
    cbspec = lambda half: pl.BlockSpec((None, 1, tn), lambda j, i, half=half: (l, 0, half * nj + j))
    return pl.pallas_call(
        functools.partial(_ffn_up_kernel, tm=tm),
        grid=(nj, m // tm),
        in_specs=[pl.BlockSpec((tm, k), lambda j, i: (i, 0)),
                  wspec(0), wspec(1), cwspec(0), cwspec(1), cbspec(0), cbspec(1)],
        out_specs=pl.BlockSpec((tm, tn), lambda j, i: (i, j)),
        out_shape=jax.ShapeDtypeStruct((m, f), BF16),
        scratch_shapes=[pltpu.VMEM((k, tn), BF16), pltpu.VMEM((k, tn), BF16),
                        pltpu.VMEM((tm + SUBLANES, tn), F32), pltpu.VMEM((tm + SUBLANES, tn), F32)],
        compiler_params=_cparams(2),
        name="ffn_up",
    )(h2, w_up, w_up, conv_w, conv_w, conv_b[:, None, :], conv_b[:, None, :])


def kernel(x, c, positions, w_ada, b_ada, norm_mix, w_in, w_gate, ret_gn_g, ret_gn_b, diff_qn, diff_kn, diff_on, lam_q1, lam_k1, lam_q2, lam_k2, w_branch, w_out, norm_ffn, w_up, conv_w, conv_b, w_down):
    batch, s, d = x.shape
    depth = w_ada.shape[0]
    x2d = x.reshape(batch * s, d)
    outs = []
    for b in range(batch):
        xb = x2d[b * s:(b + 1) * s]
        pos = positions[b]
        mod = _adaln(c[b:b + 1], w_ada, b_ada)
        tables = _rope_tables(pos)
        for l in range(depth):
            shift1, scale1, gate1, shift2, scale2, gate2 = [
                mod[l, :, t * d:(t + 1) * d] for t in range(6)]
            h = _norm_mod(xb, norm_mix[l], scale1, shift1)
            qkv = _proj(h, w_in, l, name="in_proj")
            gates = _proj(h, w_gate, l, act="sigmoid", name="gate_proj")
            rq, rk, dq, dk, sb_vt, d_vt = _prep(qkv, tables, diff_qn[l], diff_kn[l])
            y_sb = _sb_attention(qkv, sb_vt)
            y_ret = _retention(qkv, rq, rk, ret_gn_g[l], ret_gn_b[l])
            lam_init = 0.8 - 0.6 * float(np.exp(-0.3 * l))
            y_diff = _diff_attention(dq, dk, d_vt, lam_q1[l], lam_k1[l], lam_q2[l], lam_k2[l],
                                     diff_on[l], lam_init)
            merged = _merge(y_sb, y_ret, y_diff, gates, w_branch, l)
            xb = _residual_proj(merged, w_out, l, xb, gate1, name="out_proj")
            h2 = _norm_mod(xb, norm_ffn[l], scale2, shift2)
            act = _ffn_up(h2, w_up, conv_w, conv_b, l)
            xb = _residual_proj(act, w_down, l, xb, gate2, name="ffn_down")
        outs.append(xb)
    return jnp.concatenate(outs, axis=0).reshape(batch, s, d)
```

```python
import functools

import jax
import jax.numpy as jnp
import numpy as np
from jax import lax
from jax.experimental import pallas as pl
from jax.experimental.pallas import tpu as pltpu

F32 = jnp.float32
BF16 = jnp.bfloat16

CHUNK = 64
CHUNK_SHIFT = CHUNK.bit_length() - 1
assert 1 << CHUNK_SHIFT == CHUNK
ROPE_THETA = 10000.0
EPS = 1e-6
SB_HEADS, SB_HD = 4, 128
RET_HEADS, RET_DK, RET_DV = 4, 128, 256
DIFF_HEADS, DIFF_HD = 4, 64
CONV_W = 3

SB_W = SB_HEADS * SB_HD
RET_QK = RET_HEADS * RET_DK
RET_V = RET_HEADS * RET_DV
DIFF_QK = DIFF_HEADS * 2 * DIFF_HD
DIFF_V = DIFF_HEADS * 2 * DIFF_HD
OFF_SB_Q = 0
OFF_SB_K = OFF_SB_Q + SB_W
OFF_SB_V = OFF_SB_K + SB_W
OFF_R_Q = OFF_SB_V + SB_W
OFF_R_K = OFF_R_Q + RET_QK
OFF_R_V = OFF_R_K + RET_QK
OFF_R_G = OFF_R_V + RET_V
OFF_D_Q = OFF_R_G + RET_V
OFF_D_K = OFF_D_Q + DIFF_QK
OFF_D_V = OFF_D_K + DIFF_QK
D_IN = OFF_D_V + DIFF_V

V7X_VMEM_BYTES = 64 * 1024 * 1024
VMEM_LIMIT = V7X_VMEM_BYTES - 8 * 1024 * 1024
LANES = 128
ATT_TK = 512
ATT_STRIP = 256
ATT_UNROLL = 4
SUBLANES = 8


def _cparams(n_axes, vmem=VMEM_LIMIT):
    return pltpu.CompilerParams(dimension_semantics=("arbitrary",) * n_axes,
                                vmem_limit_bytes=vmem)


def _tile(n, t):
    t = min(n, t)
    assert n % t == 0, (n, t)
    return t


def _adaln_kernel(c_ref, w_ref, b_ref, o_ref):
    c = c_ref[...]
    ca = c * jax.nn.sigmoid(c)
    o_ref[0] = jnp.sum(w_ref[0] * ca, axis=0, keepdims=True) + b_ref[0]


def _adaln(c, w_ada, b_ada):
    depth, d, n = w_ada.shape
    tn = _tile(n, 1024)
    return pl.pallas_call(
        _adaln_kernel,
        grid=(depth, n // tn),
        in_specs=[pl.BlockSpec((d, 1), lambda l, j: (0, 0)),
                  pl.BlockSpec((1, d, tn), lambda l, j: (l, 0, j)),
                  pl.BlockSpec((1, 1, tn), lambda l, j: (l, 0, j))],
        out_specs=pl.BlockSpec((1, 1, tn), lambda l, j: (l, 0, j)),
        out_shape=jax.ShapeDtypeStruct((depth, 1, n), F32),
        compiler_params=_cparams(2),
        name="adaln",
    )(c.reshape(d, 1), w_ada, b_ada.reshape(depth, 1, n))


def _norm_mod_kernel(x_ref, g_ref, scale_ref, shift_ref, o_ref):
    x = x_ref[...]
    y = x * lax.rsqrt(jnp.mean(x * x, axis=-1, keepdims=True) + EPS)
    y = y * g_ref[...]
    o_ref[...] = (y * (1.0 + scale_ref[...]) + shift_ref[...]).astype(o_ref.dtype)


def _norm_mod(x, g, scale, shift):
    s, d = x.shape
    tm = _tile(s, 512)
    row = pl.BlockSpec((1, d), lambda i: (0, 0))
    return pl.pallas_call(
        _norm_mod_kernel,
        grid=(s // tm,),
        in_specs=[pl.BlockSpec((tm, d), lambda i: (i, 0)), row, row, row],
        out_specs=pl.BlockSpec((tm, d), lambda i: (i, 0)),
        out_shape=jax.ShapeDtypeStruct((s, d), BF16),
        compiler_params=_cparams(1),
        name="norm_mod",
    )(x, g.reshape(1, d), scale, shift)


def _proj_kernel(a_ref, w_ref, o_ref, wbf_ref, *, act):
    @pl.when(pl.program_id(1) == 0)
    def _():
        wbf_ref[...] = w_ref[...].astype(BF16)

    r = jnp.dot(a_ref[...], wbf_ref[...], preferred_element_type=F32)
    if act == "sigmoid":
        r = jax.nn.sigmoid(r)
    o_ref[...] = r.astype(o_ref.dtype)


def _proj(a, w, l, act=None, name="proj"):
    m, k = a.shape
    n = w.shape[2]
    tm, tn = _tile(m, 2048), _tile(n, 1024)
    return pl.pallas_call(
        functools.partial(_proj_kernel, act=act),
        grid=(n // tn, m // tm),
        in_specs=[pl.BlockSpec((tm, k), lambda j, i: (i, 0)),
                  pl.BlockSpec((None, k, tn), lambda j, i: (l, 0, j))],
        out_specs=pl.BlockSpec((tm, tn), lambda j, i: (i, j)),
        out_shape=jax.ShapeDtypeStruct((m, n), BF16),
        scratch_shapes=[pltpu.VMEM((k, tn), BF16)],
        compiler_params=_cparams(2),
        name=name,
    )(a, w)


def _rope_tables_kernel(pos_ref, inv_ref, cos_r_ref, sin_r_ref, cos_d_ref, sin_d_ref):
    pos = pos_ref[...].astype(F32)
    lane = lax.broadcasted_iota(jnp.int32, (pos.shape[0], LANES), 1)
    ang = pos * inv_ref[...]
    c, s = jnp.cos(ang), jnp.sin(ang)
    half_r, half_d = RET_DK // 2, DIFF_HD // 2

    def ret_table(t):
        return jnp.where(lane < half_r, t, pltpu.roll(t, half_r, 1))

    def diff_table(t):
        d = pltpu.roll(t, LANES - half_r, 1)
        e = jnp.where(lane < half_d, d, pltpu.roll(d, half_d, 1))
        return jnp.where(lane < 2 * half_d, e, pltpu.roll(e, 2 * half_d, 1))

    cos_r_ref[...] = ret_table(c)
    sin_r_ref[...] = jnp.where(lane < half_r, -1.0, 1.0) * ret_table(s)
    cos_d_ref[...] = diff_table(c)
    sin_d_ref[...] = jnp.where((lane & (DIFF_HD - 1)) < half_d, -1.0, 1.0) * diff_table(s)


def _rope_tables(positions):
    s = positions.shape[0]
    tm = _tile(s, 512)
    assert RET_DK == LANES and 2 * DIFF_HD == LANES
    inv_r = ROPE_THETA ** (-jnp.arange(0, RET_DK, 2, dtype=F32) / RET_DK)
    inv_d = ROPE_THETA ** (-jnp.arange(0, DIFF_HD, 2, dtype=F32) / DIFF_HD)
    pad = jnp.zeros((LANES - RET_DK // 2 - DIFF_HD // 2,), F32)
    inv = jnp.concatenate([inv_r, inv_d, pad]).reshape(1, LANES)
    table = pl.BlockSpec((tm, LANES), lambda i: (i, 0))
    return pl.pallas_call(
        _rope_tables_kernel,
        grid=(s // tm,),
        in_specs=[pl.BlockSpec((tm, 1), lambda i: (i, 0)), pl.BlockSpec((1, LANES), lambda i: (0, 0))],
        out_specs=[table] * 4,
        out_shape=[jax.ShapeDtypeStruct((s, LANES), F32)] * 4,
        compiler_params=_cparams(1),
        name="rope_tables",
    )(positions.reshape(s, 1), inv)


def _prep_kernel(cos_r_ref, sin_r_ref, cos_d_ref, sin_d_ref, qn_ref, kn_ref,
                 rq_ref, rk_ref, dq_ref, dk_ref, sbv_ref, dv_ref,
                 orq_ref, ork_ref, odq_ref, odk_ref, osbvt_ref, odvt_ref):
    tm = rq_ref.shape[0]
    lane = lax.broadcasted_iota(jnp.int32, (tm, LANES), 1)

    for v_ref, ovt_ref in ((sbv_ref, osbvt_ref), (dv_ref, odvt_ref)):
        for h in range(ovt_ref.shape[0]):
            vt = v_ref[:, h * LANES:(h + 1) * LANES].astype(F32).T
            for cb in range(tm // ATT_TK):
                ovt_ref[h, cb] = vt[:, cb * ATT_TK:(cb + 1) * ATT_TK].astype(ovt_ref.dtype)

    cos_r, sin_r = cos_r_ref[...], sin_r_ref[...]
    k_scale = RET_DK ** -0.5
    for h in range(RET_HEADS):
        sl = slice(h * RET_DK, (h + 1) * RET_DK)
        xq = rq_ref[:, sl].astype(F32)
        xk = rk_ref[:, sl].astype(F32)
        orq_ref[:, sl] = (xq * cos_r + pltpu.roll(xq, RET_DK // 2, 1) * sin_r).astype(orq_ref.dtype)
        ork_ref[:, sl] = ((xk * cos_r + pltpu.roll(xk, RET_DK // 2, 1) * sin_r) * k_scale).astype(ork_ref.dtype)

    first_half = (lane & (DIFF_HD - 1)) < DIFF_HD // 2
    cos_d, sin_d = cos_d_ref[...], sin_d_ref[...]
    low_map = lane < DIFF_HD
    q_scale = DIFF_HD ** -0.5

    def qk_norm_rope(x, gain):
        x2 = x * x
        ms_lo = jnp.sum(jnp.where(low_map, x2, 0.0), axis=-1, keepdims=True) * (1.0 / DIFF_HD)
        ms_hi = jnp.sum(jnp.where(low_map, 0.0, x2), axis=-1, keepdims=True) * (1.0 / DIFF_HD)
        r = jnp.where(low_map, lax.rsqrt(ms_lo + EPS), lax.rsqrt(ms_hi + EPS))
        y = x * r * gain
        partner = jnp.where(first_half, pltpu.roll(y, LANES - DIFF_HD // 2, 1),
                            pltpu.roll(y, DIFF_HD // 2, 1))
        return y * cos_d + partner * sin_d

    for h in range(DIFF_HEADS):
        sl = slice(h * 2 * DIFF_HD, (h + 1) * 2 * DIFF_HD)
        odq_ref[:, sl] = (qk_norm_rope(dq_ref[:, sl].astype(F32), qn_ref[...]) * q_scale).astype(odq_ref.dtype)
        odk_ref[:, sl] = qk_norm_rope(dk_ref[:, sl].astype(F32), kn_ref[...]).astype(odk_ref.dtype)


def _prep(qkv, tables, qn, kn):
    s = qkv.shape[0]
    tm = _tile(s, 512)
    w = 512
    assert RET_QK == w and DIFF_QK == w
    col = lambda off: pl.BlockSpec((tm, w), lambda i, off=off: (i, off // w))
    row = pl.BlockSpec((1, LANES), lambda i: (0, 0))
    table = pl.BlockSpec((tm, LANES), lambda i: (i, 0))
    out_spec = pl.BlockSpec((tm, w), lambda i: (i, 0))
    out = jax.ShapeDtypeStruct((s, w), BF16)
    assert SB_HD == LANES and 2 * DIFF_HD == LANES and SB_W == w and DIFF_V == w
    nkb = tm // ATT_TK
    vt_spec = pl.BlockSpec((SB_HEADS, nkb, LANES, ATT_TK), lambda i: (0, i, 0, 0))
    vt_out = jax.ShapeDtypeStruct((SB_HEADS, s // ATT_TK, LANES, ATT_TK), BF16)
    return pl.pallas_call(
        _prep_kernel,
        grid=(s // tm,),
        in_specs=[table, table, table, table, row, row,
                  col(OFF_R_Q), col(OFF_R_K), col(OFF_D_Q), col(OFF_D_K),
                  col(OFF_SB_V), col(OFF_D_V)],
        out_specs=[out_spec] * 4 + [vt_spec] * 2,
        out_shape=[out] * 4 + [vt_out] * 2,
        compiler_params=_cparams(1),
        name="rope_prep",
    )(*tables, jnp.tile(qn, 2).reshape(1, LANES), jnp.tile(kn, 2).reshape(1, LANES),
      qkv, qkv, qkv, qkv, qkv, qkv)


def _sb_kernel(q_ref, k_ref, vt_ref, o_ref, acc_ref, csum_ref, z_ref, lb_ref, later_ref, lf0_ref,
               w_ref, *, tq, tk, scale):
    i = pl.program_id(1)
    st = ATT_STRIP
    n_strip, n_half = tq // st, tk // st
    assert tq == tk
    acc_ref[...] = jnp.zeros_like(acc_ref)
    csum_ref[...] = jnp.zeros_like(csum_ref)
    row = lax.broadcasted_iota(jnp.int32, (st, st), 0)
    col = lax.broadcasted_iota(jnp.int32, (st, st), 1)
    tri = jnp.where(col > row, 1.0, 0.0).astype(BF16)

    def scores(c, kb, hf):
        ks = pl.multiple_of(kb * tk, tk)
        k = k_ref[pl.ds(ks + hf * st, st), :]
        return lax.dot_general(k, q_ref[c * st:(c + 1) * st, :], (((1,), (1,)), ((), ())),
                               preferred_element_type=F32)

    def log_probs(raw, masked):
        z = raw * scale
        sp = jnp.log(1.0 + jnp.exp(-jnp.abs(z)))
        log_beta = jnp.minimum(z, 0.0) - sp
        log_fail = log_beta - z
        if masked:
            log_fail = jnp.where(row < col, log_fail, 0.0)
        return log_beta, log_fail.astype(BF16), log_fail[0:1, :]

    def later_sum(lf_bf):
        return jnp.dot(tri, lf_bf, preferred_element_type=F32)

    def weights(c, hf, log_beta, later, lf0, masked):
        lanes = slice(c * st, (c + 1) * st)
        csum = csum_ref[:, lanes]
        w = jnp.exp(log_beta + later + csum)
        if masked:
            w = jnp.where(row < col, w, 0.0)
        w_ref[hf * st:(hf + 1) * st, :] = w.astype(BF16)
        csum_ref[:, lanes] = csum + later[0:1, :] + lf0

    def accumulate(c, kb, n_live):
        lanes = slice(c * st, (c + 1) * st)
        acc_ref[:, lanes] += jnp.dot(vt_ref[kb, :, 0:n_live * st], w_ref[0:n_live * st, :],
                                     preferred_element_type=F32)

    diag_steps = [(c, hf, hf == c, c + 1 if hf == 0 else None)
                  for c in range(n_strip) for hf in reversed(range(c + 1))]
    block_steps = [(c, hf, False, n_half if hf == 0 else None)
                   for c in range(n_strip) for hf in reversed(range(n_half))]
    assert len(block_steps) % 2 == 0

    def stage23(p, raw, masked):
        log_beta, lf_bf, lf0 = log_probs(raw, masked)
        lb_ref[p], lf0_ref[p] = log_beta, lf0
        later_ref[p] = later_sum(lf_bf)

    def run(section, parity):
        for n in range(len(section) - 2):
            p = (parity + n) % 2
            ((c, hf, masked, n_live), kb), ((_, _, masked1, _), _), ((c2, hf2, _, _), kb2) = section[n:n + 3]
            z_ref[p] = scores(c2, kb2, hf2)
            stage23(1 - p, z_ref[1 - p], masked1)
            weights(c, hf, lb_ref[p], later_ref[p], lf0_ref[p], masked)
            if n_live:
                accumulate(c, kb, n_live)

    def at(steps, kb):
        return [(step, kb) for step in steps]

    (c0, hf0, masked0, _), (c1, hf1, _, _) = diag_steps[:2]
    stage23(0, scores(c0, i, hf0), masked0)
    z_ref[1] = scores(c1, i, hf1)
    run(at(diag_steps, i) + at(block_steps[:2], jnp.maximum(i - 1, 0)), 0)

    def sweep(first_kb, n_blocks):
        section = []
        for b in range(n_blocks):
            section += at(block_steps, first_kb - b)
        run(section + at(block_steps[:2], jnp.maximum(first_kb - n_blocks, 0)), len(diag_steps))

    def body(t, carry):
        sweep(i - 1 - ATT_UNROLL * t, ATT_UNROLL)
        return carry

    def tail_body(t, carry):
        sweep(i % ATT_UNROLL - 1 - t, 1)
        return carry

    lax.fori_loop(0, i // ATT_UNROLL, body, 0)
    lax.fori_loop(0, i % ATT_UNROLL, tail_body, 0)

    o_ref[...] = acc_ref[...].T.astype(o_ref.dtype)


def _sb_attention(qkv, vt):
    s = qkv.shape[0]
    d = SB_HD
    tq = tk = ATT_TK
    return pl.pallas_call(
        functools.partial(_sb_kernel, tq=tq, tk=tk, scale=d ** -0.5),
        grid=(SB_HEADS, s // tq),
        in_specs=[pl.BlockSpec((tq, d), lambda h, i: (i, OFF_SB_Q // d + h)),
                  pl.BlockSpec((s, d), lambda h, i: (0, OFF_SB_K // d + h)),
                  pl.BlockSpec((None, s // tk, d, tk), lambda h, i: (h, 0, 0, 0))],
        out_specs=pl.BlockSpec((tq, d), lambda h, i: (i, h)),
        out_shape=jax.ShapeDtypeStruct((s, SB_W), BF16),
        scratch_shapes=[pltpu.VMEM((d, tq), F32), pltpu.VMEM((1, tq), F32),
                        pltpu.VMEM((2, ATT_STRIP, ATT_STRIP), F32),
                        pltpu.VMEM((2, ATT_STRIP, ATT_STRIP), F32),
                        pltpu.VMEM((2, ATT_STRIP, ATT_STRIP), F32),
                        pltpu.VMEM((2, 1, ATT_STRIP), F32),
                        pltpu.VMEM((tk, ATT_STRIP), BF16)],
        compiler_params=_cparams(2),
        name="sb_attention",
    )(qkv, qkv, vt)


def _ret_kernel(lg_ref, q_ref, k_ref, v0_ref, v1_ref, g0_ref, g1_ref, gn_g_ref, gn_b_ref, o_ref,
                state_ref, decay_ref, *, tl):
    heads = range(RET_HEADS)
    per_block = RET_HEADS // 2
    r = lax.broadcasted_iota(jnp.int32, (tl, 1), 0).astype(F32)

    def wide(refs, h):
        c0 = (h % per_block) * RET_DV
        return refs[h // per_block][:, c0:c0 + RET_DV]

    @pl.when(pl.program_id(0) == 0)
    def _():
        state_ref[...] = jnp.zeros_like(state_ref)
        n = lax.broadcasted_iota(jnp.int32, (tl, tl), 0)
        m = lax.broadcasted_iota(jnp.int32, (tl, tl), 1)
        dist = jnp.abs(n - m).astype(F32)
        visible = (m >> CHUNK_SHIFT) <= (n >> CHUNK_SHIFT)
        for h in heads:
            decay_ref[h] = jnp.where(visible, jnp.exp(dist * lg_ref[h]), 0.0)

    qs = [q_ref[:, h * RET_DK:(h + 1) * RET_DK] for h in heads]
    ks = [k_ref[:, h * RET_DK:(h + 1) * RET_DK] for h in heads]
    vs = [wide((v0_ref, v1_ref), h) for h in heads]
    scores = [lax.dot_general(qs[h], ks[h], (((1,), (1,)), ((), ())), preferred_element_type=F32)
              for h in heads]
    cross = [jnp.dot(qs[h], state_ref[h].astype(BF16), preferred_element_type=F32) for h in heads]
    pushed = []
    for h in heads:
        zeta = jnp.exp((tl - 1.0 - r) * lg_ref[h])
        kz = (ks[h].astype(F32) * zeta).astype(BF16)
        pushed.append(lax.dot_general(kz, vs[h], (((0,), (0,)), ((), ())),
                                      preferred_element_type=F32))
    intra = [jnp.dot((scores[h] * decay_ref[h]).astype(BF16), vs[h], preferred_element_type=F32)
             for h in heads]
    for h in heads:
        lg = lg_ref[h]
        block_decay = jnp.exp(jnp.full((1, 1), tl, F32) * lg)
        state_ref[h] = state_ref[h] * block_decay + pushed[h]
        y = intra[h] + cross[h] * jnp.exp((r + 1.0) * lg)
        mu = jnp.mean(y, axis=-1, keepdims=True)
        yc = y - mu
        var = jnp.mean(yc * yc, axis=-1, keepdims=True)
        cols = slice(h * RET_DV, (h + 1) * RET_DV)
        yn = yc * lax.rsqrt(var + EPS) * gn_g_ref[:, cols] + gn_b_ref[:, cols]
        gate = wide((g0_ref, g1_ref), h).astype(F32)
        o_ref[:, cols] = (yn * (gate * jax.nn.sigmoid(gate))).astype(o_ref.dtype)


def _retention(qkv, rq, rk, gn_g, gn_b):
    s = qkv.shape[0]
    tl = _tile(s, 256)
    log_gamma = jnp.log(1.0 - 2.0 ** (-5.0 - jnp.arange(RET_HEADS, dtype=F32)))
    half = RET_V // 2
    assert RET_QK == half and OFF_R_V % half == 0 and OFF_R_G % half == 0 and RET_HEADS % 2 == 0
    col = lambda off: pl.BlockSpec((tl, half), lambda b, off=off: (b, off // half))
    row = pl.BlockSpec((1, RET_V), lambda b: (0, 0))
    return pl.pallas_call(
        functools.partial(_ret_kernel, tl=tl),
        grid=(s // tl,),
        in_specs=[pl.BlockSpec(memory_space=pltpu.SMEM), col(0), col(0),
                  col(OFF_R_V), col(OFF_R_V + half), col(OFF_R_G), col(OFF_R_G + half), row, row],
        out_specs=pl.BlockSpec((tl, RET_V), lambda b: (b, 0)),
        out_shape=jax.ShapeDtypeStruct((s, RET_V), BF16),
        scratch_shapes=[pltpu.VMEM((RET_HEADS, RET_DK, RET_DV), F32),
                        pltpu.VMEM((RET_HEADS, tl, tl), F32)],
        compiler_params=_cparams(1),
        name="retention",
    )(log_gamma, rq, rk, qkv, qkv, qkv, qkv, gn_g.reshape(1, RET_V), gn_b.reshape(1, RET_V))


def _diff_kernel(lq1_ref, lk1_ref, lq2_ref, lk2_ref, on_ref, q_ref, k_ref, vt_ref, o_ref,
                 q2_ref, m_ref, l_ref, acc_ref, s0_ref, s1_ref, *, tq, tk, lam_init):
    i = pl.program_id(1)
    st = ATT_STRIP
    assert tq == tk and tq == 2 * st
    q = q_ref[...]
    lane = lax.broadcasted_iota(jnp.int32, q.shape, 1)
    zero = jnp.zeros_like(q)
    q2_ref[0:tq, :] = jnp.where(lane < DIFF_HD, q, zero)
    q2_ref[tq:2 * tq, :] = jnp.where(lane < DIFF_HD, zero, q)
    m_ref[...] = jnp.full_like(m_ref, -jnp.inf)
    l_ref[...] = jnp.zeros_like(l_ref)
    acc_ref[...] = jnp.zeros_like(acc_ref)

    n_strip = 2 * tq // st
    s_bufs = (s0_ref, s1_ref)

    def scores(c, kb, nk):
        ks = pl.multiple_of(kb * tk, tk)
        qc = q2_ref[c * st:(c + 1) * st, :]
        parts = [lax.dot_general(k_ref[pl.ds(ks + r, st), :], qc, (((1,), (1,)), ((), ())),
                                 preferred_element_type=F32) for r in range(0, nk, st)]
        return parts[0] if len(parts) == 1 else jnp.concatenate(parts, axis=0)

    def softmax_pv(c, kb, nk, s, diagonal):
        lanes = slice(c * st, (c + 1) * st)
        if diagonal:
            kchunk = lax.broadcasted_iota(jnp.int32, (nk, st), 0) >> CHUNK_SHIFT
            qchunk = ((c % 2) * st + lax.broadcasted_iota(jnp.int32, (nk, st), 1)) >> CHUNK_SHIFT
            s = jnp.where(kchunk <= qchunk, s, -jnp.inf)
        m_old = m_ref[:, lanes]
        m_new = jnp.maximum(m_old, jnp.max(s, axis=0, keepdims=True))
        alpha = jnp.exp(m_old - m_new)
        p = jnp.exp(s - m_new)
        l_ref[:, lanes] = alpha * l_ref[:, lanes] + jnp.sum(p, axis=0, keepdims=True)
        acc_ref[:, lanes] = alpha * acc_ref[:, lanes] + jnp.dot(
            vt_ref[kb, :, 0:nk], p.astype(BF16), preferred_element_type=F32)
        m_ref[:, lanes] = m_new

    def key_block(kb, diagonal):
        for c in range(n_strip):
            nk = (c % 2 + 1) * st if diagonal else tk
            if c + 1 < n_strip:
                s_bufs[(c + 1) % 2][...] = scores(c + 1, kb, tk)
            elif not diagonal:
                s_bufs[0][...] = scores(0, kb + 1, tk)
            softmax_pv(c, kb, nk, s_bufs[c % 2][0:nk, :], diagonal)

    assert n_strip % 2 == 0
    s_bufs[0][...] = scores(0, 0, tk)

    def body(t, carry):
        for b in range(ATT_UNROLL):
            key_block(ATT_UNROLL * t + b, False)
        return carry

    def tail_body(kb, carry):
        key_block(kb, False)
        return carry

    n_main = i // ATT_UNROLL
    lax.fori_loop(0, n_main, body, 0)
    lax.fori_loop(n_main * ATT_UNROLL, i, tail_body, 0)
    key_block(i, True)

    lam = (jnp.exp(jnp.sum(lq1_ref[...] * lk1_ref[...], axis=-1, keepdims=True))
           - jnp.exp(jnp.sum(lq2_ref[...] * lk2_ref[...], axis=-1, keepdims=True)) + lam_init)
    o = acc_ref[...] / l_ref[...]
    a = o[:, 0:tq] - lam * o[:, tq:2 * tq]
    a = a * lax.rsqrt(jnp.mean(a * a, axis=0, keepdims=True) + EPS)
    o_ref[...] = (a.T * on_ref[...] * (1.0 - lam_init)).astype(o_ref.dtype)


def _diff_attention(dq, dk, vt, lq1, lk1, lq2, lk2, on, lam_init):
    s = dq.shape[0]
    dv = 2 * DIFF_HD
    tq = tk = ATT_TK
    lam_spec = pl.BlockSpec((1, DIFF_HD), lambda h, i: (0, 0))
    return pl.pallas_call(
        functools.partial(_diff_kernel, tq=tq, tk=tk, lam_init=lam_init),
        grid=(DIFF_HEADS, s // tq),
        in_specs=[lam_spec, lam_spec, lam_spec, lam_spec,
                  pl.BlockSpec((1, dv), lambda h, i: (0, 0)),
                  pl.BlockSpec((tq, dv), lambda h, i: (i, h)),
                  pl.BlockSpec((s, dv), lambda h, i: (0, h)),
                  pl.BlockSpec((None, s // tk, dv, tk), lambda h, i: (h, 0, 0, 0))],
        out_specs=pl.BlockSpec((tq, dv), lambda h, i: (i, h)),
        out_shape=jax.ShapeDtypeStruct((s, DIFF_V), BF16),
        scratch_shapes=[pltpu.VMEM((2 * tq, dv), BF16), pltpu.VMEM((1, 2 * tq), F32),
                        pltpu.VMEM((1, 2 * tq), F32), pltpu.VMEM((dv, 2 * tq), F32),
                        pltpu.VMEM((tk, ATT_STRIP), F32), pltpu.VMEM((tk, ATT_STRIP), F32)],
        compiler_params=_cparams(2),
        name="diff_attention",
    )(lq1.reshape(1, DIFF_HD), lk1.reshape(1, DIFF_HD), lq2.reshape(1, DIFF_HD),
      lk2.reshape(1, DIFF_HD), on.reshape(1, dv), dq, dk, vt)


def _merge_kernel(ysb_ref, yret_ref, ydiff_ref, gsb_ref, gret_ref, gdiff_ref, w_ref, o_ref, wbf_ref):
    @pl.when(pl.program_id(1) == 0)
    def _():
        wbf_ref[...] = w_ref[...].astype(BF16)

    r_sb = jnp.dot(ysb_ref[...], wbf_ref[0:SB_W, :], preferred_element_type=F32)
    r_ret = jnp.dot(yret_ref[...], wbf_ref[SB_W:SB_W + RET_V, :], preferred_element_type=F32)
    r_diff = jnp.dot(ydiff_ref[...], wbf_ref[SB_W + RET_V:SB_W + RET_V + DIFF_V, :],
                     preferred_element_type=F32)
    merged = (gsb_ref[...].astype(F32) * r_sb + gret_ref[...].astype(F32) * r_ret
              + gdiff_ref[...].astype(F32) * r_diff)
    o_ref[...] = merged.astype(o_ref.dtype)


def _merge(y_sb, y_ret, y_diff, gates, w_branch, l):
    s = y_sb.shape[0]
    _, kw, n = w_branch.shape
    tm, tn = _tile(s, 1024), _tile(n, 1024)
    nj = n // tn
    gate = lambda b: pl.BlockSpec((tm, tn), lambda j, i, b=b: (i, b * nj + j))
    return pl.pallas_call(
        _merge_kernel,
        grid=(nj, s // tm),
        in_specs=[pl.BlockSpec((tm, SB_W), lambda j, i: (i, 0)),
                  pl.BlockSpec((tm, RET_V), lambda j, i: (i, 0)),
                  pl.BlockSpec((tm, DIFF_V), lambda j, i: (i, 0)),
                  gate(0), gate(1), gate(2),
                  pl.BlockSpec((None, kw, tn), lambda j, i: (l, 0, j))],
        out_specs=pl.BlockSpec((tm, tn), lambda j, i: (i, j)),
        out_shape=jax.ShapeDtypeStruct((s, n), BF16),
        scratch_shapes=[pltpu.VMEM((kw, tn), BF16)],
        compiler_params=_cparams(2),
        name="merge",
    )(y_sb, y_ret, y_diff, gates, gates, gates, w_branch)


def _residual_kernel(a_ref, w_ref, x_ref, gate_ref, o_ref, wbf_ref):
    @pl.when(pl.program_id(1) == 0)
    def _():
        wbf_ref[...] = w_ref[...].astype(BF16)

    r = jnp.dot(a_ref[...], wbf_ref[...], preferred_element_type=F32)
    o_ref[...] = x_ref[...] + gate_ref[...] * r


def _cast_kernel(w_ref, o_ref):
    o_ref[...] = w_ref[...].astype(o_ref.dtype)


def _to_bf16(w, l):
    _, k, n = w.shape
    tk = _tile(k, 512)
    return pl.pallas_call(
        _cast_kernel,
        grid=(k // tk,),
        in_specs=[pl.BlockSpec((None, tk, n), lambda i: (l, i, 0))],
        out_specs=pl.BlockSpec((tk, n), lambda i: (i, 0)),
        out_shape=jax.ShapeDtypeStruct((k, n), BF16),
        compiler_params=_cparams(1),
        name="to_bf16",
    )(w)


def _residual_bf16_kernel(a_ref, w_ref, x_ref, gate_ref, o_ref):
    r = jnp.dot(a_ref[...], w_ref[...], preferred_element_type=F32)
    o_ref[...] = x_ref[...] + gate_ref[...] * r


def _residual_proj_deep(a, w, l, x, gate, name):
    m, k = a.shape
    n = w.shape[2]
    tm, tn = _tile(m, 512), _tile(n, 1024)
    return pl.pallas_call(
        _residual_bf16_kernel,
        grid=(n // tn, m // tm),
        in_specs=[pl.BlockSpec((tm, k), lambda j, i: (i, 0)),
                  pl.BlockSpec((k, tn), lambda j, i: (0, j)),
                  pl.BlockSpec((tm, tn), lambda j, i: (i, j)),
                  pl.BlockSpec((1, tn), lambda j, i: (0, j))],
        out_specs=pl.BlockSpec((tm, tn), lambda j, i: (i, j)),
        out_shape=jax.ShapeDtypeStruct((m, n), F32),
        compiler_params=_cparams(2),
        name=name,
    )(a, _to_bf16(w, l), x, gate)


def _residual_proj(a, w, l, x, gate, name):
    m, k = a.shape
    n = w.shape[2]
    t = next(t for t in (1024, 512, 256, 128)
             if 14 * k * t + 20 * t * t <= VMEM_LIMIT - 4 * 1024 * 1024)
    tm, tn = _tile(m, t), _tile(n, t)
    return pl.pallas_call(
        _residual_kernel,
        grid=(n // tn, m // tm),
        in_specs=[pl.BlockSpec((tm, k), lambda j, i: (i, 0)),
                  pl.BlockSpec((None, k, tn), lambda j, i: (l, 0, j)),
                  pl.BlockSpec((tm, tn), lambda j, i: (i, j)),
                  pl.BlockSpec((1, tn), lambda j, i: (0, j))],
        out_specs=pl.BlockSpec((tm, tn), lambda j, i: (i, j)),
        out_shape=jax.ShapeDtypeStruct((m, n), F32),
        scratch_shapes=[pltpu.VMEM((k, tn), BF16)],
        compiler_params=_cparams(2),
        name=name,
    )(a, w, x, gate)


def _ffn_up_kernel(a_ref, wg_ref, wv_ref, cwg_ref, cwv_ref, cbg_ref, cbv_ref, o_ref,
                   wg_bf_ref, wv_bf_ref, ug_ref, uv_ref, *, tm):
    i = pl.program_id(1)
    halo = SUBLANES

    @pl.when(i == 0)
    def _():
        wg_bf_ref[...] = wg_ref[...].astype(BF16)
        wv_bf_ref[...] = wv_ref[...].astype(BF16)
        ug_ref[0:halo, :] = jnp.zeros((halo, ug_ref.shape[1]), F32)
        uv_ref[0:halo, :] = jnp.zeros((halo, uv_ref.shape[1]), F32)

    a = a_ref[...]

    def conv(u_ref, wbf_ref, cw_ref, cb_ref):
        u_ref[halo:halo + tm, :] = jnp.dot(a, wbf_ref[...], preferred_element_type=F32)
        y = cb_ref[...]
        for t in range(CONV_W):
            shift = CONV_W - 1 - t
            y = y + cw_ref[t:t + 1, :] * u_ref[halo - shift:halo - shift + tm, :]
        u_ref[0:halo, :] = u_ref[tm:tm + halo, :]
        return y

    yg = conv(ug_ref, wg_bf_ref, cwg_ref, cbg_ref)
    yv = conv(uv_ref, wv_bf_ref, cwv_ref, cbv_ref)
    o_ref[...] = (yg * jax.nn.sigmoid(yg) * yv).astype(o_ref.dtype)


def _ffn_up(h2, w_up, conv_w, conv_b, l):
    m, k = h2.shape
    f = w_up.shape[2] // 2
    tm, tn = _tile(m, 1024), _tile(f, 512)
    nj = f // tn
    wspec = lambda half: pl.BlockSpec((None, k, tn), lambda j, i, half=half: (l, 0, half * nj + j))
    cwspec = lambda half: pl.BlockSpec((None, CONV_W, tn), lambda j, i, half=half: (l, 0, half * nj + j))
    cbspec = lambda half: pl.BlockSpec((None, 1, tn), lambda j, i, half=half: (l, 0, half * nj + j))
    return pl.pallas_call(
        functools.partial(_ffn_up_kernel, tm=tm),
        grid=(nj, m // tm),
        in_specs=[pl.BlockSpec((tm, k), lambda j, i: (i, 0)),
                  wspec(0), wspec(1), cwspec(0), cwspec(1), cbspec(0), cbspec(1)],
        out_specs=pl.BlockSpec((tm, tn), lambda j, i: (i, j)),
        out_shape=jax.ShapeDtypeStruct((m, f), BF16),
        scratch_shapes=[pltpu.VMEM((k, tn), BF16), pltpu.VMEM((k, tn), BF16),
                        pltpu.VMEM((tm + SUBLANES, tn), F32), pltpu.VMEM((tm + SUBLANES, tn), F32)],
        compiler_params=_cparams(2),
        name="ffn_up",
    )(h2, w_up, w_up, conv_w, conv_w, conv_b[:, None, :], conv_b[:, None, :])


def kernel(x, c, positions, w_ada, b_ada, norm_mix, w_in, w_gate, ret_gn_g, ret_gn_b, diff_qn, diff_kn, diff_on, lam_q1, lam_k1, lam_q2, lam_k2, w_branch, w_out, norm_ffn, w_up, conv_w, conv_b, w_down):
    batch, s, d = x.shape
    depth = w_ada.shape[0]
    x2d = x.reshape(batch * s, d)
    outs = []
    for b in range(batch):
        xb = x2d[b * s:(b + 1) * s]
        pos = positions[b]
        mod = _adaln(c[b:b + 1], w_ada, b_ada)
        tables = _rope_tables(pos)
        for l in range(depth):
            shift1, scale1, gate1, shift2, scale2, gate2 = [
                mod[l, :, t * d:(t + 1) * d] for t in range(6)]
            h = _norm_mod(xb, norm_mix[l], scale1, shift1)
            qkv = _proj(h, w_in, l, name="in_proj")
            gates = _proj(h, w_gate, l, act="sigmoid", name="gate_proj")
            rq, rk, dq, dk, sb_vt, d_vt = _prep(qkv, tables, diff_qn[l], diff_kn[l])
            y_sb = _sb_attention(qkv, sb_vt)
            y_ret = _retention(qkv, rq, rk, ret_gn_g[l], ret_gn_b[l])
            lam_init = 0.8 - 0.6 * float(np.exp(-0.3 * l))
            y_diff = _diff_attention(dq, dk, d_vt, lam_q1[l], lam_k1[l], lam_q2[l], lam_k2[l],
                                     diff_on[l], lam_init)
            merged = _merge(y_sb, y_ret, y_diff, gates, w_branch, l)
            xb = _residual_proj(merged, w_out, l, xb, gate1, name="out_proj")
            h2 = _norm_mod(xb, norm_ffn[l], scale2, shift2)
            act = _ffn_up(h2, w_up, conv_w, conv_b, l)
            xb = _residual_proj_deep(act, w_down, l, xb, gate2, name="ffn_down")
        outs.append(xb)
    return jnp.concatenate(outs, axis=0).reshape(batch, s, d)
```

```python
import functools

import jax
import jax.numpy as jnp
import numpy as np
from jax import lax
from jax.experimental import pallas as pl
from jax.experimental.pallas import tpu as pltpu

F32 = jnp.float32
BF16 = jnp.bfloat16

CHUNK = 64
CHUNK_SHIFT = CHUNK.bit_length() - 1
assert 1 << CHUNK_SHIFT == CHUNK
ROPE_THETA = 10000.0
EPS = 1e-6
SB_HEADS, SB_HD = 4, 128
RET_HEADS, RET_DK, RET_DV = 4, 128, 256
DIFF_HEADS, DIFF_HD = 4, 64
CONV_W = 3

SB_W = SB_HEADS * SB_HD
RET_QK = RET_HEADS * RET_DK
RET_V = RET_HEADS * RET_DV
DIFF_QK = DIFF_HEADS * 2 * DIFF_HD
DIFF_V = DIFF_HEADS * 2 * DIFF_HD
OFF_SB_Q = 0
OFF_SB_K = OFF_SB_Q + SB_W
OFF_SB_V = OFF_SB_K + SB_W
OFF_R_Q = OFF_SB_V + SB_W
OFF_R_K = OFF_R_Q + RET_QK
OFF_R_V = OFF_R_K + RET_QK
OFF_R_G = OFF_R_V + RET_V
OFF_D_Q = OFF_R_G + RET_V
OFF_D_K = OFF_D_Q + DIFF_QK
OFF_D_V = OFF_D_K + DIFF_QK
D_IN = OFF_D_V + DIFF_V

V7X_VMEM_BYTES = 64 * 1024 * 1024
VMEM_LIMIT = V7X_VMEM_BYTES - 8 * 1024 * 1024
LANES = 128
ATT_TK = 512
ATT_STRIP = 256
ATT_UNROLL = 4
SB_PROBE_BLOCKS = 2
SB_ZERO_LOG_WEIGHT = -104.0
SUBLANES = 8


def _cparams(n_axes, vmem=VMEM_LIMIT):
    return pltpu.CompilerParams(dimension_semantics=("arbitrary",) * n_axes,
                                vmem_limit_bytes=vmem)


def _tile(n, t):
    t = min(n, t)
    assert n % t == 0, (n, t)
    return t


def _adaln_kernel(c_ref, w_ref, b_ref, o_ref):
    c = c_ref[...]
    ca = c * jax.nn.sigmoid(c)
    o_ref[0] = jnp.sum(w_ref[0] * ca, axis=0, keepdims=True) + b_ref[0]


def _adaln(c, w_ada, b_ada):
    depth, d, n = w_ada.shape
    tn = _tile(n, 1024)
    return pl.pallas_call(
        _adaln_kernel,
        grid=(depth, n // tn),
        in_specs=[pl.BlockSpec((d, 1), lambda l, j: (0, 0)),
                  pl.BlockSpec((1, d, tn), lambda l, j: (l, 0, j)),
                  pl.BlockSpec((1, 1, tn), lambda l, j: (l, 0, j))],
        out_specs=pl.BlockSpec((1, 1, tn), lambda l, j: (l, 0, j)),
        out_shape=jax.ShapeDtypeStruct((depth, 1, n), F32),
        compiler_params=_cparams(2),
        name="adaln",
    )(c.reshape(d, 1), w_ada, b_ada.reshape(depth, 1, n))


def _norm_mod_kernel(x_ref, g_ref, scale_ref, shift_ref, o_ref):
    x = x_ref[...]
    y = x * lax.rsqrt(jnp.mean(x * x, axis=-1, keepdims=True) + EPS)
    y = y * g_ref[...]
    o_ref[...] = (y * (1.0 + scale_ref[...]) + shift_ref[...]).astype(o_ref.dtype)


def _norm_mod(x, g, scale, shift):
    s, d = x.shape
    tm = _tile(s, 512)
    row = pl.BlockSpec((1, d), lambda i: (0, 0))
    return pl.pallas_call(
        _norm_mod_kernel,
        grid=(s // tm,),
        in_specs=[pl.BlockSpec((tm, d), lambda i: (i, 0)), row, row, row],
        out_specs=pl.BlockSpec((tm, d), lambda i: (i, 0)),
        out_shape=jax.ShapeDtypeStruct((s, d), BF16),
        compiler_params=_cparams(1),
        name="norm_mod",
    )(x, g.reshape(1, d), scale, shift)


def _proj_kernel(a_ref, w_ref, o_ref, wbf_ref, *, act):
    @pl.when(pl.program_id(1) == 0)
    def _():
        wbf_ref[...] = w_ref[...].astype(BF16)

    r = jnp.dot(a_ref[...], wbf_ref[...], preferred_element_type=F32)
    if act == "sigmoid":
        r = jax.nn.sigmoid(r)
    o_ref[...] = r.astype(o_ref.dtype)


def _proj(a, w, l, act=None, name="proj"):
    m, k = a.shape
    n = w.shape[2]
    tm, tn = _tile(m, 2048), _tile(n, 1024)
    return pl.pallas_call(
        functools.partial(_proj_kernel, act=act),
        grid=(n // tn, m // tm),
        in_specs=[pl.BlockSpec((tm, k), lambda j, i: (i, 0)),
                  pl.BlockSpec((None, k, tn), lambda j, i: (l, 0, j))],
        out_specs=pl.BlockSpec((tm, tn), lambda j, i: (i, j)),
        out_shape=jax.ShapeDtypeStruct((m, n), BF16),
        scratch_shapes=[pltpu.VMEM((k, tn), BF16)],
        compiler_params=_cparams(2),
        name=name,
    )(a, w)


def _rope_tables_kernel(pos_ref, inv_ref, cos_r_ref, sin_r_ref, cos_d_ref, sin_d_ref):
    pos = pos_ref[...].astype(F32)
    lane = lax.broadcasted_iota(jnp.int32, (pos.shape[0], LANES), 1)
    ang = pos * inv_ref[...]
    c, s = jnp.cos(ang), jnp.sin(ang)
    half_r, half_d = RET_DK // 2, DIFF_HD // 2

    def ret_table(t):
        return jnp.where(lane < half_r, t, pltpu.roll(t, half_r, 1))

    def diff_table(t):
        d = pltpu.roll(t, LANES - half_r, 1)
        e = jnp.where(lane < half_d, d, pltpu.roll(d, half_d, 1))
        return jnp.where(lane < 2 * half_d, e, pltpu.roll(e, 2 * half_d, 1))

    cos_r_ref[...] = ret_table(c)
    sin_r_ref[...] = jnp.where(lane < half_r, -1.0, 1.0) * ret_table(s)
    cos_d_ref[...] = diff_table(c)
    sin_d_ref[...] = jnp.where((lane & (DIFF_HD - 1)) < half_d, -1.0, 1.0) * diff_table(s)


def _rope_tables(positions):
    s = positions.shape[0]
    tm = _tile(s, 512)
    assert RET_DK == LANES and 2 * DIFF_HD == LANES
    inv_r = ROPE_THETA ** (-jnp.arange(0, RET_DK, 2, dtype=F32) / RET_DK)
    inv_d = ROPE_THETA ** (-jnp.arange(0, DIFF_HD, 2, dtype=F32) / DIFF_HD)
    pad = jnp.zeros((LANES - RET_DK // 2 - DIFF_HD // 2,), F32)
    inv = jnp.concatenate([inv_r, inv_d, pad]).reshape(1, LANES)
    table = pl.BlockSpec((tm, LANES), lambda i: (i, 0))
    return pl.pallas_call(
        _rope_tables_kernel,
        grid=(s // tm,),
        in_specs=[pl.BlockSpec((tm, 1), lambda i: (i, 0)), pl.BlockSpec((1, LANES), lambda i: (0, 0))],
        out_specs=[table] * 4,
        out_shape=[jax.ShapeDtypeStruct((s, LANES), F32)] * 4,
        compiler_params=_cparams(1),
        name="rope_tables",
    )(positions.reshape(s, 1), inv)


def _prep_kernel(cos_r_ref, sin_r_ref, cos_d_ref, sin_d_ref, qn_ref, kn_ref,
                 rq_ref, rk_ref, dq_ref, dk_ref, sbv_ref, dv_ref,
                 orq_ref, ork_ref, odq_ref, odk_ref, osbvt_ref, odvt_ref):
    tm = rq_ref.shape[0]
    lane = lax.broadcasted_iota(jnp.int32, (tm, LANES), 1)

    for v_ref, ovt_ref in ((sbv_ref, osbvt_ref), (dv_ref, odvt_ref)):
        for h in range(ovt_ref.shape[0]):
            vt = v_ref[:, h * LANES:(h + 1) * LANES].astype(F32).T
            for cb in range(tm // ATT_TK):
                ovt_ref[h, cb] = vt[:, cb * ATT_TK:(cb + 1) * ATT_TK].astype(ovt_ref.dtype)

    cos_r, sin_r = cos_r_ref[...], sin_r_ref[...]
    k_scale = RET_DK ** -0.5
    for h in range(RET_HEADS):
        sl = slice(h * RET_DK, (h + 1) * RET_DK)
        xq = rq_ref[:, sl].astype(F32)
        xk = rk_ref[:, sl].astype(F32)
        orq_ref[:, sl] = (xq * cos_r + pltpu.roll(xq, RET_DK // 2, 1) * sin_r).astype(orq_ref.dtype)
        ork_ref[:, sl] = ((xk * cos_r + pltpu.roll(xk, RET_DK // 2, 1) * sin_r) * k_scale).astype(ork_ref.dtype)

    first_half = (lane & (DIFF_HD - 1)) < DIFF_HD // 2
    cos_d, sin_d = cos_d_ref[...], sin_d_ref[...]
    low_map = lane < DIFF_HD
    q_scale = DIFF_HD ** -0.5

    def qk_norm_rope(x, gain):
        x2 = x * x
        ms_lo = jnp.sum(jnp.where(low_map, x2, 0.0), axis=-1, keepdims=True) * (1.0 / DIFF_HD)
        ms_hi = jnp.sum(jnp.where(low_map, 0.0, x2), axis=-1, keepdims=True) * (1.0 / DIFF_HD)
        r = jnp.where(low_map, lax.rsqrt(ms_lo + EPS), lax.rsqrt(ms_hi + EPS))
        y = x * r * gain
        partner = jnp.where(first_half, pltpu.roll(y, LANES - DIFF_HD // 2, 1),
                            pltpu.roll(y, DIFF_HD // 2, 1))
        return y * cos_d + partner * sin_d

    for h in range(DIFF_HEADS):
        sl = slice(h * 2 * DIFF_HD, (h + 1) * 2 * DIFF_HD)
        odq_ref[:, sl] = (qk_norm_rope(dq_ref[:, sl].astype(F32), qn_ref[...]) * q_scale).astype(odq_ref.dtype)
        odk_ref[:, sl] = qk_norm_rope(dk_ref[:, sl].astype(F32), kn_ref[...]).astype(odk_ref.dtype)


def _prep(qkv, tables, qn, kn):
    s = qkv.shape[0]
    tm = _tile(s, 512)
    w = 512
    assert RET_QK == w and DIFF_QK == w
    col = lambda off: pl.BlockSpec((tm, w), lambda i, off=off: (i, off // w))
    row = pl.BlockSpec((1, LANES), lambda i: (0, 0))
    table = pl.BlockSpec((tm, LANES), lambda i: (i, 0))
    out_spec = pl.BlockSpec((tm, w), lambda i: (i, 0))
    out = jax.ShapeDtypeStruct((s, w), BF16)
    assert SB_HD == LANES and 2 * DIFF_HD == LANES and SB_W == w and DIFF_V == w
    nkb = tm // ATT_TK
    vt_spec = pl.BlockSpec((SB_HEADS, nkb, LANES, ATT_TK), lambda i: (0, i, 0, 0))
    vt_out = jax.ShapeDtypeStruct((SB_HEADS, s // ATT_TK, LANES, ATT_TK), BF16)
    return pl.pallas_call(
        _prep_kernel,
        grid=(s // tm,),
        in_specs=[table, table, table, table, row, row,
                  col(OFF_R_Q), col(OFF_R_K), col(OFF_D_Q), col(OFF_D_K),
                  col(OFF_SB_V), col(OFF_D_V)],
        out_specs=[out_spec] * 4 + [vt_spec] * 2,
        out_shape=[out] * 4 + [vt_out] * 2,
        compiler_params=_cparams(1),
        name="rope_prep",
    )(*tables, jnp.tile(qn, 2).reshape(1, LANES), jnp.tile(kn, 2).reshape(1, LANES),
      qkv, qkv, qkv, qkv, qkv, qkv)


def _sb_kernel(q_ref, k_ref, vt_ref, o_ref, acc_ref, csum_ref, z_ref, lb_ref, later_ref, lf0_ref,
               w_ref, *, tq, tk, scale):
    i = pl.program_id(1)
    st = ATT_STRIP
    n_strip, n_half = tq // st, tk // st
    assert tq == tk
    acc_ref[...] = jnp.zeros_like(acc_ref)
    csum_ref[...] = jnp.zeros_like(csum_ref)
    row = lax.broadcasted_iota(jnp.int32, (st, st), 0)
    col = lax.broadcasted_iota(jnp.int32, (st, st), 1)
    tri = jnp.where(col > row, 1.0, 0.0).astype(BF16)

    def scores(c, kb, hf):
        ks = pl.multiple_of(kb * tk, tk)
        k = k_ref[pl.ds(ks + hf * st, st), :]
        return lax.dot_general(k, q_ref[c * st:(c + 1) * st, :], (((1,), (1,)), ((), ())),
                               preferred_element_type=F32)

    def log_probs(raw, masked):
        z = raw * scale
        sp = jnp.log(1.0 + jnp.exp(-jnp.abs(z)))
        log_beta = jnp.minimum(z, 0.0) - sp
        log_fail = log_beta - z
        if masked:
            log_fail = jnp.where(row < col, log_fail, 0.0)
        return log_beta, log_fail.astype(BF16), log_fail[0:1, :]

    def later_sum(lf_bf):
        return jnp.dot(tri, lf_bf, preferred_element_type=F32)

    def weights(c, hf, log_beta, later, lf0, masked):
        lanes = slice(c * st, (c + 1) * st)
        csum = csum_ref[:, lanes]
        w = jnp.exp(log_beta + later + csum)
        if masked:
            w = jnp.where(row < col, w, 0.0)
        w_ref[hf * st:(hf + 1) * st, :] = w.astype(BF16)
        csum_ref[:, lanes] = csum + later[0:1, :] + lf0

    def accumulate(c, kb, n_live):
        lanes = slice(c * st, (c + 1) * st)
        acc_ref[:, lanes] += jnp.dot(vt_ref[kb, :, 0:n_live * st], w_ref[0:n_live * st, :],
                                     preferred_element_type=F32)

    diag_steps = [(c, hf, hf == c, c + 1 if hf == 0 else None)
                  for c in range(n_strip) for hf in reversed(range(c + 1))]
    block_steps = [(c, hf, False, n_half if hf == 0 else None)
                   for c in range(n_strip) for hf in reversed(range(n_half))]
    assert len(block_steps) % 2 == 0

    def stage23(p, raw, masked):
        log_beta, lf_bf, lf0 = log_probs(raw, masked)
        lb_ref[p], lf0_ref[p] = log_beta, lf0
        later_ref[p] = later_sum(lf_bf)

    def run(section, parity):
        for n in range(len(section) - 2):
            p = (parity + n) % 2
            ((c, hf, masked, n_live), kb), ((_, _, masked1, _), _), ((c2, hf2, _, _), kb2) = section[n:n + 3]
            z_ref[p] = scores(c2, kb2, hf2)
            stage23(1 - p, z_ref[1 - p], masked1)
            weights(c, hf, lb_ref[p], later_ref[p], lf0_ref[p], masked)
            if n_live:
                accumulate(c, kb, n_live)

    def at(steps, kb):
        return [(step, kb) for step in steps]

    (c0, hf0, masked0, _), (c1, hf1, _, _) = diag_steps[:2]
    stage23(0, scores(c0, i, hf0), masked0)
    z_ref[1] = scores(c1, i, hf1)
    run(at(diag_steps, i) + at(block_steps[:2], jnp.maximum(i - 1, 0)), 0)

    def sweep(first_kb, n_blocks):
        section = []
        for b in range(n_blocks):
            section += at(block_steps, first_kb - b)
        run(section + at(block_steps[:2], jnp.maximum(first_kb - n_blocks, 0)), len(diag_steps))

    def live():
        return (jnp.max(csum_ref[...]) >= SB_ZERO_LOG_WEIGHT).astype(jnp.int32)

    def sweep_while(first_kb, n_trips, n_blocks):
        def cond(state):
            t, alive = state
            return jnp.logical_and(t < n_trips, alive > 0)

        def body(state):
            t, _ = state
            sweep(first_kb - n_blocks * t, n_blocks)
            return t + 1, live()

        lax.while_loop(cond, body, (jnp.int32(0), live()))

    n_probe = jnp.minimum(i, SB_PROBE_BLOCKS)
    n_rest = i - n_probe
    sweep_while(i - 1, n_probe, 1)
    sweep_while(n_rest - 1, n_rest // ATT_UNROLL, ATT_UNROLL)
    sweep_while(n_rest % ATT_UNROLL - 1, n_rest % ATT_UNROLL, 1)

    o_ref[...] = acc_ref[...].T.astype(o_ref.dtype)


def _sb_attention(qkv, vt):
    s = qkv.shape[0]
    d = SB_HD
    tq = tk = ATT_TK
    return pl.pallas_call(
        functools.partial(_sb_kernel, tq=tq, tk=tk, scale=d ** -0.5),
        grid=(SB_HEADS, s // tq),
        in_specs=[pl.BlockSpec((tq, d), lambda h, i: (i, OFF_SB_Q // d + h)),
                  pl.BlockSpec((s, d), lambda h, i: (0, OFF_SB_K // d + h)),
                  pl.BlockSpec((None, s // tk, d, tk), lambda h, i: (h, 0, 0, 0))],
        out_specs=pl.BlockSpec((tq, d), lambda h, i: (i, h)),
        out_shape=jax.ShapeDtypeStruct((s, SB_W), BF16),
        scratch_shapes=[pltpu.VMEM((d, tq), F32), pltpu.VMEM((1, tq), F32),
                        pltpu.VMEM((2, ATT_STRIP, ATT_STRIP), F32),
                        pltpu.VMEM((2, ATT_STRIP, ATT_STRIP), F32),
                        pltpu.VMEM((2, ATT_STRIP, ATT_STRIP), F32),
                        pltpu.VMEM((2, 1, ATT_STRIP), F32),
                        pltpu.VMEM((tk, ATT_STRIP), BF16)],
        compiler_params=_cparams(2),
        name="sb_attention",
    )(qkv, qkv, vt)


def _ret_kernel(lg_ref, q_ref, k_ref, v0_ref, v1_ref, g0_ref, g1_ref, gn_g_ref, gn_b_ref, o_ref,
                state_ref, decay_ref, *, tl):
    heads = range(RET_HEADS)
    per_block = RET_HEADS // 2
    r = lax.broadcasted_iota(jnp.int32, (tl, 1), 0).astype(F32)

    def wide(refs, h):
        c0 = (h % per_block) * RET_DV
        return refs[h // per_block][:, c0:c0 + RET_DV]

    @pl.when(pl.program_id(0) == 0)
    def _():
        state_ref[...] = jnp.zeros_like(state_ref)
        n = lax.broadcasted_iota(jnp.int32, (tl, tl), 0)
        m = lax.broadcasted_iota(jnp.int32, (tl, tl), 1)
        dist = jnp.abs(n - m).astype(F32)
        visible = (m >> CHUNK_SHIFT) <= (n >> CHUNK_SHIFT)
        for h in heads:
            decay_ref[h] = jnp.where(visible, jnp.exp(dist * lg_ref[h]), 0.0)

    qs = [q_ref[:, h * RET_DK:(h + 1) * RET_DK] for h in heads]
    ks = [k_ref[:, h * RET_DK:(h + 1) * RET_DK] for h in heads]
    vs = [wide((v0_ref, v1_ref), h) for h in heads]
    scores = [lax.dot_general(qs[h], ks[h], (((1,), (1,)), ((), ())), preferred_element_type=F32)
              for h in heads]
    cross = [jnp.dot(qs[h], state_ref[h].astype(BF16), preferred_element_type=F32) for h in heads]
    pushed = []
    for h in heads:
        zeta = jnp.exp((tl - 1.0 - r) * lg_ref[h])
        kz = (ks[h].astype(F32) * zeta).astype(BF16)
        pushed.append(lax.dot_general(kz, vs[h], (((0,), (0,)), ((), ())),
                                      preferred_element_type=F32))
    intra = [jnp.dot((scores[h] * decay_ref[h]).astype(BF16), vs[h], preferred_element_type=F32)
             for h in heads]
    for h in heads:
        lg = lg_ref[h]
        block_decay = jnp.exp(jnp.full((1, 1), tl, F32) * lg)
        state_ref[h] = state_ref[h] * block_decay + pushed[h]
        y = intra[h] + cross[h] * jnp.exp((r + 1.0) * lg)
        mu = jnp.mean(y, axis=-1, keepdims=True)
        yc = y - mu
        var = jnp.mean(yc * yc, axis=-1, keepdims=True)
        cols = slice(h * RET_DV, (h + 1) * RET_DV)
        yn = yc * lax.rsqrt(var + EPS) * gn_g_ref[:, cols] + gn_b_ref[:, cols]
        gate = wide((g0_ref, g1_ref), h).astype(F32)
        o_ref[:, cols] = (yn * (gate * jax.nn.sigmoid(gate))).astype(o_ref.dtype)


def _retention(qkv, rq, rk, gn_g, gn_b):
    s = qkv.shape[0]
    tl = _tile(s, 256)
    log_gamma = jnp.log(1.0 - 2.0 ** (-5.0 - jnp.arange(RET_HEADS, dtype=F32)))
    half = RET_V // 2
    assert RET_QK == half and OFF_R_V % half == 0 and OFF_R_G % half == 0 and RET_HEADS % 2 == 0
    col = lambda off: pl.BlockSpec((tl, half), lambda b, off=off: (b, off // half))
    row = pl.BlockSpec((1, RET_V), lambda b: (0, 0))
    return pl.pallas_call(
        functools.partial(_ret_kernel, tl=tl),
        grid=(s // tl,),
        in_specs=[pl.BlockSpec(memory_space=pltpu.SMEM), col(0), col(0),
                  col(OFF_R_V), col(OFF_R_V + half), col(OFF_R_G), col(OFF_R_G + half), row, row],
        out_specs=pl.BlockSpec((tl, RET_V), lambda b: (b, 0)),
        out_shape=jax.ShapeDtypeStruct((s, RET_V), BF16),
        scratch_shapes=[pltpu.VMEM((RET_HEADS, RET_DK, RET_DV), F32),
                        pltpu.VMEM((RET_HEADS, tl, tl), F32)],
        compiler_params=_cparams(1),
        name="retention",
    )(log_gamma, rq, rk, qkv, qkv, qkv, qkv, gn_g.reshape(1, RET_V), gn_b.reshape(1, RET_V))


def _diff_kernel(lq1_ref, lk1_ref, lq2_ref, lk2_ref, on_ref, q_ref, k_ref, vt_ref, o_ref,
                 q2_ref, m_ref, l_ref, acc_ref, s0_ref, s1_ref, *, tq, tk, lam_init):
    i = pl.program_id(1)
    st = ATT_STRIP
    assert tq == tk and tq == 2 * st
    q = q_ref[...]
    lane = lax.broadcasted_iota(jnp.int32, q.shape, 1)
    zero = jnp.zeros_like(q)
    q2_ref[0:tq, :] = jnp.where(lane < DIFF_HD, q, zero)
    q2_ref[tq:2 * tq, :] = jnp.where(lane < DIFF_HD, zero, q)
    m_ref[...] = jnp.full_like(m_ref, -jnp.inf)
    l_ref[...] = jnp.zeros_like(l_ref)
    acc_ref[...] = jnp.zeros_like(acc_ref)

    n_strip = 2 * tq // st
    s_bufs = (s0_ref, s1_ref)

    def scores(c, kb, nk):
        ks = pl.multiple_of(kb * tk, tk)
        qc = q2_ref[c * st:(c + 1) * st, :]
        parts = [lax.dot_general(k_ref[pl.ds(ks + r, st), :], qc, (((1,), (1,)), ((), ())),
                                 preferred_element_type=F32) for r in range(0, nk, st)]
        return parts[0] if len(parts) == 1 else jnp.concatenate(parts, axis=0)

    def softmax_pv(c, kb, nk, s, diagonal):
        lanes = slice(c * st, (c + 1) * st)
        if diagonal:
            kchunk = lax.broadcasted_iota(jnp.int32, (nk, st), 0) >> CHUNK_SHIFT
            qchunk = ((c % 2) * st + lax.broadcasted_iota(jnp.int32, (nk, st), 1)) >> CHUNK_SHIFT
            s = jnp.where(kchunk <= qchunk, s, -jnp.inf)
        m_old = m_ref[:, lanes]
        m_new = jnp.maximum(m_old, jnp.max(s, axis=0, keepdims=True))
        alpha = jnp.exp(m_old - m_new)
        p = jnp.exp(s - m_new)
        l_ref[:, lanes] = alpha * l_ref[:, lanes] + jnp.sum(p, axis=0, keepdims=True)
        acc_ref[:, lanes] = alpha * acc_ref[:, lanes] + jnp.dot(
            vt_ref[kb, :, 0:nk], p.astype(BF16), preferred_element_type=F32)
        m_ref[:, lanes] = m_new

    def key_block(kb, diagonal):
        for c in range(n_strip):
            nk = (c % 2 + 1) * st if diagonal else tk
            if c + 1 < n_strip:
                s_bufs[(c + 1) % 2][...] = scores(c + 1, kb, tk)
            elif not diagonal:
                s_bufs[0][...] = scores(0, kb + 1, tk)
            softmax_pv(c, kb, nk, s_bufs[c % 2][0:nk, :], diagonal)

    assert n_strip % 2 == 0
    s_bufs[0][...] = scores(0, 0, tk)

    def body(t, carry):
        for b in range(ATT_UNROLL):
            key_block(ATT_UNROLL * t + b, False)
        return carry

    def tail_body(kb, carry):
        key_block(kb, False)
        return carry

    n_main = i // ATT_UNROLL
    lax.fori_loop(0, n_main, body, 0)
    lax.fori_loop(n_main * ATT_UNROLL, i, tail_body, 0)
    key_block(i, True)

    lam = (jnp.exp(jnp.sum(lq1_ref[...] * lk1_ref[...], axis=-1, keepdims=True))
           - jnp.exp(jnp.sum(lq2_ref[...] * lk2_ref[...], axis=-1, keepdims=True)) + lam_init)
    o = acc_ref[...] / l_ref[...]
    a = o[:, 0:tq] - lam * o[:, tq:2 * tq]
    a = a * lax.rsqrt(jnp.mean(a * a, axis=0, keepdims=True) + EPS)
    o_ref[...] = (a.T * on_ref[...] * (1.0 - lam_init)).astype(o_ref.dtype)


def _diff_attention(dq, dk, vt, lq1, lk1, lq2, lk2, on, lam_init):
    s = dq.shape[0]
    dv = 2 * DIFF_HD
    tq = tk = ATT_TK
    lam_spec = pl.BlockSpec((1, DIFF_HD), lambda h, i: (0, 0))
    return pl.pallas_call(
        functools.partial(_diff_kernel, tq=tq, tk=tk, lam_init=lam_init),
        grid=(DIFF_HEADS, s // tq),
        in_specs=[lam_spec, lam_spec, lam_spec, lam_spec,
                  pl.BlockSpec((1, dv), lambda h, i: (0, 0)),
                  pl.BlockSpec((tq, dv), lambda h, i: (i, h)),
                  pl.BlockSpec((s, dv), lambda h, i: (0, h)),
                  pl.BlockSpec((None, s // tk, dv, tk), lambda h, i: (h, 0, 0, 0))],
        out_specs=pl.BlockSpec((tq, dv), lambda h, i: (i, h)),
        out_shape=jax.ShapeDtypeStruct((s, DIFF_V), BF16),
        scratch_shapes=[pltpu.VMEM((2 * tq, dv), BF16), pltpu.VMEM((1, 2 * tq), F32),
                        pltpu.VMEM((1, 2 * tq), F32), pltpu.VMEM((dv, 2 * tq), F32),
                        pltpu.VMEM((tk, ATT_STRIP), F32), pltpu.VMEM((tk, ATT_STRIP), F32)],
        compiler_params=_cparams(2),
        name="diff_attention",
    )(lq1.reshape(1, DIFF_HD), lk1.reshape(1, DIFF_HD), lq2.reshape(1, DIFF_HD),
      lk2.reshape(1, DIFF_HD), on.reshape(1, dv), dq, dk, vt)


def _merge_kernel(ysb_ref, yret_ref, ydiff_ref, gsb_ref, gret_ref, gdiff_ref, w_ref, o_ref, wbf_ref):
    @pl.when(pl.program_id(1) == 0)
    def _():
        wbf_ref[...] = w_ref[...].astype(BF16)

    r_sb = jnp.dot(ysb_ref[...], wbf_ref[0:SB_W, :], preferred_element_type=F32)
    r_ret = jnp.dot(yret_ref[...], wbf_ref[SB_W:SB_W + RET_V, :], preferred_element_type=F32)
    r_diff = jnp.dot(ydiff_ref[...], wbf_ref[SB_W + RET_V:SB_W + RET_V + DIFF_V, :],
                     preferred_element_type=F32)
    merged = (gsb_ref[...].astype(F32) * r_sb + gret_ref[...].astype(F32) * r_ret
              + gdiff_ref[...].astype(F32) * r_diff)
    o_ref[...] = merged.astype(o_ref.dtype)


def _merge(y_sb, y_ret, y_diff, gates, w_branch, l):
    s = y_sb.shape[0]
    _, kw, n = w_branch.shape
    tm, tn = _tile(s, 1024), _tile(n, 1024)
    nj = n // tn
    gate = lambda b: pl.BlockSpec((tm, tn), lambda j, i, b=b: (i, b * nj + j))
    return pl.pallas_call(
        _merge_kernel,
        grid=(nj, s // tm),
        in_specs=[pl.BlockSpec((tm, SB_W), lambda j, i: (i, 0)),
                  pl.BlockSpec((tm, RET_V), lambda j, i: (i, 0)),
                  pl.BlockSpec((tm, DIFF_V), lambda j, i: (i, 0)),
                  gate(0), gate(1), gate(2),
                  pl.BlockSpec((None, kw, tn), lambda j, i: (l, 0, j))],
        out_specs=pl.BlockSpec((tm, tn), lambda j, i: (i, j)),
        out_shape=jax.ShapeDtypeStruct((s, n), BF16),
        scratch_shapes=[pltpu.VMEM((kw, tn), BF16)],
        compiler_params=_cparams(2),
        name="merge",
    )(y_sb, y_ret, y_diff, gates, gates, gates, w_branch)


def _residual_kernel(a_ref, w_ref, x_ref, gate_ref, o_ref, wbf_ref):
    @pl.when(pl.program_id(1) == 0)
    def _():
        wbf_ref[...] = w_ref[...].astype(BF16)

    r = jnp.dot(a_ref[...], wbf_ref[...], preferred_element_type=F32)
    o_ref[...] = x_ref[...] + gate_ref[...] * r


def _cast_kernel(w_ref, o_ref):
    o_ref[...] = w_ref[...].astype(o_ref.dtype)


def _to_bf16(w, l):
    _, k, n = w.shape
    tk = _tile(k, 512)
    return pl.pallas_call(
        _cast_kernel,
        grid=(k // tk,),
        in_specs=[pl.BlockSpec((None, tk, n), lambda i: (l, i, 0))],
        out_specs=pl.BlockSpec((tk, n), lambda i: (i, 0)),
        out_shape=jax.ShapeDtypeStruct((k, n), BF16),
        compiler_params=_cparams(1),
        name="to_bf16",
    )(w)


def _residual_bf16_kernel(a_ref, w_ref, x_ref, gate_ref, o_ref):
    r = jnp.dot(a_ref[...], w_ref[...], preferred_element_type=F32)
    o_ref[...] = x_ref[...] + gate_ref[...] * r


def _residual_proj_deep(a, w, l, x, gate, name):
    m, k = a.shape
    n = w.shape[2]
    tm, tn = _tile(m, 512), _tile(n, 1024)
    return pl.pallas_call(
        _residual_bf16_kernel,
        grid=(n // tn, m // tm),
        in_specs=[pl.BlockSpec((tm, k), lambda j, i: (i, 0)),
                  pl.BlockSpec((k, tn), lambda j, i: (0, j)),
                  pl.BlockSpec((tm, tn), lambda j, i: (i, j)),
                  pl.BlockSpec((1, tn), lambda j, i: (0, j))],
        out_specs=pl.BlockSpec((tm, tn), lambda j, i: (i, j)),
        out_shape=jax.ShapeDtypeStruct((m, n), F32),
        compiler_params=_cparams(2),
        name=name,
    )(a, _to_bf16(w, l), x, gate)


def _residual_proj(a, w, l, x, gate, name):
    m, k = a.shape
    n = w.shape[2]
    t = next(t for t in (1024, 512, 256, 128)
             if 14 * k * t + 20 * t * t <= VMEM_LIMIT - 4 * 1024 * 1024)
    tm, tn = _tile(m, t), _tile(n, t)
    return pl.pallas_call(
        _residual_kernel,
        grid=(n // tn, m // tm),
        in_specs=[pl.BlockSpec((tm, k), lambda j, i: (i, 0)),
                  pl.BlockSpec((None, k, tn), lambda j, i: (l, 0, j)),
                  pl.BlockSpec((tm, tn), lambda j, i: (i, j)),
                  pl.BlockSpec((1, tn), lambda j, i: (0, j))],
        out_specs=pl.BlockSpec((tm, tn), lambda j, i: (i, j)),
        out_shape=jax.ShapeDtypeStruct((m, n), F32),
        scratch_shapes=[pltpu.VMEM((k, tn), BF16)],
        compiler_params=_cparams(2),
        name=name,
    )(a, w, x, gate)


def _ffn_up_kernel(a_ref, wg_ref, wv_ref, cwg_ref, cwv_ref, cbg_ref, cbv_ref, o_ref,
                   wg_bf_ref, wv_bf_ref, ug_ref, uv_ref, *, tm):
    i = pl.program_id(1)
    halo = SUBLANES

    @pl.when(i == 0)
    def _():
        wg_bf_ref[...] = wg_ref[...].astype(BF16)
        wv_bf_ref[...] = wv_ref[...].astype(BF16)
        ug_ref[0:halo, :] = jnp.zeros((halo, ug_ref.shape[1]), F32)
        uv_ref[0:halo, :] = jnp.zeros((halo, uv_ref.shape[1]), F32)

    a = a_ref[...]

    def conv(u_ref, wbf_ref, cw_ref, cb_ref):
        u_ref[halo:halo + tm, :] = jnp.dot(a, wbf_ref[...], preferred_element_type=F32)
        y = cb_ref[...]
        for t in range(CONV_W):
            shift = CONV_W - 1 - t
            y = y + cw_ref[t:t + 1, :] * u_ref[halo - shift:halo - shift + tm, :]
        u_ref[0:halo, :] = u_ref[tm:tm + halo, :]
        return y

    yg = conv(ug_ref, wg_bf_ref, cwg_ref, cbg_ref)
    yv = conv(uv_ref, wv_bf_ref, cwv_ref, cbv_ref)
    o_ref[...] = (yg * jax.nn.sigmoid(yg) * yv).astype(o_ref.dtype)


def _ffn_up(h2, w_up, conv_w, conv_b, l):
    m, k = h2.shape
    f = w_up.shape[2] // 2
    tm, tn = _tile(m, 1024), _tile(f, 512)
    nj = f // tn
    wspec = lambda half: pl.BlockSpec((None, k, tn), lambda j, i, half=half: (l, 0, half * nj + j))
    cwspec = lambda half: pl.BlockSpec((None, CONV_W, tn), lambda j, i, half=half: (l, 0, half * nj + j))
    cbspec = lambda half: pl.BlockSpec((None, 1, tn), lambda j, i, half=half: (l, 0, half * nj + j))
    return pl.pallas_call(
        functools.partial(_ffn_up_kernel, tm=tm),
        grid=(nj, m // tm),
        in_specs=[pl.BlockSpec((tm, k), lambda j, i: (i, 0)),
                  wspec(0), wspec(1), cwspec(0), cwspec(1), cbspec(0), cbspec(1)],
        out_specs=pl.BlockSpec((tm, tn), lambda j, i: (i, j)),
        out_shape=jax.ShapeDtypeStruct((m, f), BF16),
        scratch_shapes=[pltpu.VMEM((k, tn), BF16), pltpu.VMEM((k, tn), BF16),
                        pltpu.VMEM((tm + SUBLANES, tn), F32), pltpu.VMEM((tm + SUBLANES, tn), F32)],
        compiler_params=_cparams(2),
        name="ffn_up",
    )(h2, w_up, w_up, conv_w, conv_w, conv_b[:, None, :], conv_b[:, None, :])


def kernel(x, c, positions, w_ada, b_ada, norm_mix, w_in, w_gate, ret_gn_g, ret_gn_b, diff_qn, diff_kn, diff_on, lam_q1, lam_k1, lam_q2, lam_k2, w_branch, w_out, norm_ffn, w_up, conv_w, conv_b, w_down):
    batch, s, d = x.shape
    depth = w_ada.shape[0]
    x2d = x.reshape(batch * s, d)
    outs = []
    for b in range(batch):
        xb = x2d[b * s:(b + 1) * s]
        pos = positions[b]
        mod = _adaln(c[b:b + 1], w_ada, b_ada)
        tables = _rope_tables(pos)
        for l in range(depth):
            shift1, scale1, gate1, shift2, scale2, gate2 = [
                mod[l, :, t * d:(t + 1) * d] for t in range(6)]
            h = _norm_mod(xb, norm_mix[l], scale1, shift1)
            qkv = _proj(h, w_in, l, name="in_proj")
            gates = _proj(h, w_gate, l, act="sigmoid", name="gate_proj")
            rq, rk, dq, dk, sb_vt, d_vt = _prep(qkv, tables, diff_qn[l], diff_kn[l])
            y_sb = _sb_attention(qkv, sb_vt)
            y_ret = _retention(qkv, rq, rk, ret_gn_g[l], ret_gn_b[l])
            lam_init = 0.8 - 0.6 * float(np.exp(-0.3 * l))
            y_diff = _diff_attention(dq, dk, d_vt, lam_q1[l], lam_k1[l], lam_q2[l], lam_k2[l],
                                     diff_on[l], lam_init)
            merged = _merge(y_sb, y_ret, y_diff, gates, w_branch, l)
            xb = _residual_proj(merged, w_out, l, xb, gate1, name="out_proj")
            h2 = _norm_mod(xb, norm_ffn[l], scale2, shift2)
            act = _ffn_up(h2, w_up, conv_w, conv_b, l)
            xb = _residual_proj_deep(act, w_down, l, xb, gate2, name="ffn_down")
        outs.append(xb)
    return jnp.concatenate(outs, axis=0).reshape(batch, s, d)
```

```python
import functools

import jax
import jax.numpy as jnp
import numpy as np
from jax import lax
from jax.experimental import pallas as pl
from jax.experimental.pallas import tpu as pltpu

F32 = jnp.float32
BF16 = jnp.bfloat16

CHUNK = 64
CHUNK_SHIFT = CHUNK.bit_length() - 1
assert 1 << CHUNK_SHIFT == CHUNK
ROPE_THETA = 10000.0
EPS = 1e-6
SB_HEADS, SB_HD = 4, 128
RET_HEADS, RET_DK, RET_DV = 4, 128, 256
DIFF_HEADS, DIFF_HD = 4, 64
CONV_W = 3

SB_W = SB_HEADS * SB_HD
RET_QK = RET_HEADS * RET_DK
RET_V = RET_HEADS * RET_DV
DIFF_QK = DIFF_HEADS * 2 * DIFF_HD
DIFF_V = DIFF_HEADS * 2 * DIFF_HD
OFF_SB_Q = 0
OFF_SB_K = OFF_SB_Q + SB_W
OFF_SB_V = OFF_SB_K + SB_W
OFF_R_Q = OFF_SB_V + SB_W
OFF_R_K = OFF_R_Q + RET_QK
OFF_R_V = OFF_R_K + RET_QK
OFF_R_G = OFF_R_V + RET_V
OFF_D_Q = OFF_R_G + RET_V
OFF_D_K = OFF_D_Q + DIFF_QK
OFF_D_V = OFF_D_K + DIFF_QK
D_IN = OFF_D_V + DIFF_V

V7X_VMEM_BYTES = 64 * 1024 * 1024
VMEM_LIMIT = V7X_VMEM_BYTES - 8 * 1024 * 1024
LANES = 128
ATT_TK = 512
ATT_STRIP = 256
ATT_UNROLL = 4
SB_PROBE_BLOCKS = 2
SB_ZERO_LOG_WEIGHT = -104.0
SUBLANES = 8


def _cparams(n_axes, vmem=VMEM_LIMIT):
    return pltpu.CompilerParams(dimension_semantics=("arbitrary",) * n_axes,
                                vmem_limit_bytes=vmem)


def _tile(n, t):
    t = min(n, t)
    assert n % t == 0, (n, t)
    return t


def _adaln_kernel(c_ref, w_ref, b_ref, o_ref):
    c = c_ref[...]
    ca = c * jax.nn.sigmoid(c)
    o_ref[0] = jnp.sum(w_ref[0] * ca, axis=0, keepdims=True) + b_ref[0]


def _adaln(c, w_ada, b_ada):
    depth, d, n = w_ada.shape
    tn = _tile(n, 1024)
    return pl.pallas_call(
        _adaln_kernel,
        grid=(depth, n // tn),
        in_specs=[pl.BlockSpec((d, 1), lambda l, j: (0, 0)),
                  pl.BlockSpec((1, d, tn), lambda l, j: (l, 0, j)),
                  pl.BlockSpec((1, 1, tn), lambda l, j: (l, 0, j))],
        out_specs=pl.BlockSpec((1, 1, tn), lambda l, j: (l, 0, j)),
        out_shape=jax.ShapeDtypeStruct((depth, 1, n), F32),
        compiler_params=_cparams(2),
        name="adaln",
    )(c.reshape(d, 1), w_ada, b_ada.reshape(depth, 1, n))


def _norm_mod_kernel(x_ref, g_ref, scale_ref, shift_ref, o_ref):
    x = x_ref[...]
    y = x * lax.rsqrt(jnp.mean(x * x, axis=-1, keepdims=True) + EPS)
    y = y * g_ref[...]
    o_ref[...] = (y * (1.0 + scale_ref[...]) + shift_ref[...]).astype(o_ref.dtype)


def _norm_mod(x, g, scale, shift):
    s, d = x.shape
    tm = _tile(s, 512)
    row = pl.BlockSpec((1, d), lambda i: (0, 0))
    return pl.pallas_call(
        _norm_mod_kernel,
        grid=(s // tm,),
        in_specs=[pl.BlockSpec((tm, d), lambda i: (i, 0)), row, row, row],
        out_specs=pl.BlockSpec((tm, d), lambda i: (i, 0)),
        out_shape=jax.ShapeDtypeStruct((s, d), BF16),
        compiler_params=_cparams(1),
        name="norm_mod",
    )(x, g.reshape(1, d), scale, shift)


def _proj_kernel(a_ref, w_ref, o_ref, wbf_ref, *, act):
    @pl.when(pl.program_id(1) == 0)
    def _():
        wbf_ref[...] = w_ref[...].astype(BF16)

    r = jnp.dot(a_ref[...], wbf_ref[...], preferred_element_type=F32)
    if act == "sigmoid":
        r = jax.nn.sigmoid(r)
    o_ref[...] = r.astype(o_ref.dtype)


def _proj(a, w, l, act=None, name="proj"):
    m, k = a.shape
    n = w.shape[2]
    tm, tn = _tile(m, 2048), _tile(n, 1024)
    return pl.pallas_call(
        functools.partial(_proj_kernel, act=act),
        grid=(n // tn, m // tm),
        in_specs=[pl.BlockSpec((tm, k), lambda j, i: (i, 0)),
                  pl.BlockSpec((None, k, tn), lambda j, i: (l, 0, j))],
        out_specs=pl.BlockSpec((tm, tn), lambda j, i: (i, j)),
        out_shape=jax.ShapeDtypeStruct((m, n), BF16),
        scratch_shapes=[pltpu.VMEM((k, tn), BF16)],
        compiler_params=_cparams(2),
        name=name,
    )(a, w)


def _rope_tables_kernel(pos_ref, inv_ref, cos_r_ref, sin_r_ref, cos_d_ref, sin_d_ref):
    pos = pos_ref[...].astype(F32)
    lane = lax.broadcasted_iota(jnp.int32, (pos.shape[0], LANES), 1)
    ang = pos * inv_ref[...]
    c, s = jnp.cos(ang), jnp.sin(ang)
    half_r, half_d = RET_DK // 2, DIFF_HD // 2

    def ret_table(t):
        return jnp.where(lane < half_r, t, pltpu.roll(t, half_r, 1))

    def diff_table(t):
        d = pltpu.roll(t, LANES - half_r, 1)
        e = jnp.where(lane < half_d, d, pltpu.roll(d, half_d, 1))
        return jnp.where(lane < 2 * half_d, e, pltpu.roll(e, 2 * half_d, 1))

    cos_r_ref[...] = ret_table(c)
    sin_r_ref[...] = jnp.where(lane < half_r, -1.0, 1.0) * ret_table(s)
    cos_d_ref[...] = diff_table(c)
    sin_d_ref[...] = jnp.where((lane & (DIFF_HD - 1)) < half_d, -1.0, 1.0) * diff_table(s)


def _rope_tables(positions):
    s = positions.shape[0]
    tm = _tile(s, 512)
    assert RET_DK == LANES and 2 * DIFF_HD == LANES
    inv_r = ROPE_THETA ** (-jnp.arange(0, RET_DK, 2, dtype=F32) / RET_DK)
    inv_d = ROPE_THETA ** (-jnp.arange(0, DIFF_HD, 2, dtype=F32) / DIFF_HD)
    pad = jnp.zeros((LANES - RET_DK // 2 - DIFF_HD // 2,), F32)
    inv = jnp.concatenate([inv_r, inv_d, pad]).reshape(1, LANES)
    table = pl.BlockSpec((tm, LANES), lambda i: (i, 0))
    return pl.pallas_call(
        _rope_tables_kernel,
        grid=(s // tm,),
        in_specs=[pl.BlockSpec((tm, 1), lambda i: (i, 0)), pl.BlockSpec((1, LANES), lambda i: (0, 0))],
        out_specs=[table] * 4,
        out_shape=[jax.ShapeDtypeStruct((s, LANES), F32)] * 4,
        compiler_params=_cparams(1),
        name="rope_tables",
    )(positions.reshape(s, 1), inv)


def _prep_kernel(cos_r_ref, sin_r_ref, cos_d_ref, sin_d_ref, qn_ref, kn_ref,
                 rq_ref, rk_ref, dq_ref, dk_ref, sbv_ref, dv_ref,
                 orq_ref, ork_ref, odq_ref, odk_ref, osbvt_ref, odvt_ref):
    tm = rq_ref.shape[0]
    lane = lax.broadcasted_iota(jnp.int32, (tm, LANES), 1)

    for v_ref, ovt_ref in ((sbv_ref, osbvt_ref), (dv_ref, odvt_ref)):
        for h in range(ovt_ref.shape[0]):
            vt = v_ref[:, h * LANES:(h + 1) * LANES].astype(F32).T
            for cb in range(tm // ATT_TK):
                ovt_ref[h, cb] = vt[:, cb * ATT_TK:(cb + 1) * ATT_TK].astype(ovt_ref.dtype)

    cos_r, sin_r = cos_r_ref[...], sin_r_ref[...]
    k_scale = RET_DK ** -0.5
    for h in range(RET_HEADS):
        sl = slice(h * RET_DK, (h + 1) * RET_DK)
        xq = rq_ref[:, sl].astype(F32)
        xk = rk_ref[:, sl].astype(F32)
        orq_ref[:, sl] = (xq * cos_r + pltpu.roll(xq, RET_DK // 2, 1) * sin_r).astype(orq_ref.dtype)
        ork_ref[:, sl] = ((xk * cos_r + pltpu.roll(xk, RET_DK // 2, 1) * sin_r) * k_scale).astype(ork_ref.dtype)

    first_half = (lane & (DIFF_HD - 1)) < DIFF_HD // 2
    cos_d, sin_d = cos_d_ref[...], sin_d_ref[...]
    low_map = lane < DIFF_HD
    q_scale = DIFF_HD ** -0.5

    def qk_norm_rope(x, gain):
        x2 = x * x
        ms_lo = jnp.sum(jnp.where(low_map, x2, 0.0), axis=-1, keepdims=True) * (1.0 / DIFF_HD)
        ms_hi = jnp.sum(jnp.where(low_map, 0.0, x2), axis=-1, keepdims=True) * (1.0 / DIFF_HD)
        r = jnp.where(low_map, lax.rsqrt(ms_lo + EPS), lax.rsqrt(ms_hi + EPS))
        y = x * r * gain
        partner = jnp.where(first_half, pltpu.roll(y, LANES - DIFF_HD // 2, 1),
                            pltpu.roll(y, DIFF_HD // 2, 1))
        return y * cos_d + partner * sin_d

    for h in range(DIFF_HEADS):
        sl = slice(h * 2 * DIFF_HD, (h + 1) * 2 * DIFF_HD)
        odq_ref[:, sl] = (qk_norm_rope(dq_ref[:, sl].astype(F32), qn_ref[...]) * q_scale).astype(odq_ref.dtype)
        odk_ref[:, sl] = qk_norm_rope(dk_ref[:, sl].astype(F32), kn_ref[...]).astype(odk_ref.dtype)


def _prep(qkv, tables, qn, kn):
    s = qkv.shape[0]
    tm = _tile(s, 512)
    w = 512
    assert RET_QK == w and DIFF_QK == w
    col = lambda off: pl.BlockSpec((tm, w), lambda i, off=off: (i, off // w))
    row = pl.BlockSpec((1, LANES), lambda i: (0, 0))
    table = pl.BlockSpec((tm, LANES), lambda i: (i, 0))
    out_spec = pl.BlockSpec((tm, w), lambda i: (i, 0))
    out = jax.ShapeDtypeStruct((s, w), BF16)
    assert SB_HD == LANES and 2 * DIFF_HD == LANES and SB_W == w and DIFF_V == w
    nkb = tm // ATT_TK
    vt_spec = pl.BlockSpec((SB_HEADS, nkb, LANES, ATT_TK), lambda i: (0, i, 0, 0))
    vt_out = jax.ShapeDtypeStruct((SB_HEADS, s // ATT_TK, LANES, ATT_TK), BF16)
    return pl.pallas_call(
        _prep_kernel,
        grid=(s // tm,),
        in_specs=[table, table, table, table, row, row,
                  col(OFF_R_Q), col(OFF_R_K), col(OFF_D_Q), col(OFF_D_K),
                  col(OFF_SB_V), col(OFF_D_V)],
        out_specs=[out_spec] * 4 + [vt_spec] * 2,
        out_shape=[out] * 4 + [vt_out] * 2,
        compiler_params=_cparams(1),
        name="rope_prep",
    )(*tables, jnp.tile(qn, 2).reshape(1, LANES), jnp.tile(kn, 2).reshape(1, LANES),
      qkv, qkv, qkv, qkv, qkv, qkv)


def _sb_kernel(q_ref, k_ref, vt_ref, o_ref, acc_ref, csum_ref, z_ref, lb_ref, later_ref, lf0_ref,
               w_ref, *, tq, tk, scale):
    i = pl.program_id(1)
    st = ATT_STRIP
    n_strip, n_half = tq // st, tk // st
    assert tq == tk
    acc_ref[...] = jnp.zeros_like(acc_ref)
    csum_ref[...] = jnp.zeros_like(csum_ref)
    row = lax.broadcasted_iota(jnp.int32, (st, st), 0)
    col = lax.broadcasted_iota(jnp.int32, (st, st), 1)
    tri = jnp.where(col > row, 1.0, 0.0).astype(BF16)

    def scores(c, kb, hf):
        ks = pl.multiple_of(kb * tk, tk)
        k = k_ref[pl.ds(ks + hf * st, st), :]
        return lax.dot_general(k, q_ref[c * st:(c + 1) * st, :], (((1,), (1,)), ((), ())),
                               preferred_element_type=F32)

    def log_probs(raw, masked):
        z = raw * scale
        sp = jnp.log(1.0 + jnp.exp(-jnp.abs(z)))
        log_beta = jnp.minimum(z, 0.0) - sp
        log_fail = log_beta - z
        if masked:
            log_fail = jnp.where(row < col, log_fail, 0.0)
        return log_beta, log_fail.astype(BF16), log_fail[0:1, :]

    def later_sum(lf_bf):
        return jnp.dot(tri, lf_bf, preferred_element_type=F32)

    def weights(c, hf, log_beta, later, lf0, masked):
        lanes = slice(c * st, (c + 1) * st)
        csum = csum_ref[:, lanes]
        w = jnp.exp(log_beta + later + csum)
        if masked:
            w = jnp.where(row < col, w, 0.0)
        w_ref[hf * st:(hf + 1) * st, :] = w.astype(BF16)
        csum_ref[:, lanes] = csum + later[0:1, :] + lf0

    def accumulate(c, kb, n_live):
        lanes = slice(c * st, (c + 1) * st)
        acc_ref[:, lanes] += jnp.dot(vt_ref[kb, :, 0:n_live * st], w_ref[0:n_live * st, :],
                                     preferred_element_type=F32)

    diag_steps = [(c, hf, hf == c, c + 1 if hf == 0 else None)
                  for c in range(n_strip) for hf in reversed(range(c + 1))]
    block_steps = [(c, hf, False, n_half if hf == 0 else None)
                   for c in range(n_strip) for hf in reversed(range(n_half))]
    assert len(block_steps) % 2 == 0

    def stage23(p, raw, masked):
        log_beta, lf_bf, lf0 = log_probs(raw, masked)
        lb_ref[p], lf0_ref[p] = log_beta, lf0
        later_ref[p] = later_sum(lf_bf)

    def run(section, parity):
        for n in range(len(section) - 2):
            p = (parity + n) % 2
            ((c, hf, masked, n_live), kb), ((_, _, masked1, _), _), ((c2, hf2, _, _), kb2) = section[n:n + 3]
            z_ref[p] = scores(c2, kb2, hf2)
            stage23(1 - p, z_ref[1 - p], masked1)
            weights(c, hf, lb_ref[p], later_ref[p], lf0_ref[p], masked)
            if n_live:
                accumulate(c, kb, n_live)

    def at(steps, kb):
        return [(step, kb) for step in steps]

    (c0, hf0, masked0, _), (c1, hf1, _, _) = diag_steps[:2]
    stage23(0, scores(c0, i, hf0), masked0)
    z_ref[1] = scores(c1, i, hf1)
    run(at(diag_steps, i) + at(block_steps[:2], jnp.maximum(i - 1, 0)), 0)

    def sweep(first_kb, n_blocks):
        section = []
        for b in range(n_blocks):
            section += at(block_steps, first_kb - b)
        run(section + at(block_steps[:2], jnp.maximum(first_kb - n_blocks, 0)), len(diag_steps))

    def live():
        return (jnp.max(csum_ref[...]) >= SB_ZERO_LOG_WEIGHT).astype(jnp.int32)

    def sweep_while(alive, first_kb, n_trips, n_blocks):
        def cond(state):
            t, alive = state
            return jnp.logical_and(t < n_trips, alive > 0)

        def body(state):
            t, _ = state
            sweep(first_kb - n_blocks * t, n_blocks)
            return t + 1, live()

        return lax.while_loop(cond, body, (jnp.int32(0), alive))[1]

    n_probe = jnp.minimum(i, SB_PROBE_BLOCKS)
    n_rest = i - n_probe
    alive = sweep_while(jnp.int32(1), i - 1, n_probe, 1)
    alive = sweep_while(alive, n_rest - 1, n_rest // ATT_UNROLL, ATT_UNROLL)
    sweep_while(alive, n_rest % ATT_UNROLL - 1, n_rest % ATT_UNROLL, 1)

    o_ref[...] = acc_ref[...].T.astype(o_ref.dtype)


def _sb_attention(qkv, vt):
    s = qkv.shape[0]
    d = SB_HD
    tq = tk = ATT_TK
    return pl.pallas_call(
        functools.partial(_sb_kernel, tq=tq, tk=tk, scale=d ** -0.5),
        grid=(SB_HEADS, s // tq),
        in_specs=[pl.BlockSpec((tq, d), lambda h, i: (i, OFF_SB_Q // d + h)),
                  pl.BlockSpec((s, d), lambda h, i: (0, OFF_SB_K // d + h)),
                  pl.BlockSpec((None, s // tk, d, tk), lambda h, i: (h, 0, 0, 0))],
        out_specs=pl.BlockSpec((tq, d), lambda h, i: (i, h)),
        out_shape=jax.ShapeDtypeStruct((s, SB_W), BF16),
        scratch_shapes=[pltpu.VMEM((d, tq), F32), pltpu.VMEM((1, tq), F32),
                        pltpu.VMEM((2, ATT_STRIP, ATT_STRIP), F32),
                        pltpu.VMEM((2, ATT_STRIP, ATT_STRIP), F32),
                        pltpu.VMEM((2, ATT_STRIP, ATT_STRIP), F32),
                        pltpu.VMEM((2, 1, ATT_STRIP), F32),
                        pltpu.VMEM((tk, ATT_STRIP), BF16)],
        compiler_params=_cparams(2),
        name="sb_attention",
    )(qkv, qkv, vt)


def _ret_kernel(lg_ref, q_ref, k_ref, v0_ref, v1_ref, g0_ref, g1_ref, gn_g_ref, gn_b_ref, o_ref,
                state_ref, decay_ref, *, tl):
    heads = range(RET_HEADS)
    per_block = RET_HEADS // 2
    r = lax.broadcasted_iota(jnp.int32, (tl, 1), 0).astype(F32)

    def wide(refs, h):
        c0 = (h % per_block) * RET_DV
        return refs[h // per_block][:, c0:c0 + RET_DV]

    @pl.when(pl.program_id(0) == 0)
    def _():
        state_ref[...] = jnp.zeros_like(state_ref)
        n = lax.broadcasted_iota(jnp.int32, (tl, tl), 0)
        m = lax.broadcasted_iota(jnp.int32, (tl, tl), 1)
        dist = jnp.abs(n - m).astype(F32)
        visible = (m >> CHUNK_SHIFT) <= (n >> CHUNK_SHIFT)
        for h in heads:
            decay_ref[h] = jnp.where(visible, jnp.exp(dist * lg_ref[h]), 0.0)

    qs = [q_ref[:, h * RET_DK:(h + 1) * RET_DK] for h in heads]
    ks = [k_ref[:, h * RET_DK:(h + 1) * RET_DK] for h in heads]
    vs = [wide((v0_ref, v1_ref), h) for h in heads]
    scores = [lax.dot_general(qs[h], ks[h], (((1,), (1,)), ((), ())), preferred_element_type=F32)
              for h in heads]
    cross = [jnp.dot(qs[h], state_ref[h].astype(BF16), preferred_element_type=F32) for h in heads]
    pushed = []
    for h in heads:
        zeta = jnp.exp((tl - 1.0 - r) * lg_ref[h])
        kz = (ks[h].astype(F32) * zeta).astype(BF16)
        pushed.append(lax.dot_general(kz, vs[h], (((0,), (0,)), ((), ())),
                                      preferred_element_type=F32))
    intra = [jnp.dot((scores[h] * decay_ref[h]).astype(BF16), vs[h], preferred_element_type=F32)
             for h in heads]
    for h in heads:
        lg = lg_ref[h]
        block_decay = jnp.exp(jnp.full((1, 1), tl, F32) * lg)
        state_ref[h] = state_ref[h] * block_decay + pushed[h]
        y = intra[h] + cross[h] * jnp.exp((r + 1.0) * lg)
        mu = jnp.mean(y, axis=-1, keepdims=True)
        yc = y - mu
        var = jnp.mean(yc * yc, axis=-1, keepdims=True)
        cols = slice(h * RET_DV, (h + 1) * RET_DV)
        yn = yc * lax.rsqrt(var + EPS) * gn_g_ref[:, cols] + gn_b_ref[:, cols]
        gate = wide((g0_ref, g1_ref), h).astype(F32)
        o_ref[:, cols] = (yn * (gate * jax.nn.sigmoid(gate))).astype(o_ref.dtype)


def _retention(qkv, rq, rk, gn_g, gn_b):
    s = qkv.shape[0]
    tl = _tile(s, 256)
    log_gamma = jnp.log(1.0 - 2.0 ** (-5.0 - jnp.arange(RET_HEADS, dtype=F32)))
    half = RET_V // 2
    assert RET_QK == half and OFF_R_V % half == 0 and OFF_R_G % half == 0 and RET_HEADS % 2 == 0
    col = lambda off: pl.BlockSpec((tl, half), lambda b, off=off: (b, off // half))
    row = pl.BlockSpec((1, RET_V), lambda b: (0, 0))
    return pl.pallas_call(
        functools.partial(_ret_kernel, tl=tl),
        grid=(s // tl,),
        in_specs=[pl.BlockSpec(memory_space=pltpu.SMEM), col(0), col(0),
                  col(OFF_R_V), col(OFF_R_V + half), col(OFF_R_G), col(OFF_R_G + half), row, row],
        out_specs=pl.BlockSpec((tl, RET_V), lambda b: (b, 0)),
        out_shape=jax.ShapeDtypeStruct((s, RET_V), BF16),
        scratch_shapes=[pltpu.VMEM((RET_HEADS, RET_DK, RET_DV), F32),
                        pltpu.VMEM((RET_HEADS, tl, tl), F32)],
        compiler_params=_cparams(1),
        name="retention",
    )(log_gamma, rq, rk, qkv, qkv, qkv, qkv, gn_g.reshape(1, RET_V), gn_b.reshape(1, RET_V))


def _diff_kernel(lq1_ref, lk1_ref, lq2_ref, lk2_ref, on_ref, q_ref, k_ref, vt_ref, o_ref,
                 q2_ref, m_ref, l_ref, acc_ref, s0_ref, s1_ref, *, tq, tk, lam_init):
    i = pl.program_id(1)
    st = ATT_STRIP
    assert tq == tk and tq == 2 * st
    q = q_ref[...]
    lane = lax.broadcasted_iota(jnp.int32, q.shape, 1)
    zero = jnp.zeros_like(q)
    q2_ref[0:tq, :] = jnp.where(lane < DIFF_HD, q, zero)
    q2_ref[tq:2 * tq, :] = jnp.where(lane < DIFF_HD, zero, q)
    m_ref[...] = jnp.full_like(m_ref, -jnp.inf)
    l_ref[...] = jnp.zeros_like(l_ref)
    acc_ref[...] = jnp.zeros_like(acc_ref)

    n_strip = 2 * tq // st
    s_bufs = (s0_ref, s1_ref)

    def scores(c, kb, nk):
        ks = pl.multiple_of(kb * tk, tk)
        qc = q2_ref[c * st:(c + 1) * st, :]
        parts = [lax.dot_general(k_ref[pl.ds(ks + r, st), :], qc, (((1,), (1,)), ((), ())),
                                 preferred_element_type=F32) for r in range(0, nk, st)]
        return parts[0] if len(parts) == 1 else jnp.concatenate(parts, axis=0)

    def softmax_pv(c, kb, nk, s, diagonal):
        lanes = slice(c * st, (c + 1) * st)
        if diagonal:
            kchunk = lax.broadcasted_iota(jnp.int32, (nk, st), 0) >> CHUNK_SHIFT
            qchunk = ((c % 2) * st + lax.broadcasted_iota(jnp.int32, (nk, st), 1)) >> CHUNK_SHIFT
            s = jnp.where(kchunk <= qchunk, s, -jnp.inf)
        m_old = m_ref[:, lanes]
        m_new = jnp.maximum(m_old, jnp.max(s, axis=0, keepdims=True))
        alpha = jnp.exp(m_old - m_new)
        p = jnp.exp(s - m_new)
        l_ref[:, lanes] = alpha * l_ref[:, lanes] + jnp.sum(p, axis=0, keepdims=True)
        acc_ref[:, lanes] = alpha * acc_ref[:, lanes] + jnp.dot(
            vt_ref[kb, :, 0:nk], p.astype(BF16), preferred_element_type=F32)
        m_ref[:, lanes] = m_new

    def key_block(kb, diagonal):
        for c in range(n_strip):
            nk = (c % 2 + 1) * st if diagonal else tk
            if c + 1 < n_strip:
                s_bufs[(c + 1) % 2][...] = scores(c + 1, kb, tk)
            elif not diagonal:
                s_bufs[0][...] = scores(0, kb + 1, tk)
            softmax_pv(c, kb, nk, s_bufs[c % 2][0:nk, :], diagonal)

    assert n_strip % 2 == 0
    s_bufs[0][...] = scores(0, 0, tk)

    def body(t, carry):
        for b in range(ATT_UNROLL):
            key_block(ATT_UNROLL * t + b, False)
        return carry

    def tail_body(kb, carry):
        key_block(kb, False)
        return carry

    n_main = i // ATT_UNROLL
    lax.fori_loop(0, n_main, body, 0)
    lax.fori_loop(n_main * ATT_UNROLL, i, tail_body, 0)
    key_block(i, True)

    lam = (jnp.exp(jnp.sum(lq1_ref[...] * lk1_ref[...], axis=-1, keepdims=True))
           - jnp.exp(jnp.sum(lq2_ref[...] * lk2_ref[...], axis=-1, keepdims=True)) + lam_init)
    o = acc_ref[...] / l_ref[...]
    a = o[:, 0:tq] - lam * o[:, tq:2 * tq]
    a = a * lax.rsqrt(jnp.mean(a * a, axis=0, keepdims=True) + EPS)
    o_ref[...] = (a.T * on_ref[...] * (1.0 - lam_init)).astype(o_ref.dtype)


def _diff_attention(dq, dk, vt, lq1, lk1, lq2, lk2, on, lam_init):
    s = dq.shape[0]
    dv = 2 * DIFF_HD
    tq = tk = ATT_TK
    lam_spec = pl.BlockSpec((1, DIFF_HD), lambda h, i: (0, 0))
    return pl.pallas_call(
        functools.partial(_diff_kernel, tq=tq, tk=tk, lam_init=lam_init),
        grid=(DIFF_HEADS, s // tq),
        in_specs=[lam_spec, lam_spec, lam_spec, lam_spec,
                  pl.BlockSpec((1, dv), lambda h, i: (0, 0)),
                  pl.BlockSpec((tq, dv), lambda h, i: (i, h)),
                  pl.BlockSpec((s, dv), lambda h, i: (0, h)),
                  pl.BlockSpec((None, s // tk, dv, tk), lambda h, i: (h, 0, 0, 0))],
        out_specs=pl.BlockSpec((tq, dv), lambda h, i: (i, h)),
        out_shape=jax.ShapeDtypeStruct((s, DIFF_V), BF16),
        scratch_shapes=[pltpu.VMEM((2 * tq, dv), BF16), pltpu.VMEM((1, 2 * tq), F32),
                        pltpu.VMEM((1, 2 * tq), F32), pltpu.VMEM((dv, 2 * tq), F32),
                        pltpu.VMEM((tk, ATT_STRIP), F32), pltpu.VMEM((tk, ATT_STRIP), F32)],
        compiler_params=_cparams(2),
        name="diff_attention",
    )(lq1.reshape(1, DIFF_HD), lk1.reshape(1, DIFF_HD), lq2.reshape(1, DIFF_HD),
      lk2.reshape(1, DIFF_HD), on.reshape(1, dv), dq, dk, vt)


def _merge_kernel(ysb_ref, yret_ref, ydiff_ref, gsb_ref, gret_ref, gdiff_ref, w_ref, o_ref, wbf_ref):
    @pl.when(pl.program_id(1) == 0)
    def _():
        wbf_ref[...] = w_ref[...].astype(BF16)

    r_sb = jnp.dot(ysb_ref[...], wbf_ref[0:SB_W, :], preferred_element_type=F32)
    r_ret = jnp.dot(yret_ref[...], wbf_ref[SB_W:SB_W + RET_V, :], preferred_element_type=F32)
    r_diff = jnp.dot(ydiff_ref[...], wbf_ref[SB_W + RET_V:SB_W + RET_V + DIFF_V, :],
                     preferred_element_type=F32)
    merged = (gsb_ref[...].astype(F32) * r_sb + gret_ref[...].astype(F32) * r_ret
              + gdiff_ref[...].astype(F32) * r_diff)
    o_ref[...] = merged.astype(o_ref.dtype)


def _merge(y_sb, y_ret, y_diff, gates, w_branch, l):
    s = y_sb.shape[0]
    _, kw, n = w_branch.shape
    tm, tn = _tile(s, 1024), _tile(n, 1024)
    nj = n // tn
    gate = lambda b: pl.BlockSpec((tm, tn), lambda j, i, b=b: (i, b * nj + j))
    return pl.pallas_call(
        _merge_kernel,
        grid=(nj, s // tm),
        in_specs=[pl.BlockSpec((tm, SB_W), lambda j, i: (i, 0)),
                  pl.BlockSpec((tm, RET_V), lambda j, i: (i, 0)),
                  pl.BlockSpec((tm, DIFF_V), lambda j, i: (i, 0)),
                  gate(0), gate(1), gate(2),
                  pl.BlockSpec((None, kw, tn), lambda j, i: (l, 0, j))],
        out_specs=pl.BlockSpec((tm, tn), lambda j, i: (i, j)),
        out_shape=jax.ShapeDtypeStruct((s, n), BF16),
        scratch_shapes=[pltpu.VMEM((kw, tn), BF16)],
        compiler_params=_cparams(2),
        name="merge",
    )(y_sb, y_ret, y_diff, gates, gates, gates, w_branch)


def _cast_kernel(w_ref, o_ref):
    o_ref[...] = w_ref[...].astype(o_ref.dtype)


def _to_bf16(w, l):
    _, k, n = w.shape
    tk = _tile(k, 512)
    return pl.pallas_call(
        _cast_kernel,
        grid=(k // tk,),
        in_specs=[pl.BlockSpec((None, tk, n), lambda i: (l, i, 0))],
        out_specs=pl.BlockSpec((tk, n), lambda i: (i, 0)),
        out_shape=jax.ShapeDtypeStruct((k, n), BF16),
        compiler_params=_cparams(1),
        name="to_bf16",
    )(w)


def _residual_bf16_kernel(a_ref, w_ref, x_ref, gate_ref, o_ref):
    r = jnp.dot(a_ref[...], w_ref[...], preferred_element_type=F32)
    o_ref[...] = x_ref[...] + gate_ref[...] * r


def _residual_proj_deep(a, w, l, x, gate, name):
    m, k = a.shape
    n = w.shape[2]
    tm, tn = _tile(m, 512), _tile(n, 1024)
    return pl.pallas_call(
        _residual_bf16_kernel,
        grid=(n // tn, m // tm),
        in_specs=[pl.BlockSpec((tm, k), lambda j, i: (i, 0)),
                  pl.BlockSpec((k, tn), lambda j, i: (0, j)),
                  pl.BlockSpec((tm, tn), lambda j, i: (i, j)),
                  pl.BlockSpec((1, tn), lambda j, i: (0, j))],
        out_specs=pl.BlockSpec((tm, tn), lambda j, i: (i, j)),
        out_shape=jax.ShapeDtypeStruct((m, n), F32),
        compiler_params=_cparams(2),
        name=name,
    )(a, _to_bf16(w, l), x, gate)


def _residual_norm_kernel(a_ref, w_ref, x_ref, gate_ref, g_ref, scale_ref, shift_ref,
                          xo_ref, h_ref):
    r = jnp.dot(a_ref[...], w_ref[...], preferred_element_type=F32)
    x = x_ref[...] + gate_ref[...] * r
    xo_ref[...] = x
    y = x * lax.rsqrt(jnp.mean(x * x, axis=-1, keepdims=True) + EPS)
    y = y * g_ref[...]
    h_ref[...] = (y * (1.0 + scale_ref[...]) + shift_ref[...]).astype(h_ref.dtype)


def _residual_proj_norm(a, w, l, x, gate, g, scale, shift, name):
    m, k = a.shape
    n = w.shape[2]
    tm = _tile(m, 512)
    row = pl.BlockSpec((1, n), lambda i: (0, 0))
    full = pl.BlockSpec((tm, n), lambda i: (i, 0))
    return pl.pallas_call(
        _residual_norm_kernel,
        grid=(m // tm,),
        in_specs=[pl.BlockSpec((tm, k), lambda i: (i, 0)),
                  pl.BlockSpec((k, n), lambda i: (0, 0)),
                  full, row, row, row, row],
        out_specs=[full, full],
        out_shape=[jax.ShapeDtypeStruct((m, n), F32), jax.ShapeDtypeStruct((m, n), BF16)],
        compiler_params=_cparams(1),
        name=name,
    )(a, _to_bf16(w, l), x, gate, g.reshape(1, n), scale, shift)


def _ffn_up_kernel(a_ref, wg_ref, wv_ref, cwg_ref, cwv_ref, cbg_ref, cbv_ref, o_ref,
                   wg_bf_ref, wv_bf_ref, ug_ref, uv_ref, *, tm):
    i = pl.program_id(1)
    halo = SUBLANES

    @pl.when(i == 0)
    def _():
        wg_bf_ref[...] = wg_ref[...].astype(BF16)
        wv_bf_ref[...] = wv_ref[...].astype(BF16)
        ug_ref[0:halo, :] = jnp.zeros((halo, ug_ref.shape[1]), F32)
        uv_ref[0:halo, :] = jnp.zeros((halo, uv_ref.shape[1]), F32)

    a = a_ref[...]

    def conv(u_ref, wbf_ref, cw_ref, cb_ref):
        u_ref[halo:halo + tm, :] = jnp.dot(a, wbf_ref[...], preferred_element_type=F32)
        y = cb_ref[...]
        for t in range(CONV_W):
            shift = CONV_W - 1 - t
            y = y + cw_ref[t:t + 1, :] * u_ref[halo - shift:halo - shift + tm, :]
        u_ref[0:halo, :] = u_ref[tm:tm + halo, :]
        return y

    yg = conv(ug_ref, wg_bf_ref, cwg_ref, cbg_ref)
    yv = conv(uv_ref, wv_bf_ref, cwv_ref, cbv_ref)
    o_ref[...] = (yg * jax.nn.sigmoid(yg) * yv).astype(o_ref.dtype)


def _ffn_up(h2, w_up, conv_w, conv_b, l):
    m, k = h2.shape
    f = w_up.shape[2] // 2
    tm, tn = _tile(m, 1024), _tile(f, 512)
    nj = f // tn
    wspec = lambda half: pl.BlockSpec((None, k, tn), lambda j, i, half=half: (l, 0, half * nj + j))
    cwspec = lambda half: pl.BlockSpec((None, CONV_W, tn), lambda j, i, half=half: (l, 0, half * nj + j))
    cbspec = lambda half: pl.BlockSpec((None, 1, tn), lambda j, i, half=half: (l, 0, half * nj + j))
    return pl.pallas_call(
        functools.partial(_ffn_up_kernel, tm=tm),
        grid=(nj, m // tm),
        in_specs=[pl.BlockSpec((tm, k), lambda j, i: (i, 0)),
                  wspec(0), wspec(1), cwspec(0), cwspec(1), cbspec(0), cbspec(1)],
        out_specs=pl.BlockSpec((tm, tn), lambda j, i: (i, j)),
        out_shape=jax.ShapeDtypeStruct((m, f), BF16),
        scratch_shapes=[pltpu.VMEM((k, tn), BF16), pltpu.VMEM((k, tn), BF16),
                        pltpu.VMEM((tm + SUBLANES, tn), F32), pltpu.VMEM((tm + SUBLANES, tn), F32)],
        compiler_params=_cparams(2),
        name="ffn_up",
    )(h2, w_up, w_up, conv_w, conv_w, conv_b[:, None, :], conv_b[:, None, :])


def kernel(x, c, positions, w_ada, b_ada, norm_mix, w_in, w_gate, ret_gn_g, ret_gn_b, diff_qn, diff_kn, diff_on, lam_q1, lam_k1, lam_q2, lam_k2, w_branch, w_out, norm_ffn, w_up, conv_w, conv_b, w_down):
    batch, s, d = x.shape
    depth = w_ada.shape[0]
    x2d = x.reshape(batch * s, d)
    outs = []
    for b in range(batch):
        xb = x2d[b * s:(b + 1) * s]
        pos = positions[b]
        mod = _adaln(c[b:b + 1], w_ada, b_ada)
        tables = _rope_tables(pos)
        for l in range(depth):
            shift1, scale1, gate1, shift2, scale2, gate2 = [
                mod[l, :, t * d:(t + 1) * d] for t in range(6)]
            h = _norm_mod(xb, norm_mix[l], scale1, shift1)
            qkv = _proj(h, w_in, l, name="in_proj")
            gates = _proj(h, w_gate, l, act="sigmoid", name="gate_proj")
            rq, rk, dq, dk, sb_vt, d_vt = _prep(qkv, tables, diff_qn[l], diff_kn[l])
            y_sb = _sb_attention(qkv, sb_vt)
            y_ret = _retention(qkv, rq, rk, ret_gn_g[l], ret_gn_b[l])
            lam_init = 0.8 - 0.6 * float(np.exp(-0.3 * l))
            y_diff = _diff_attention(dq, dk, d_vt, lam_q1[l], lam_k1[l], lam_q2[l], lam_k2[l],
                                     diff_on[l], lam_init)
            merged = _merge(y_sb, y_ret, y_diff, gates, w_branch, l)
            xb, h2 = _residual_proj_norm(merged, w_out, l, xb, gate1, norm_ffn[l], scale2, shift2,
                                         name="out_proj")
            act = _ffn_up(h2, w_up, conv_w, conv_b, l)
            xb = _residual_proj_deep(act, w_down, l, xb, gate2, name="ffn_down")
        outs.append(xb)
    return jnp.concatenate(outs, axis=0).reshape(batch, s, d)
```

```python
import functools

import jax
import jax.numpy as jnp
import numpy as np
from jax import lax
from jax.experimental import pallas as pl
from jax.experimental.pallas import tpu as pltpu

F32 = jnp.float32
BF16 = jnp.bfloat16

CHUNK = 64
CHUNK_SHIFT = CHUNK.bit_length() - 1
assert 1 << CHUNK_SHIFT == CHUNK
ROPE_THETA = 10000.0
EPS = 1e-6
SB_HEADS, SB_HD = 4, 128
RET_HEADS, RET_DK, RET_DV = 4, 128, 256
DIFF_HEADS, DIFF_HD = 4, 64
CONV_W = 3

SB_W = SB_HEADS * SB_HD
RET_QK = RET_HEADS * RET_DK
RET_V = RET_HEADS * RET_DV
DIFF_QK = DIFF_HEADS * 2 * DIFF_HD
DIFF_V = DIFF_HEADS * 2 * DIFF_HD
OFF_SB_Q = 0
OFF_SB_K = OFF_SB_Q + SB_W
OFF_SB_V = OFF_SB_K + SB_W
OFF_R_Q = OFF_SB_V + SB_W
OFF_R_K = OFF_R_Q + RET_QK
OFF_R_V = OFF_R_K + RET_QK
OFF_R_G = OFF_R_V + RET_V
OFF_D_Q = OFF_R_G + RET_V
OFF_D_K = OFF_D_Q + DIFF_QK
OFF_D_V = OFF_D_K + DIFF_QK
D_IN = OFF_D_V + DIFF_V

V7X_VMEM_BYTES = 64 * 1024 * 1024
VMEM_LIMIT = V7X_VMEM_BYTES - 8 * 1024 * 1024
LANES = 128
ATT_TK = 512
ATT_STRIP = 256
ATT_UNROLL = 4
SB_PROBE_BLOCKS = 2
SB_ZERO_LOG_WEIGHT = -104.0
SUBLANES = 8


def _cparams(n_axes, vmem=VMEM_LIMIT):
    return pltpu.CompilerParams(dimension_semantics=("arbitrary",) * n_axes,
                                vmem_limit_bytes=vmem)


def _tile(n, t):
    t = min(n, t)
    assert n % t == 0, (n, t)
    return t


def _adaln_kernel(c_ref, w_ref, b_ref, o_ref):
    c = c_ref[...]
    ca = c * jax.nn.sigmoid(c)
    o_ref[0] = jnp.sum(w_ref[0] * ca, axis=0, keepdims=True) + b_ref[0]


def _adaln(c, w_ada, b_ada):
    depth, d, n = w_ada.shape
    tn = _tile(n, 1024)
    return pl.pallas_call(
        _adaln_kernel,
        grid=(depth, n // tn),
        in_specs=[pl.BlockSpec((d, 1), lambda l, j: (0, 0)),
                  pl.BlockSpec((1, d, tn), lambda l, j: (l, 0, j)),
                  pl.BlockSpec((1, 1, tn), lambda l, j: (l, 0, j))],
        out_specs=pl.BlockSpec((1, 1, tn), lambda l, j: (l, 0, j)),
        out_shape=jax.ShapeDtypeStruct((depth, 1, n), F32),
        compiler_params=_cparams(2),
        name="adaln",
    )(c.reshape(d, 1), w_ada, b_ada.reshape(depth, 1, n))


def _norm_mod_kernel(x_ref, g_ref, scale_ref, shift_ref, o_ref):
    x = x_ref[...]
    y = x * lax.rsqrt(jnp.mean(x * x, axis=-1, keepdims=True) + EPS)
    y = y * g_ref[...]
    o_ref[...] = (y * (1.0 + scale_ref[...]) + shift_ref[...]).astype(o_ref.dtype)


def _norm_mod(x, g, scale, shift):
    s, d = x.shape
    tm = _tile(s, 512)
    row = pl.BlockSpec((1, d), lambda i: (0, 0))
    return pl.pallas_call(
        _norm_mod_kernel,
        grid=(s // tm,),
        in_specs=[pl.BlockSpec((tm, d), lambda i: (i, 0)), row, row, row],
        out_specs=pl.BlockSpec((tm, d), lambda i: (i, 0)),
        out_shape=jax.ShapeDtypeStruct((s, d), BF16),
        compiler_params=_cparams(1),
        name="norm_mod",
    )(x, g.reshape(1, d), scale, shift)


def _proj_kernel(a_ref, w_ref, o_ref, wbf_ref, *, act):
    @pl.when(pl.program_id(1) == 0)
    def _():
        wbf_ref[...] = w_ref[...].astype(BF16)

    r = jnp.dot(a_ref[...], wbf_ref[...], preferred_element_type=F32)
    if act == "sigmoid":
        r = jax.nn.sigmoid(r)
    o_ref[...] = r.astype(o_ref.dtype)


def _proj(a, w, l, act=None, name="proj"):
    m, k = a.shape
    n = w.shape[2]
    tm, tn = _tile(m, 2048), _tile(n, 1024)
    return pl.pallas_call(
        functools.partial(_proj_kernel, act=act),
        grid=(n // tn, m // tm),
        in_specs=[pl.BlockSpec((tm, k), lambda j, i: (i, 0)),
                  pl.BlockSpec((None, k, tn), lambda j, i: (l, 0, j))],
        out_specs=pl.BlockSpec((tm, tn), lambda j, i: (i, j)),
        out_shape=jax.ShapeDtypeStruct((m, n), BF16),
        scratch_shapes=[pltpu.VMEM((k, tn), BF16)],
        compiler_params=_cparams(2),
        name=name,
    )(a, w)


def _rope_tables_kernel(pos_ref, inv_ref, cos_r_ref, sin_r_ref, cos_d_ref, sin_d_ref):
    pos = pos_ref[...].astype(F32)
    lane = lax.broadcasted_iota(jnp.int32, (pos.shape[0], LANES), 1)
    ang = pos * inv_ref[...]
    c, s = jnp.cos(ang), jnp.sin(ang)
    half_r, half_d = RET_DK // 2, DIFF_HD // 2

    def ret_table(t):
        return jnp.where(lane < half_r, t, pltpu.roll(t, half_r, 1))

    def diff_table(t):
        d = pltpu.roll(t, LANES - half_r, 1)
        e = jnp.where(lane < half_d, d, pltpu.roll(d, half_d, 1))
        return jnp.where(lane < 2 * half_d, e, pltpu.roll(e, 2 * half_d, 1))

    cos_r_ref[...] = ret_table(c)
    sin_r_ref[...] = jnp.where(lane < half_r, -1.0, 1.0) * ret_table(s)
    cos_d_ref[...] = diff_table(c)
    sin_d_ref[...] = jnp.where((lane & (DIFF_HD - 1)) < half_d, -1.0, 1.0) * diff_table(s)


def _rope_tables(positions):
    s = positions.shape[0]
    tm = _tile(s, 512)
    assert RET_DK == LANES and 2 * DIFF_HD == LANES
    inv_r = ROPE_THETA ** (-jnp.arange(0, RET_DK, 2, dtype=F32) / RET_DK)
    inv_d = ROPE_THETA ** (-jnp.arange(0, DIFF_HD, 2, dtype=F32) / DIFF_HD)
    pad = jnp.zeros((LANES - RET_DK // 2 - DIFF_HD // 2,), F32)
    inv = jnp.concatenate([inv_r, inv_d, pad]).reshape(1, LANES)
    table = pl.BlockSpec((tm, LANES), lambda i: (i, 0))
    return pl.pallas_call(
        _rope_tables_kernel,
        grid=(s // tm,),
        in_specs=[pl.BlockSpec((tm, 1), lambda i: (i, 0)), pl.BlockSpec((1, LANES), lambda i: (0, 0))],
        out_specs=[table] * 4,
        out_shape=[jax.ShapeDtypeStruct((s, LANES), F32)] * 4,
        compiler_params=_cparams(1),
        name="rope_tables",
    )(positions.reshape(s, 1), inv)


def _prep_kernel(cos_r_ref, sin_r_ref, cos_d_ref, sin_d_ref, qn_ref, kn_ref,
                 rq_ref, rk_ref, dq_ref, dk_ref, sbv_ref, dv_ref,
                 orq_ref, ork_ref, odq_ref, odk_ref, osbvt_ref, odvt_ref):
    tm = rq_ref.shape[0]
    lane = lax.broadcasted_iota(jnp.int32, (tm, LANES), 1)

    for v_ref, ovt_ref in ((sbv_ref, osbvt_ref), (dv_ref, odvt_ref)):
        for h in range(ovt_ref.shape[0]):
            vt = v_ref[:, h * LANES:(h + 1) * LANES].astype(F32).T
            for cb in range(tm // ATT_TK):
                ovt_ref[h, cb] = vt[:, cb * ATT_TK:(cb + 1) * ATT_TK].astype(ovt_ref.dtype)

    cos_r, sin_r = cos_r_ref[...], sin_r_ref[...]
    k_scale = RET_DK ** -0.5
    for h in range(RET_HEADS):
        sl = slice(h * RET_DK, (h + 1) * RET_DK)
        xq = rq_ref[:, sl].astype(F32)
        xk = rk_ref[:, sl].astype(F32)
        orq_ref[:, sl] = (xq * cos_r + pltpu.roll(xq, RET_DK // 2, 1) * sin_r).astype(orq_ref.dtype)
        ork_ref[:, sl] = ((xk * cos_r + pltpu.roll(xk, RET_DK // 2, 1) * sin_r) * k_scale).astype(ork_ref.dtype)

    first_half = (lane & (DIFF_HD - 1)) < DIFF_HD // 2
    cos_d, sin_d = cos_d_ref[...], sin_d_ref[...]
    low_map = lane < DIFF_HD
    q_scale = DIFF_HD ** -0.5

    def qk_norm_rope(x, gain):
        x2 = x * x
        ms_lo = jnp.sum(jnp.where(low_map, x2, 0.0), axis=-1, keepdims=True) * (1.0 / DIFF_HD)
        ms_hi = jnp.sum(jnp.where(low_map, 0.0, x2), axis=-1, keepdims=True) * (1.0 / DIFF_HD)
        r = jnp.where(low_map, lax.rsqrt(ms_lo + EPS), lax.rsqrt(ms_hi + EPS))
        y = x * r * gain
        partner = jnp.where(first_half, pltpu.roll(y, LANES - DIFF_HD // 2, 1),
                            pltpu.roll(y, DIFF_HD // 2, 1))
        return y * cos_d + partner * sin_d

    for h in range(DIFF_HEADS):
        sl = slice(h * 2 * DIFF_HD, (h + 1) * 2 * DIFF_HD)
        odq_ref[:, sl] = (qk_norm_rope(dq_ref[:, sl].astype(F32), qn_ref[...]) * q_scale).astype(odq_ref.dtype)
        odk_ref[:, sl] = qk_norm_rope(dk_ref[:, sl].astype(F32), kn_ref[...]).astype(odk_ref.dtype)


def _prep(qkv, tables, qn, kn):
    s = qkv.shape[0]
    tm = _tile(s, 512)
    w = 512
    assert RET_QK == w and DIFF_QK == w
    col = lambda off: pl.BlockSpec((tm, w), lambda i, off=off: (i, off // w))
    row = pl.BlockSpec((1, LANES), lambda i: (0, 0))
    table = pl.BlockSpec((tm, LANES), lambda i: (i, 0))
    out_spec = pl.BlockSpec((tm, w), lambda i: (i, 0))
    out = jax.ShapeDtypeStruct((s, w), BF16)
    assert SB_HD == LANES and 2 * DIFF_HD == LANES and SB_W == w and DIFF_V == w
    nkb = tm // ATT_TK
    vt_spec = pl.BlockSpec((SB_HEADS, nkb, LANES, ATT_TK), lambda i: (0, i, 0, 0))
    vt_out = jax.ShapeDtypeStruct((SB_HEADS, s // ATT_TK, LANES, ATT_TK), BF16)
    return pl.pallas_call(
        _prep_kernel,
        grid=(s // tm,),
        in_specs=[table, table, table, table, row, row,
                  col(OFF_R_Q), col(OFF_R_K), col(OFF_D_Q), col(OFF_D_K),
                  col(OFF_SB_V), col(OFF_D_V)],
        out_specs=[out_spec] * 4 + [vt_spec] * 2,
        out_shape=[out] * 4 + [vt_out] * 2,
        compiler_params=_cparams(1),
        name="rope_prep",
    )(*tables, jnp.tile(qn, 2).reshape(1, LANES), jnp.tile(kn, 2).reshape(1, LANES),
      qkv, qkv, qkv, qkv, qkv, qkv)


def _sb_kernel(q_ref, k_ref, vt_ref, o_ref, acc_ref, csum_ref, z_ref, lb_ref, later_ref, lf0_ref,
               w_ref, *, tq, tk, scale):
    i = pl.program_id(1)
    st = ATT_STRIP
    n_strip, n_half = tq // st, tk // st
    assert tq == tk
    acc_ref[...] = jnp.zeros_like(acc_ref)
    csum_ref[...] = jnp.zeros_like(csum_ref)
    row = lax.broadcasted_iota(jnp.int32, (st, st), 0)
    col = lax.broadcasted_iota(jnp.int32, (st, st), 1)
    tri = jnp.where(col > row, 1.0, 0.0).astype(BF16)

    def scores(c, kb, hf):
        ks = pl.multiple_of(kb * tk, tk)
        k = k_ref[pl.ds(ks + hf * st, st), :]
        return lax.dot_general(k, q_ref[c * st:(c + 1) * st, :], (((1,), (1,)), ((), ())),
                               preferred_element_type=F32)

    def log_probs(raw, masked):
        z = raw * scale
        sp = jnp.log(1.0 + jnp.exp(-jnp.abs(z)))
        log_beta = jnp.minimum(z, 0.0) - sp
        log_fail = log_beta - z
        if masked:
            log_fail = jnp.where(row < col, log_fail, 0.0)
        return log_beta, log_fail.astype(BF16), log_fail[0:1, :]

    def later_sum(lf_bf):
        return jnp.dot(tri, lf_bf, preferred_element_type=F32)

    def weights(c, hf, log_beta, later, lf0, masked):
        lanes = slice(c * st, (c + 1) * st)
        csum = csum_ref[:, lanes]
        w = jnp.exp(log_beta + later + csum)
        if masked:
            w = jnp.where(row < col, w, 0.0)
        w_ref[hf * st:(hf + 1) * st, :] = w.astype(BF16)
        csum_ref[:, lanes] = csum + later[0:1, :] + lf0

    def accumulate(c, kb, n_live):
        lanes = slice(c * st, (c + 1) * st)
        acc_ref[:, lanes] += jnp.dot(vt_ref[kb, :, 0:n_live * st], w_ref[0:n_live * st, :],
                                     preferred_element_type=F32)

    diag_steps = [(c, hf, hf == c, c + 1 if hf == 0 else None)
                  for c in range(n_strip) for hf in reversed(range(c + 1))]

    def strip_steps(c):
        return [(c, hf, False, n_half if hf == 0 else None) for hf in reversed(range(n_half))]

    assert n_half == 2

    def stage23(p, raw, masked):
        log_beta, lf_bf, lf0 = log_probs(raw, masked)
        lb_ref[p], lf0_ref[p] = log_beta, lf0
        later_ref[p] = later_sum(lf_bf)

    def start(section):
        ((c0, hf0, masked0, _), kb0), ((c1, hf1, _, _), kb1) = section[:2]
        stage23(0, scores(c0, kb0, hf0), masked0)
        z_ref[1] = scores(c1, kb1, hf1)

    def run(section, n_real):
        for n in range(n_real):
            p = n % 2
            (c, hf, masked, n_live), kb = section[n]
            if n + 2 < len(section):
                (c2, hf2, _, _), kb2 = section[n + 2]
                z_ref[p] = scores(c2, kb2, hf2)
            if n + 1 < len(section):
                stage23(1 - p, z_ref[1 - p], section[n + 1][0][2])
            weights(c, hf, lb_ref[p], later_ref[p], lf0_ref[p], masked)
            if n_live:
                accumulate(c, kb, n_live)

    def at(steps, kb):
        return [(step, kb) for step in steps]

    diag = at(diag_steps, i)
    start(diag)
    run(diag, len(diag))

    for c in range(n_strip):
        steps = strip_steps(c)
        lanes = slice(c * st, (c + 1) * st)

        def live(lanes=lanes):
            return (jnp.max(csum_ref[:, lanes]) >= SB_ZERO_LOG_WEIGHT).astype(jnp.int32)

        def sweep(first_kb, n_blocks, steps=steps):
            section = []
            for b in range(n_blocks):
                section += at(steps, first_kb - b)
            run(section + at(steps, jnp.maximum(first_kb - n_blocks, 0)), len(section))

        def sweep_while(alive, first_kb, n_trips, n_blocks, sweep=sweep, live=live):
            def cond(state):
                t, alive = state
                return jnp.logical_and(t < n_trips, alive > 0)

            def body(state):
                t, _ = state
                sweep(first_kb - n_blocks * t, n_blocks)
                return t + 1, live()

            return lax.while_loop(cond, body, (jnp.int32(0), alive))[1]

        @pl.when(jnp.logical_and(i > 0, live() > 0))
        def _(steps=steps, sweep_while=sweep_while):
            start(at(steps, i - 1))
            n_probe = jnp.minimum(i, SB_PROBE_BLOCKS)
            n_rest = i - n_probe
            alive = sweep_while(jnp.int32(1), i - 1, n_probe, 1)
            alive = sweep_while(alive, n_rest - 1, n_rest // ATT_UNROLL, ATT_UNROLL)
            sweep_while(alive, n_rest % ATT_UNROLL - 1, n_rest % ATT_UNROLL, 1)

    o_ref[...] = acc_ref[...].T.astype(o_ref.dtype)


def _sb_attention(qkv, vt):
    s = qkv.shape[0]
    d = SB_HD
    tq = tk = ATT_TK
    return pl.pallas_call(
        functools.partial(_sb_kernel, tq=tq, tk=tk, scale=d ** -0.5),
        grid=(SB_HEADS, s // tq),
        in_specs=[pl.BlockSpec((tq, d), lambda h, i: (i, OFF_SB_Q // d + h)),
                  pl.BlockSpec((s, d), lambda h, i: (0, OFF_SB_K // d + h)),
                  pl.BlockSpec((None, s // tk, d, tk), lambda h, i: (h, 0, 0, 0))],
        out_specs=pl.BlockSpec((tq, d), lambda h, i: (i, h)),
        out_shape=jax.ShapeDtypeStruct((s, SB_W), BF16),
        scratch_shapes=[pltpu.VMEM((d, tq), F32), pltpu.VMEM((1, tq), F32),
                        pltpu.VMEM((2, ATT_STRIP, ATT_STRIP), F32),
                        pltpu.VMEM((2, ATT_STRIP, ATT_STRIP), F32),
                        pltpu.VMEM((2, ATT_STRIP, ATT_STRIP), F32),
                        pltpu.VMEM((2, 1, ATT_STRIP), F32),
                        pltpu.VMEM((tk, ATT_STRIP), BF16)],
        compiler_params=_cparams(2),
        name="sb_attention",
    )(qkv, qkv, vt)


def _ret_kernel(lg_ref, q_ref, k_ref, v0_ref, v1_ref, g0_ref, g1_ref, gn_g_ref, gn_b_ref, o_ref,
                state_ref, decay_ref, *, tl):
    heads = range(RET_HEADS)
    per_block = RET_HEADS // 2
    r = lax.broadcasted_iota(jnp.int32, (tl, 1), 0).astype(F32)

    def wide(refs, h):
        c0 = (h % per_block) * RET_DV
        return refs[h // per_block][:, c0:c0 + RET_DV]

    @pl.when(pl.program_id(0) == 0)
    def _():
        state_ref[...] = jnp.zeros_like(state_ref)
        n = lax.broadcasted_iota(jnp.int32, (tl, tl), 0)
        m = lax.broadcasted_iota(jnp.int32, (tl, tl), 1)
        dist = jnp.abs(n - m).astype(F32)
        visible = (m >> CHUNK_SHIFT) <= (n >> CHUNK_SHIFT)
        for h in heads:
            decay_ref[h] = jnp.where(visible, jnp.exp(dist * lg_ref[h]), 0.0)

    qs = [q_ref[:, h * RET_DK:(h + 1) * RET_DK] for h in heads]
    ks = [k_ref[:, h * RET_DK:(h + 1) * RET_DK] for h in heads]
    vs = [wide((v0_ref, v1_ref), h) for h in heads]
    scores = [lax.dot_general(qs[h], ks[h], (((1,), (1,)), ((), ())), preferred_element_type=F32)
              for h in heads]
    cross = [jnp.dot(qs[h], state_ref[h].astype(BF16), preferred_element_type=F32) for h in heads]
    pushed = []
    for h in heads:
        zeta = jnp.exp((tl - 1.0 - r) * lg_ref[h])
        kz = (ks[h].astype(F32) * zeta).astype(BF16)
        pushed.append(lax.dot_general(kz, vs[h], (((0,), (0,)), ((), ())),
                                      preferred_element_type=F32))
    intra = [jnp.dot((scores[h] * decay_ref[h]).astype(BF16), vs[h], preferred_element_type=F32)
             for h in heads]
    for h in heads:
        lg = lg_ref[h]
        block_decay = jnp.exp(jnp.full((1, 1), tl, F32) * lg)
        state_ref[h] = state_ref[h] * block_decay + pushed[h]
        y = intra[h] + cross[h] * jnp.exp((r + 1.0) * lg)
        mu = jnp.mean(y, axis=-1, keepdims=True)
        yc = y - mu
        var = jnp.mean(yc * yc, axis=-1, keepdims=True)
        cols = slice(h * RET_DV, (h + 1) * RET_DV)
        yn = yc * lax.rsqrt(var + EPS) * gn_g_ref[:, cols] + gn_b_ref[:, cols]
        gate = wide((g0_ref, g1_ref), h).astype(F32)
        o_ref[:, cols] = (yn * (gate * jax.nn.sigmoid(gate))).astype(o_ref.dtype)


def _retention(qkv, rq, rk, gn_g, gn_b):
    s = qkv.shape[0]
    tl = _tile(s, 256)
    log_gamma = jnp.log(1.0 - 2.0 ** (-5.0 - jnp.arange(RET_HEADS, dtype=F32)))
    half = RET_V // 2
    assert RET_QK == half and OFF_R_V % half == 0 and OFF_R_G % half == 0 and RET_HEADS % 2 == 0
    col = lambda off: pl.BlockSpec((tl, half), lambda b, off=off: (b, off // half))
    row = pl.BlockSpec((1, RET_V), lambda b: (0, 0))
    return pl.pallas_call(
        functools.partial(_ret_kernel, tl=tl),
        grid=(s // tl,),
        in_specs=[pl.BlockSpec(memory_space=pltpu.SMEM), col(0), col(0),
                  col(OFF_R_V), col(OFF_R_V + half), col(OFF_R_G), col(OFF_R_G + half), row, row],
        out_specs=pl.BlockSpec((tl, RET_V), lambda b: (b, 0)),
        out_shape=jax.ShapeDtypeStruct((s, RET_V), BF16),
        scratch_shapes=[pltpu.VMEM((RET_HEADS, RET_DK, RET_DV), F32),
                        pltpu.VMEM((RET_HEADS, tl, tl), F32)],
        compiler_params=_cparams(1),
        name="retention",
    )(log_gamma, rq, rk, qkv, qkv, qkv, qkv, gn_g.reshape(1, RET_V), gn_b.reshape(1, RET_V))


def _diff_kernel(lq1_ref, lk1_ref, lq2_ref, lk2_ref, on_ref, q_ref, k_ref, vt_ref, o_ref,
                 q2_ref, m_ref, l_ref, acc_ref, s0_ref, s1_ref, *, tq, tk, lam_init):
    i = pl.program_id(1)
    st = ATT_STRIP
    assert tq == tk and tq == 2 * st
    q = q_ref[...]
    lane = lax.broadcasted_iota(jnp.int32, q.shape, 1)
    zero = jnp.zeros_like(q)
    q2_ref[0:tq, :] = jnp.where(lane < DIFF_HD, q, zero)
    q2_ref[tq:2 * tq, :] = jnp.where(lane < DIFF_HD, zero, q)
    m_ref[...] = jnp.full_like(m_ref, -jnp.inf)
    l_ref[...] = jnp.zeros_like(l_ref)
    acc_ref[...] = jnp.zeros_like(acc_ref)

    n_strip = 2 * tq // st
    s_bufs = (s0_ref, s1_ref)

    def scores(c, kb, nk):
        ks = pl.multiple_of(kb * tk, tk)
        qc = q2_ref[c * st:(c + 1) * st, :]
        parts = [lax.dot_general(k_ref[pl.ds(ks + r, st), :], qc, (((1,), (1,)), ((), ())),
                                 preferred_element_type=F32) for r in range(0, nk, st)]
        return parts[0] if len(parts) == 1 else jnp.concatenate(parts, axis=0)

    def softmax_pv(c, kb, nk, s, diagonal):
        lanes = slice(c * st, (c + 1) * st)
        if diagonal:
            kchunk = lax.broadcasted_iota(jnp.int32, (nk, st), 0) >> CHUNK_SHIFT
            qchunk = ((c % 2) * st + lax.broadcasted_iota(jnp.int32, (nk, st), 1)) >> CHUNK_SHIFT
            s = jnp.where(kchunk <= qchunk, s, -jnp.inf)
        m_old = m_ref[:, lanes]
        m_new = jnp.maximum(m_old, jnp.max(s, axis=0, keepdims=True))
        alpha = jnp.exp(m_old - m_new)
        p = jnp.exp(s - m_new)
        l_ref[:, lanes] = alpha * l_ref[:, lanes] + jnp.sum(p, axis=0, keepdims=True)
        acc_ref[:, lanes] = alpha * acc_ref[:, lanes] + jnp.dot(
            vt_ref[kb, :, 0:nk], p.astype(BF16), preferred_element_type=F32)
        m_ref[:, lanes] = m_new

    def key_block(kb, diagonal):
        for c in range(n_strip):
            nk = (c % 2 + 1) * st if diagonal else tk
            if c + 1 < n_strip:
                s_bufs[(c + 1) % 2][...] = scores(c + 1, kb, tk)
            elif not diagonal:
                s_bufs[0][...] = scores(0, kb + 1, tk)
            softmax_pv(c, kb, nk, s_bufs[c % 2][0:nk, :], diagonal)

    assert n_strip % 2 == 0
    s_bufs[0][...] = scores(0, 0, tk)

    def body(t, carry):
        for b in range(ATT_UNROLL):
            key_block(ATT_UNROLL * t + b, False)
        return carry

    def tail_body(kb, carry):
        key_block(kb, False)
        return carry

    n_main = i // ATT_UNROLL
    lax.fori_loop(0, n_main, body, 0)
    lax.fori_loop(n_main * ATT_UNROLL, i, tail_body, 0)
    key_block(i, True)

    lam = (jnp.exp(jnp.sum(lq1_ref[...] * lk1_ref[...], axis=-1, keepdims=True))
           - jnp.exp(jnp.sum(lq2_ref[...] * lk2_ref[...], axis=-1, keepdims=True)) + lam_init)
    o = acc_ref[...] / l_ref[...]
    a = o[:, 0:tq] - lam * o[:, tq:2 * tq]
    a = a * lax.rsqrt(jnp.mean(a * a, axis=0, keepdims=True) + EPS)
    o_ref[...] = (a.T * on_ref[...] * (1.0 - lam_init)).astype(o_ref.dtype)


def _diff_attention(dq, dk, vt, lq1, lk1, lq2, lk2, on, lam_init):
    s = dq.shape[0]
    dv = 2 * DIFF_HD
    tq = tk = ATT_TK
    lam_spec = pl.BlockSpec((1, DIFF_HD), lambda h, i: (0, 0))
    return pl.pallas_call(
        functools.partial(_diff_kernel, tq=tq, tk=tk, lam_init=lam_init),
        grid=(DIFF_HEADS, s // tq),
        in_specs=[lam_spec, lam_spec, lam_spec, lam_spec,
                  pl.BlockSpec((1, dv), lambda h, i: (0, 0)),
                  pl.BlockSpec((tq, dv), lambda h, i: (i, h)),
                  pl.BlockSpec((s, dv), lambda h, i: (0, h)),
                  pl.BlockSpec((None, s // tk, dv, tk), lambda h, i: (h, 0, 0, 0))],
        out_specs=pl.BlockSpec((tq, dv), lambda h, i: (i, h)),
        out_shape=jax.ShapeDtypeStruct((s, DIFF_V), BF16),
        scratch_shapes=[pltpu.VMEM((2 * tq, dv), BF16), pltpu.VMEM((1, 2 * tq), F32),
                        pltpu.VMEM((1, 2 * tq), F32), pltpu.VMEM((dv, 2 * tq), F32),
                        pltpu.VMEM((tk, ATT_STRIP), F32), pltpu.VMEM((tk, ATT_STRIP), F32)],
        compiler_params=_cparams(2),
        name="diff_attention",
    )(lq1.reshape(1, DIFF_HD), lk1.reshape(1, DIFF_HD), lq2.reshape(1, DIFF_HD),
      lk2.reshape(1, DIFF_HD), on.reshape(1, dv), dq, dk, vt)


def _merge_kernel(ysb_ref, yret_ref, ydiff_ref, gsb_ref, gret_ref, gdiff_ref, w_ref, o_ref, wbf_ref):
    @pl.when(pl.program_id(1) == 0)
    def _():
        wbf_ref[...] = w_ref[...].astype(BF16)

    r_sb = jnp.dot(ysb_ref[...], wbf_ref[0:SB_W, :], preferred_element_type=F32)
    r_ret = jnp.dot(yret_ref[...], wbf_ref[SB_W:SB_W + RET_V, :], preferred_element_type=F32)
    r_diff = jnp.dot(ydiff_ref[...], wbf_ref[SB_W + RET_V:SB_W + RET_V + DIFF_V, :],
                     preferred_element_type=F32)
    merged = (gsb_ref[...].astype(F32) * r_sb + gret_ref[...].astype(F32) * r_ret
              + gdiff_ref[...].astype(F32) * r_diff)
    o_ref[...] = merged.astype(o_ref.dtype)


def _merge(y_sb, y_ret, y_diff, gates, w_branch, l):
    s = y_sb.shape[0]
    _, kw, n = w_branch.shape
    tm, tn = _tile(s, 1024), _tile(n, 1024)
    nj = n // tn
    gate = lambda b: pl.BlockSpec((tm, tn), lambda j, i, b=b: (i, b * nj + j))
    return pl.pallas_call(
        _merge_kernel,
        grid=(nj, s // tm),
        in_specs=[pl.BlockSpec((tm, SB_W), lambda j, i: (i, 0)),
                  pl.BlockSpec((tm, RET_V), lambda j, i: (i, 0)),
                  pl.BlockSpec((tm, DIFF_V), lambda j, i: (i, 0)),
                  gate(0), gate(1), gate(2),
                  pl.BlockSpec((None, kw, tn), lambda j, i: (l, 0, j))],
        out_specs=pl.BlockSpec((tm, tn), lambda j, i: (i, j)),
        out_shape=jax.ShapeDtypeStruct((s, n), BF16),
        scratch_shapes=[pltpu.VMEM((kw, tn), BF16)],
        compiler_params=_cparams(2),
        name="merge",
    )(y_sb, y_ret, y_diff, gates, gates, gates, w_branch)


def _cast_kernel(w_ref, o_ref):
    o_ref[...] = w_ref[...].astype(o_ref.dtype)


def _to_bf16(w, l):
    _, k, n = w.shape
    tk = _tile(k, 512)
    return pl.pallas_call(
        _cast_kernel,
        grid=(k // tk,),
        in_specs=[pl.BlockSpec((None, tk, n), lambda i: (l, i, 0))],
        out_specs=pl.BlockSpec((tk, n), lambda i: (i, 0)),
        out_shape=jax.ShapeDtypeStruct((k, n), BF16),
        compiler_params=_cparams(1),
        name="to_bf16",
    )(w)


def _residual_bf16_kernel(a_ref, w_ref, x_ref, gate_ref, o_ref):
    r = jnp.dot(a_ref[...], w_ref[...], preferred_element_type=F32)
    o_ref[...] = x_ref[...] + gate_ref[...] * r


def _residual_proj_deep(a, w, l, x, gate, name):
    m, k = a.shape
    n = w.shape[2]
    tm, tn = _tile(m, 512), _tile(n, 1024)
    return pl.pallas_call(
        _residual_bf16_kernel,
        grid=(n // tn, m // tm),
        in_specs=[pl.BlockSpec((tm, k), lambda j, i: (i, 0)),
                  pl.BlockSpec((k, tn), lambda j, i: (0, j)),
                  pl.BlockSpec((tm, tn), lambda j, i: (i, j)),
                  pl.BlockSpec((1, tn), lambda j, i: (0, j))],
        out_specs=pl.BlockSpec((tm, tn), lambda j, i: (i, j)),
        out_shape=jax.ShapeDtypeStruct((m, n), F32),
        compiler_params=_cparams(2),
        name=name,
    )(a, _to_bf16(w, l), x, gate)


def _residual_norm_kernel(a_ref, w_ref, x_ref, gate_ref, g_ref, scale_ref, shift_ref,
                          xo_ref, h_ref):
    r = jnp.dot(a_ref[...], w_ref[...], preferred_element_type=F32)
    x = x_ref[...] + gate_ref[...] * r
    xo_ref[...] = x
    y = x * lax.rsqrt(jnp.mean(x * x, axis=-1, keepdims=True) + EPS)
    y = y * g_ref[...]
    h_ref[...] = (y * (1.0 + scale_ref[...]) + shift_ref[...]).astype(h_ref.dtype)


def _residual_proj_norm(a, w, l, x, gate, g, scale, shift, name):
    m, k = a.shape
    n = w.shape[2]
    tm = _tile(m, 512)
    row = pl.BlockSpec((1, n), lambda i: (0, 0))
    full = pl.BlockSpec((tm, n), lambda i: (i, 0))
    return pl.pallas_call(
        _residual_norm_kernel,
        grid=(m // tm,),
        in_specs=[pl.BlockSpec((tm, k), lambda i: (i, 0)),
                  pl.BlockSpec((k, n), lambda i: (0, 0)),
                  full, row, row, row, row],
        out_specs=[full, full],
        out_shape=[jax.ShapeDtypeStruct((m, n), F32), jax.ShapeDtypeStruct((m, n), BF16)],
        compiler_params=_cparams(1),
        name=name,
    )(a, _to_bf16(w, l), x, gate, g.reshape(1, n), scale, shift)


def _ffn_up_kernel(a_ref, wg_ref, wv_ref, cwg_ref, cwv_ref, cbg_ref, cbv_ref, o_ref,
                   wg_bf_ref, wv_bf_ref, ug_ref, uv_ref, *, tm):
    i = pl.program_id(1)
    halo = SUBLANES

    @pl.when(i == 0)
    def _():
        wg_bf_ref[...] = wg_ref[...].astype(BF16)
        wv_bf_ref[...] = wv_ref[...].astype(BF16)
        ug_ref[0:halo, :] = jnp.zeros((halo, ug_ref.shape[1]), F32)
        uv_ref[0:halo, :] = jnp.zeros((halo, uv_ref.shape[1]), F32)

    a = a_ref[...]

    def conv(u_ref, wbf_ref, cw_ref, cb_ref):
        u_ref[halo:halo + tm, :] = jnp.dot(a, wbf_ref[...], preferred_element_type=F32)
        y = cb_ref[...]
        for t in range(CONV_W):
            shift = CONV_W - 1 - t
            y = y + cw_ref[t:t + 1, :] * u_ref[halo - shift:halo - shift + tm, :]
        u_ref[0:halo, :] = u_ref[tm:tm + halo, :]
        return y

    yg = conv(ug_ref, wg_bf_ref, cwg_ref, cbg_ref)
    yv = conv(uv_ref, wv_bf_ref, cwv_ref, cbv_ref)
    o_ref[...] = (yg * jax.nn.sigmoid(yg) * yv).astype(o_ref.dtype)


def _ffn_up(h2, w_up, conv_w, conv_b, l):
    m, k = h2.shape
    f = w_up.shape[2] // 2
    tm, tn = _tile(m, 1024), _tile(f, 512)
    nj = f // tn
    wspec = lambda half: pl.BlockSpec((None, k, tn), lambda j, i, half=half: (l, 0, half * nj + j))
    cwspec = lambda half: pl.BlockSpec((None, CONV_W, tn), lambda j, i, half=half: (l, 0, half * nj + j))
    cbspec = lambda half: pl.BlockSpec((None, 1, tn), lambda j, i, half=half: (l, 0, half * nj + j))
    return pl.pallas_call(
        functools.partial(_ffn_up_kernel, tm=tm),
        grid=(nj, m // tm),
        in_specs=[pl.BlockSpec((tm, k), lambda j, i: (i, 0)),
                  wspec(0), wspec(1), cwspec(0), cwspec(1), cbspec(0), cbspec(1)],
        out_specs=pl.BlockSpec((tm, tn), lambda j, i: (i, j)),
        out_shape=jax.ShapeDtypeStruct((m, f), BF16),
        scratch_shapes=[pltpu.VMEM((k, tn), BF16), pltpu.VMEM((k, tn), BF16),
                        pltpu.VMEM((tm + SUBLANES, tn), F32), pltpu.VMEM((tm + SUBLANES, tn), F32)],
        compiler_params=_cparams(2),
        name="ffn_up",
    )(h2, w_up, w_up, conv_w, conv_w, conv_b[:, None, :], conv_b[:, None, :])


def kernel(x, c, positions, w_ada, b_ada, norm_mix, w_in, w_gate, ret_gn_g, ret_gn_b, diff_qn, diff_kn, diff_on, lam_q1, lam_k1, lam_q2, lam_k2, w_branch, w_out, norm_ffn, w_up, conv_w, conv_b, w_down):
    batch, s, d = x.shape
    depth = w_ada.shape[0]
    x2d = x.reshape(batch * s, d)
    outs = []
    for b in range(batch):
        xb = x2d[b * s:(b + 1) * s]
        pos = positions[b]
        mod = _adaln(c[b:b + 1], w_ada, b_ada)
        tables = _rope_tables(pos)
        for l in range(depth):
            shift1, scale1, gate1, shift2, scale2, gate2 = [
                mod[l, :, t * d:(t + 1) * d] for t in range(6)]
            h = _norm_mod(xb, norm_mix[l], scale1, shift1)
            qkv = _proj(h, w_in, l, name="in_proj")
            gates = _proj(h, w_gate, l, act="sigmoid", name="gate_proj")
            rq, rk, dq, dk, sb_vt, d_vt = _prep(qkv, tables, diff_qn[l], diff_kn[l])
            y_sb = _sb_attention(qkv, sb_vt)
            y_ret = _retention(qkv, rq, rk, ret_gn_g[l], ret_gn_b[l])
            lam_init = 0.8 - 0.6 * float(np.exp(-0.3 * l))
            y_diff = _diff_attention(dq, dk, d_vt, lam_q1[l], lam_k1[l], lam_q2[l], lam_k2[l],
                                     diff_on[l], lam_init)
            merged = _merge(y_sb, y_ret, y_diff, gates, w_branch, l)
            xb, h2 = _residual_proj_norm(merged, w_out, l, xb, gate1, norm_ffn[l], scale2, shift2,
                                         name="out_proj")
            act = _ffn_up(h2, w_up, conv_w, conv_b, l)
            xb = _residual_proj_deep(act, w_down, l, xb, gate2, name="ffn_down")
        outs.append(xb)
    return jnp.concatenate(outs, axis=0).reshape(batch, s, d)
```

```python
import functools

import jax
import jax.numpy as jnp
import numpy as np
from jax import lax
from jax.experimental import pallas as pl
from jax.experimental.pallas import tpu as pltpu

F32 = jnp.float32
BF16 = jnp.bfloat16

CHUNK = 64
CHUNK_SHIFT = CHUNK.bit_length() - 1
assert 1 << CHUNK_SHIFT == CHUNK
ROPE_THETA = 10000.0
EPS = 1e-6
SB_HEADS, SB_HD = 4, 128
RET_HEADS, RET_DK, RET_DV = 4, 128, 256
DIFF_HEADS, DIFF_HD = 4, 64
CONV_W = 3

SB_W = SB_HEADS * SB_HD
RET_QK = RET_HEADS * RET_DK
RET_V = RET_HEADS * RET_DV
DIFF_QK = DIFF_HEADS * 2 * DIFF_HD
DIFF_V = DIFF_HEADS * 2 * DIFF_HD
OFF_SB_Q = 0
OFF_SB_K = OFF_SB_Q + SB_W
OFF_SB_V = OFF_SB_K + SB_W
OFF_R_Q = OFF_SB_V + SB_W
OFF_R_K = OFF_R_Q + RET_QK
OFF_R_V = OFF_R_K + RET_QK
OFF_R_G = OFF_R_V + RET_V
OFF_D_Q = OFF_R_G + RET_V
OFF_D_K = OFF_D_Q + DIFF_QK
OFF_D_V = OFF_D_K + DIFF_QK
D_IN = OFF_D_V + DIFF_V

V7X_VMEM_BYTES = 64 * 1024 * 1024
VMEM_LIMIT = V7X_VMEM_BYTES - 8 * 1024 * 1024
LANES = 128
ATT_TK = 512
ATT_STRIP = 256
ATT_UNROLL = 4
SB_PROBE_BLOCKS = 2
SB_ZERO_LOG_WEIGHT = -104.0
SUBLANES = 8


def _cparams(n_axes, vmem=VMEM_LIMIT):
    return pltpu.CompilerParams(dimension_semantics=("arbitrary",) * n_axes,
                                vmem_limit_bytes=vmem)


def _tile(n, t):
    t = min(n, t)
    assert n % t == 0, (n, t)
    return t


def _adaln_kernel(c_ref, w_ref, b_ref, o_ref):
    c = c_ref[...]
    ca = c * jax.nn.sigmoid(c)
    o_ref[0] = jnp.sum(w_ref[0] * ca, axis=0, keepdims=True) + b_ref[0]


def _adaln(c, w_ada, b_ada):
    depth, d, n = w_ada.shape
    tn = _tile(n, 1024)
    return pl.pallas_call(
        _adaln_kernel,
        grid=(depth, n // tn),
        in_specs=[pl.BlockSpec((d, 1), lambda l, j: (0, 0)),
                  pl.BlockSpec((1, d, tn), lambda l, j: (l, 0, j)),
                  pl.BlockSpec((1, 1, tn), lambda l, j: (l, 0, j))],
        out_specs=pl.BlockSpec((1, 1, tn), lambda l, j: (l, 0, j)),
        out_shape=jax.ShapeDtypeStruct((depth, 1, n), F32),
        compiler_params=_cparams(2),
        name="adaln",
    )(c.reshape(d, 1), w_ada, b_ada.reshape(depth, 1, n))


def _norm_mod_kernel(x_ref, g_ref, scale_ref, shift_ref, o_ref):
    x = x_ref[...]
    y = x * lax.rsqrt(jnp.mean(x * x, axis=-1, keepdims=True) + EPS)
    y = y * g_ref[...]
    o_ref[...] = (y * (1.0 + scale_ref[...]) + shift_ref[...]).astype(o_ref.dtype)


def _norm_mod(x, g, scale, shift):
    s, d = x.shape
    tm = _tile(s, 1024)
    row = pl.BlockSpec((1, d), lambda i: (0, 0))
    return pl.pallas_call(
        _norm_mod_kernel,
        grid=(s // tm,),
        in_specs=[pl.BlockSpec((tm, d), lambda i: (i, 0)), row, row, row],
        out_specs=pl.BlockSpec((tm, d), lambda i: (i, 0)),
        out_shape=jax.ShapeDtypeStruct((s, d), BF16),
        compiler_params=_cparams(1),
        name="norm_mod",
    )(x, g.reshape(1, d), scale, shift)


def _proj_kernel(a_ref, w_ref, o_ref, wbf_ref, *, act):
    @pl.when(pl.program_id(1) == 0)
    def _():
        wbf_ref[...] = w_ref[...].astype(BF16)

    r = jnp.dot(a_ref[...], wbf_ref[...], preferred_element_type=F32)
    if act == "sigmoid":
        r = jax.nn.sigmoid(r)
    o_ref[...] = r.astype(o_ref.dtype)


def _proj(a, w, l, act=None, name="proj"):
    m, k = a.shape
    n = w.shape[2]
    tm, tn = _tile(m, 2048), _tile(n, 1024)
    return pl.pallas_call(
        functools.partial(_proj_kernel, act=act),
        grid=(n // tn, m // tm),
        in_specs=[pl.BlockSpec((tm, k), lambda j, i: (i, 0)),
                  pl.BlockSpec((None, k, tn), lambda j, i: (l, 0, j))],
        out_specs=pl.BlockSpec((tm, tn), lambda j, i: (i, j)),
        out_shape=jax.ShapeDtypeStruct((m, n), BF16),
        scratch_shapes=[pltpu.VMEM((k, tn), BF16)],
        compiler_params=_cparams(2),
        name=name,
    )(a, w)


def _rope_tables_kernel(pos_ref, inv_ref, cos_r_ref, sin_r_ref, cos_d_ref, sin_d_ref):
    pos = pos_ref[...].astype(F32)
    lane = lax.broadcasted_iota(jnp.int32, (pos.shape[0], LANES), 1)
    ang = pos * inv_ref[...]
    c, s = jnp.cos(ang), jnp.sin(ang)
    half_r, half_d = RET_DK // 2, DIFF_HD // 2

    def ret_table(t):
        return jnp.where(lane < half_r, t, pltpu.roll(t, half_r, 1))

    def diff_table(t):
        d = pltpu.roll(t, LANES - half_r, 1)
        e = jnp.where(lane < half_d, d, pltpu.roll(d, half_d, 1))
        return jnp.where(lane < 2 * half_d, e, pltpu.roll(e, 2 * half_d, 1))

    cos_r_ref[...] = ret_table(c)
    sin_r_ref[...] = jnp.where(lane < half_r, -1.0, 1.0) * ret_table(s)
    cos_d_ref[...] = diff_table(c)
    sin_d_ref[...] = jnp.where((lane & (DIFF_HD - 1)) < half_d, -1.0, 1.0) * diff_table(s)


def _rope_tables(positions):
    s = positions.shape[0]
    tm = _tile(s, 512)
    assert RET_DK == LANES and 2 * DIFF_HD == LANES
    inv_r = ROPE_THETA ** (-jnp.arange(0, RET_DK, 2, dtype=F32) / RET_DK)
    inv_d = ROPE_THETA ** (-jnp.arange(0, DIFF_HD, 2, dtype=F32) / DIFF_HD)
    pad = jnp.zeros((LANES - RET_DK // 2 - DIFF_HD // 2,), F32)
    inv = jnp.concatenate([inv_r, inv_d, pad]).reshape(1, LANES)
    table = pl.BlockSpec((tm, LANES), lambda i: (i, 0))
    return pl.pallas_call(
        _rope_tables_kernel,
        grid=(s // tm,),
        in_specs=[pl.BlockSpec((tm, 1), lambda i: (i, 0)), pl.BlockSpec((1, LANES), lambda i: (0, 0))],
        out_specs=[table] * 4,
        out_shape=[jax.ShapeDtypeStruct((s, LANES), F32)] * 4,
        compiler_params=_cparams(1),
        name="rope_tables",
    )(positions.reshape(s, 1), inv)


def _prep_kernel(cos_r_ref, sin_r_ref, cos_d_ref, sin_d_ref, qn_ref, kn_ref,
                 rq_ref, rk_ref, dq_ref, dk_ref, sbv_ref, dv_ref,
                 orq_ref, ork_ref, odq_ref, odk_ref, osbvt_ref, odvt_ref):
    tm = rq_ref.shape[0]
    lane = lax.broadcasted_iota(jnp.int32, (tm, LANES), 1)

    for v_ref, ovt_ref in ((sbv_ref, osbvt_ref), (dv_ref, odvt_ref)):
        for h in range(ovt_ref.shape[0]):
            vt = v_ref[:, h * LANES:(h + 1) * LANES].astype(F32).T
            for cb in range(tm // ATT_TK):
                ovt_ref[h, cb] = vt[:, cb * ATT_TK:(cb + 1) * ATT_TK].astype(ovt_ref.dtype)

    cos_r, sin_r = cos_r_ref[...], sin_r_ref[...]
    k_scale = RET_DK ** -0.5
    for h in range(RET_HEADS):
        sl = slice(h * RET_DK, (h + 1) * RET_DK)
        xq = rq_ref[:, sl].astype(F32)
        xk = rk_ref[:, sl].astype(F32)
        orq_ref[:, sl] = (xq * cos_r + pltpu.roll(xq, RET_DK // 2, 1) * sin_r).astype(orq_ref.dtype)
        ork_ref[:, sl] = ((xk * cos_r + pltpu.roll(xk, RET_DK // 2, 1) * sin_r) * k_scale).astype(ork_ref.dtype)

    first_half = (lane & (DIFF_HD - 1)) < DIFF_HD // 2
    cos_d, sin_d = cos_d_ref[...], sin_d_ref[...]
    low_map = lane < DIFF_HD
    q_scale = DIFF_HD ** -0.5

    def qk_norm_rope(x, gain):
        x2 = x * x
        ms_lo = jnp.sum(jnp.where(low_map, x2, 0.0), axis=-1, keepdims=True) * (1.0 / DIFF_HD)
        ms_hi = jnp.sum(jnp.where(low_map, 0.0, x2), axis=-1, keepdims=True) * (1.0 / DIFF_HD)
        r = jnp.where(low_map, lax.rsqrt(ms_lo + EPS), lax.rsqrt(ms_hi + EPS))
        y = x * r * gain
        partner = jnp.where(first_half, pltpu.roll(y, LANES - DIFF_HD // 2, 1),
                            pltpu.roll(y, DIFF_HD // 2, 1))
        return y * cos_d + partner * sin_d

    for h in range(DIFF_HEADS):
        sl = slice(h * 2 * DIFF_HD, (h + 1) * 2 * DIFF_HD)
        odq_ref[:, sl] = (qk_norm_rope(dq_ref[:, sl].astype(F32), qn_ref[...]) * q_scale).astype(odq_ref.dtype)
        odk_ref[:, sl] = qk_norm_rope(dk_ref[:, sl].astype(F32), kn_ref[...]).astype(odk_ref.dtype)


def _prep(qkv, tables, qn, kn):
    s = qkv.shape[0]
    tm = _tile(s, 512)
    w = 512
    assert RET_QK == w and DIFF_QK == w
    col = lambda off: pl.BlockSpec((tm, w), lambda i, off=off: (i, off // w))
    row = pl.BlockSpec((1, LANES), lambda i: (0, 0))
    table = pl.BlockSpec((tm, LANES), lambda i: (i, 0))
    out_spec = pl.BlockSpec((tm, w), lambda i: (i, 0))
    out = jax.ShapeDtypeStruct((s, w), BF16)
    assert SB_HD == LANES and 2 * DIFF_HD == LANES and SB_W == w and DIFF_V == w
    nkb = tm // ATT_TK
    vt_spec = pl.BlockSpec((SB_HEADS, nkb, LANES, ATT_TK), lambda i: (0, i, 0, 0))
    vt_out = jax.ShapeDtypeStruct((SB_HEADS, s // ATT_TK, LANES, ATT_TK), BF16)
    return pl.pallas_call(
        _prep_kernel,
        grid=(s // tm,),
        in_specs=[table, table, table, table, row, row,
                  col(OFF_R_Q), col(OFF_R_K), col(OFF_D_Q), col(OFF_D_K),
                  col(OFF_SB_V), col(OFF_D_V)],
        out_specs=[out_spec] * 4 + [vt_spec] * 2,
        out_shape=[out] * 4 + [vt_out] * 2,
        compiler_params=_cparams(1),
        name="rope_prep",
    )(*tables, jnp.tile(qn, 2).reshape(1, LANES), jnp.tile(kn, 2).reshape(1, LANES),
      qkv, qkv, qkv, qkv, qkv, qkv)


def _sb_kernel(q_ref, k_ref, vt_ref, o_ref, acc_ref, csum_ref, z_ref, lb_ref, later_ref, lf0_ref,
               w_ref, *, tq, tk, scale):
    i = pl.program_id(1)
    st = ATT_STRIP
    n_strip, n_half = tq // st, tk // st
    assert tq == tk
    acc_ref[...] = jnp.zeros_like(acc_ref)
    csum_ref[...] = jnp.zeros_like(csum_ref)
    row = lax.broadcasted_iota(jnp.int32, (st, st), 0)
    col = lax.broadcasted_iota(jnp.int32, (st, st), 1)
    tri = jnp.where(col > row, 1.0, 0.0).astype(BF16)

    def scores(c, kb, hf):
        ks = pl.multiple_of(kb * tk, tk)
        k = k_ref[pl.ds(ks + hf * st, st), :]
        return lax.dot_general(k, q_ref[c * st:(c + 1) * st, :], (((1,), (1,)), ((), ())),
                               preferred_element_type=F32)

    def log_probs(raw, masked):
        z = raw * scale
        sp = jnp.log(1.0 + jnp.exp(-jnp.abs(z)))
        log_beta = jnp.minimum(z, 0.0) - sp
        log_fail = log_beta - z
        if masked:
            log_fail = jnp.where(row < col, log_fail, 0.0)
        return log_beta, log_fail.astype(BF16), log_fail[0:1, :]

    def later_sum(lf_bf):
        return jnp.dot(tri, lf_bf, preferred_element_type=F32)

    def weights(c, hf, log_beta, later, lf0, masked):
        lanes = slice(c * st, (c + 1) * st)
        csum = csum_ref[:, lanes]
        w = jnp.exp(log_beta + later + csum)
        if masked:
            w = jnp.where(row < col, w, 0.0)
        w_ref[hf * st:(hf + 1) * st, :] = w.astype(BF16)
        csum_ref[:, lanes] = csum + later[0:1, :] + lf0

    def accumulate(c, kb, n_live):
        lanes = slice(c * st, (c + 1) * st)
        acc_ref[:, lanes] += jnp.dot(vt_ref[kb, :, 0:n_live * st], w_ref[0:n_live * st, :],
                                     preferred_element_type=F32)

    diag_steps = [(c, hf, hf == c, c + 1 if hf == 0 else None)
                  for c in range(n_strip) for hf in reversed(range(c + 1))]
    block_steps = [(c, hf, False, n_half if hf == 0 else None)
                   for c in range(n_strip) for hf in reversed(range(n_half))]
    assert len(block_steps) % 2 == 0

    def stage23(p, raw, masked):
        log_beta, lf_bf, lf0 = log_probs(raw, masked)
        lb_ref[p], lf0_ref[p] = log_beta, lf0
        later_ref[p] = later_sum(lf_bf)

    def run(section, parity):
        for n in range(len(section) - 2):
            p = (parity + n) % 2
            ((c, hf, masked, n_live), kb), ((_, _, masked1, _), _), ((c2, hf2, _, _), kb2) = section[n:n + 3]
            z_ref[p] = scores(c2, kb2, hf2)
            stage23(1 - p, z_ref[1 - p], masked1)
            weights(c, hf, lb_ref[p], later_ref[p], lf0_ref[p], masked)
            if n_live:
                accumulate(c, kb, n_live)

    def at(steps, kb):
        return [(step, kb) for step in steps]

    (c0, hf0, masked0, _), (c1, hf1, _, _) = diag_steps[:2]
    stage23(0, scores(c0, i, hf0), masked0)
    z_ref[1] = scores(c1, i, hf1)
    run(at(diag_steps, i) + at(block_steps[:2], jnp.maximum(i - 1, 0)), 0)

    def sweep(first_kb, n_blocks):
        section = []
        for b in range(n_blocks):
            section += at(block_steps, first_kb - b)
        run(section + at(block_steps[:2], jnp.maximum(first_kb - n_blocks, 0)), len(diag_steps))

    def live():
        return (jnp.max(csum_ref[...]) >= SB_ZERO_LOG_WEIGHT).astype(jnp.int32)

    def sweep_while(alive, first_kb, n_trips, n_blocks):
        def cond(state):
            t, alive = state
            return jnp.logical_and(t < n_trips, alive > 0)

        def body(state):
            t, _ = state
            sweep(first_kb - n_blocks * t, n_blocks)
            return t + 1, live()

        return lax.while_loop(cond, body, (jnp.int32(0), alive))[1]

    n_probe = jnp.minimum(i, SB_PROBE_BLOCKS)
    n_rest = i - n_probe
    alive = sweep_while(jnp.int32(1), i - 1, n_probe, 1)
    alive = sweep_while(alive, n_rest - 1, n_rest // ATT_UNROLL, ATT_UNROLL)
    sweep_while(alive, n_rest % ATT_UNROLL - 1, n_rest % ATT_UNROLL, 1)

    o_ref[...] = acc_ref[...].T.astype(o_ref.dtype)


def _sb_attention(qkv, vt):
    s = qkv.shape[0]
    d = SB_HD
    tq = tk = ATT_TK
    return pl.pallas_call(
        functools.partial(_sb_kernel, tq=tq, tk=tk, scale=d ** -0.5),
        grid=(SB_HEADS, s // tq),
        in_specs=[pl.BlockSpec((tq, d), lambda h, i: (i, OFF_SB_Q // d + h)),
                  pl.BlockSpec((s, d), lambda h, i: (0, OFF_SB_K // d + h)),
                  pl.BlockSpec((None, s // tk, d, tk), lambda h, i: (h, 0, 0, 0))],
        out_specs=pl.BlockSpec((tq, d), lambda h, i: (i, h)),
        out_shape=jax.ShapeDtypeStruct((s, SB_W), BF16),
        scratch_shapes=[pltpu.VMEM((d, tq), F32), pltpu.VMEM((1, tq), F32),
                        pltpu.VMEM((2, ATT_STRIP, ATT_STRIP), F32),
                        pltpu.VMEM((2, ATT_STRIP, ATT_STRIP), F32),
                        pltpu.VMEM((2, ATT_STRIP, ATT_STRIP), F32),
                        pltpu.VMEM((2, 1, ATT_STRIP), F32),
                        pltpu.VMEM((tk, ATT_STRIP), BF16)],
        compiler_params=_cparams(2),
        name="sb_attention",
    )(qkv, qkv, vt)


def _ret_kernel(lg_ref, q_ref, k_ref, v0_ref, v1_ref, g0_ref, g1_ref, gn_g_ref, gn_b_ref, o_ref,
                state_ref, decay_ref, *, tl):
    heads = range(RET_HEADS)
    per_block = RET_HEADS // 2
    r = lax.broadcasted_iota(jnp.int32, (tl, 1), 0).astype(F32)

    def wide(refs, h):
        c0 = (h % per_block) * RET_DV
        return refs[h // per_block][:, c0:c0 + RET_DV]

    @pl.when(pl.program_id(0) == 0)
    def _():
        state_ref[...] = jnp.zeros_like(state_ref)
        n = lax.broadcasted_iota(jnp.int32, (tl, tl), 0)
        m = lax.broadcasted_iota(jnp.int32, (tl, tl), 1)
        dist = jnp.abs(n - m).astype(F32)
        visible = (m >> CHUNK_SHIFT) <= (n >> CHUNK_SHIFT)
        for h in heads:
            decay_ref[h] = jnp.where(visible, jnp.exp(dist * lg_ref[h]), 0.0)

    qs = [q_ref[:, h * RET_DK:(h + 1) * RET_DK] for h in heads]
    ks = [k_ref[:, h * RET_DK:(h + 1) * RET_DK] for h in heads]
    vs = [wide((v0_ref, v1_ref), h) for h in heads]
    scores = [lax.dot_general(qs[h], ks[h], (((1,), (1,)), ((), ())), preferred_element_type=F32)
              for h in heads]
    cross = [jnp.dot(qs[h], state_ref[h].astype(BF16), preferred_element_type=F32) for h in heads]
    pushed = []
    for h in heads:
        zeta = jnp.exp((tl - 1.0 - r) * lg_ref[h])
        kz = (ks[h].astype(F32) * zeta).astype(BF16)
        pushed.append(lax.dot_general(kz, vs[h], (((0,), (0,)), ((), ())),
                                      preferred_element_type=F32))
    intra = [jnp.dot((scores[h] * decay_ref[h]).astype(BF16), vs[h], preferred_element_type=F32)
             for h in heads]
    for h in heads:
        lg = lg_ref[h]
        block_decay = jnp.exp(jnp.full((1, 1), tl, F32) * lg)
        state_ref[h] = state_ref[h] * block_decay + pushed[h]
        y = intra[h] + cross[h] * jnp.exp((r + 1.0) * lg)
        mu = jnp.mean(y, axis=-1, keepdims=True)
        yc = y - mu
        var = jnp.mean(yc * yc, axis=-1, keepdims=True)
        cols = slice(h * RET_DV, (h + 1) * RET_DV)
        yn = yc * lax.rsqrt(var + EPS) * gn_g_ref[:, cols] + gn_b_ref[:, cols]
        gate = wide((g0_ref, g1_ref), h).astype(F32)
        o_ref[:, cols] = (yn * (gate * jax.nn.sigmoid(gate))).astype(o_ref.dtype)


def _retention(qkv, rq, rk, gn_g, gn_b):
    s = qkv.shape[0]
    tl = _tile(s, 256)
    log_gamma = jnp.log(1.0 - 2.0 ** (-5.0 - jnp.arange(RET_HEADS, dtype=F32)))
    half = RET_V // 2
    assert RET_QK == half and OFF_R_V % half == 0 and OFF_R_G % half == 0 and RET_HEADS % 2 == 0
    col = lambda off: pl.BlockSpec((tl, half), lambda b, off=off: (b, off // half))
    row = pl.BlockSpec((1, RET_V), lambda b: (0, 0))
    return pl.pallas_call(
        functools.partial(_ret_kernel, tl=tl),
        grid=(s // tl,),
        in_specs=[pl.BlockSpec(memory_space=pltpu.SMEM), col(0), col(0),
                  col(OFF_R_V), col(OFF_R_V + half), col(OFF_R_G), col(OFF_R_G + half), row, row],
        out_specs=pl.BlockSpec((tl, RET_V), lambda b: (b, 0)),
        out_shape=jax.ShapeDtypeStruct((s, RET_V), BF16),
        scratch_shapes=[pltpu.VMEM((RET_HEADS, RET_DK, RET_DV), F32),
                        pltpu.VMEM((RET_HEADS, tl, tl), F32)],
        compiler_params=_cparams(1),
        name="retention",
    )(log_gamma, rq, rk, qkv, qkv, qkv, qkv, gn_g.reshape(1, RET_V), gn_b.reshape(1, RET_V))


def _diff_kernel(lq1_ref, lk1_ref, lq2_ref, lk2_ref, on_ref, q_ref, k_ref, vt_ref, o_ref,
                 q2_ref, m_ref, l_ref, acc_ref, s0_ref, s1_ref, *, tq, tk, lam_init):
    i = pl.program_id(1)
    st = ATT_STRIP
    assert tq == tk and tq == 2 * st
    q = q_ref[...]
    lane = lax.broadcasted_iota(jnp.int32, q.shape, 1)
    zero = jnp.zeros_like(q)
    q2_ref[0:tq, :] = jnp.where(lane < DIFF_HD, q, zero)
    q2_ref[tq:2 * tq, :] = jnp.where(lane < DIFF_HD, zero, q)
    m_ref[...] = jnp.full_like(m_ref, -jnp.inf)
    l_ref[...] = jnp.zeros_like(l_ref)
    acc_ref[...] = jnp.zeros_like(acc_ref)

    n_strip = 2 * tq // st
    s_bufs = (s0_ref, s1_ref)

    def scores(c, kb, nk):
        ks = pl.multiple_of(kb * tk, tk)
        qc = q2_ref[c * st:(c + 1) * st, :]
        parts = [lax.dot_general(k_ref[pl.ds(ks + r, st), :], qc, (((1,), (1,)), ((), ())),
                                 preferred_element_type=F32) for r in range(0, nk, st)]
        return parts[0] if len(parts) == 1 else jnp.concatenate(parts, axis=0)

    def softmax_pv(c, kb, nk, s, diagonal):
        lanes = slice(c * st, (c + 1) * st)
        if diagonal:
            kchunk = lax.broadcasted_iota(jnp.int32, (nk, st), 0) >> CHUNK_SHIFT
            qchunk = ((c % 2) * st + lax.broadcasted_iota(jnp.int32, (nk, st), 1)) >> CHUNK_SHIFT
            s = jnp.where(kchunk <= qchunk, s, -jnp.inf)
        m_old = m_ref[:, lanes]
        m_new = jnp.maximum(m_old, jnp.max(s, axis=0, keepdims=True))
        alpha = jnp.exp(m_old - m_new)
        p = jnp.exp(s - m_new)
        l_ref[:, lanes] = alpha * l_ref[:, lanes] + jnp.sum(p, axis=0, keepdims=True)
        acc_ref[:, lanes] = alpha * acc_ref[:, lanes] + jnp.dot(
            vt_ref[kb, :, 0:nk], p.astype(BF16), preferred_element_type=F32)
        m_ref[:, lanes] = m_new

    def key_block(kb, diagonal):
        for c in range(n_strip):
            nk = (c % 2 + 1) * st if diagonal else tk
            if c + 1 < n_strip:
                s_bufs[(c + 1) % 2][...] = scores(c + 1, kb, tk)
            elif not diagonal:
                s_bufs[0][...] = scores(0, kb + 1, tk)
            softmax_pv(c, kb, nk, s_bufs[c % 2][0:nk, :], diagonal)

    assert n_strip % 2 == 0
    s_bufs[0][...] = scores(0, 0, tk)

    def body(t, carry):
        for b in range(ATT_UNROLL):
            key_block(ATT_UNROLL * t + b, False)
        return carry

    def tail_body(kb, carry):
        key_block(kb, False)
        return carry

    n_main = i // ATT_UNROLL
    lax.fori_loop(0, n_main, body, 0)
    lax.fori_loop(n_main * ATT_UNROLL, i, tail_body, 0)
    key_block(i, True)

    lam = (jnp.exp(jnp.sum(lq1_ref[...] * lk1_ref[...], axis=-1, keepdims=True))
           - jnp.exp(jnp.sum(lq2_ref[...] * lk2_ref[...], axis=-1, keepdims=True)) + lam_init)
    o = acc_ref[...] / l_ref[...]
    a = o[:, 0:tq] - lam * o[:, tq:2 * tq]
    a = a * lax.rsqrt(jnp.mean(a * a, axis=0, keepdims=True) + EPS)
    o_ref[...] = (a.T * on_ref[...] * (1.0 - lam_init)).astype(o_ref.dtype)


def _diff_attention(dq, dk, vt, lq1, lk1, lq2, lk2, on, lam_init):
    s = dq.shape[0]
    dv = 2 * DIFF_HD
    tq = tk = ATT_TK
    lam_spec = pl.BlockSpec((1, DIFF_HD), lambda h, i: (0, 0))
    return pl.pallas_call(
        functools.partial(_diff_kernel, tq=tq, tk=tk, lam_init=lam_init),
        grid=(DIFF_HEADS, s // tq),
        in_specs=[lam_spec, lam_spec, lam_spec, lam_spec,
                  pl.BlockSpec((1, dv), lambda h, i: (0, 0)),
                  pl.BlockSpec((tq, dv), lambda h, i: (i, h)),
                  pl.BlockSpec((s, dv), lambda h, i: (0, h)),
                  pl.BlockSpec((None, s // tk, dv, tk), lambda h, i: (h, 0, 0, 0))],
        out_specs=pl.BlockSpec((tq, dv), lambda h, i: (i, h)),
        out_shape=jax.ShapeDtypeStruct((s, DIFF_V), BF16),
        scratch_shapes=[pltpu.VMEM((2 * tq, dv), BF16), pltpu.VMEM((1, 2 * tq), F32),
                        pltpu.VMEM((1, 2 * tq), F32), pltpu.VMEM((dv, 2 * tq), F32),
                        pltpu.VMEM((tk, ATT_STRIP), F32), pltpu.VMEM((tk, ATT_STRIP), F32)],
        compiler_params=_cparams(2),
        name="diff_attention",
    )(lq1.reshape(1, DIFF_HD), lk1.reshape(1, DIFF_HD), lq2.reshape(1, DIFF_HD),
      lk2.reshape(1, DIFF_HD), on.reshape(1, dv), dq, dk, vt)


def _merge_kernel(ysb_ref, yret_ref, ydiff_ref, gsb_ref, gret_ref, gdiff_ref, w_ref, o_ref, wbf_ref):
    @pl.when(pl.program_id(1) == 0)
    def _():
        wbf_ref[...] = w_ref[...].astype(BF16)

    r_sb = jnp.dot(ysb_ref[...], wbf_ref[0:SB_W, :], preferred_element_type=F32)
    r_ret = jnp.dot(yret_ref[...], wbf_ref[SB_W:SB_W + RET_V, :], preferred_element_type=F32)
    r_diff = jnp.dot(ydiff_ref[...], wbf_ref[SB_W + RET_V:SB_W + RET_V + DIFF_V, :],
                     preferred_element_type=F32)
    merged = (gsb_ref[...].astype(F32) * r_sb + gret_ref[...].astype(F32) * r_ret
              + gdiff_ref[...].astype(F32) * r_diff)
    o_ref[...] = merged.astype(o_ref.dtype)


def _merge(y_sb, y_ret, y_diff, gates, w_branch, l):
    s = y_sb.shape[0]
    _, kw, n = w_branch.shape
    tm, tn = _tile(s, 1024), _tile(n, 1024)
    nj = n // tn
    gate = lambda b: pl.BlockSpec((tm, tn), lambda j, i, b=b: (i, b * nj + j))
    return pl.pallas_call(
        _merge_kernel,
        grid=(nj, s // tm),
        in_specs=[pl.BlockSpec((tm, SB_W), lambda j, i: (i, 0)),
                  pl.BlockSpec((tm, RET_V), lambda j, i: (i, 0)),
                  pl.BlockSpec((tm, DIFF_V), lambda j, i: (i, 0)),
                  gate(0), gate(1), gate(2),
                  pl.BlockSpec((None, kw, tn), lambda j, i: (l, 0, j))],
        out_specs=pl.BlockSpec((tm, tn), lambda j, i: (i, j)),
        out_shape=jax.ShapeDtypeStruct((s, n), BF16),
        scratch_shapes=[pltpu.VMEM((kw, tn), BF16)],
        compiler_params=_cparams(2),
        name="merge",
    )(y_sb, y_ret, y_diff, gates, gates, gates, w_branch)


def _cast_kernel(w_ref, o_ref):
    o_ref[...] = w_ref[...].astype(o_ref.dtype)


def _to_bf16(w, l):
    _, k, n = w.shape
    tk = _tile(k, 512)
    return pl.pallas_call(
        _cast_kernel,
        grid=(k // tk,),
        in_specs=[pl.BlockSpec((None, tk, n), lambda i: (l, i, 0))],
        out_specs=pl.BlockSpec((tk, n), lambda i: (i, 0)),
        out_shape=jax.ShapeDtypeStruct((k, n), BF16),
        compiler_params=_cparams(1),
        name="to_bf16",
    )(w)


def _residual_bf16_kernel(a_ref, w_ref, x_ref, gate_ref, o_ref):
    r = jnp.dot(a_ref[...], w_ref[...], preferred_element_type=F32)
    o_ref[...] = x_ref[...] + gate_ref[...] * r


def _residual_proj_deep(a, w, l, x, gate, name):
    m, k = a.shape
    n = w.shape[2]
    tm, tn = _tile(m, 512), _tile(n, 1024)
    return pl.pallas_call(
        _residual_bf16_kernel,
        grid=(n // tn, m // tm),
        in_specs=[pl.BlockSpec((tm, k), lambda j, i: (i, 0)),
                  pl.BlockSpec((k, tn), lambda j, i: (0, j)),
                  pl.BlockSpec((tm, tn), lambda j, i: (i, j)),
                  pl.BlockSpec((1, tn), lambda j, i: (0, j))],
        out_specs=pl.BlockSpec((tm, tn), lambda j, i: (i, j)),
        out_shape=jax.ShapeDtypeStruct((m, n), F32),
        compiler_params=_cparams(2),
        name=name,
    )(a, _to_bf16(w, l), x, gate)


def _residual_norm_kernel(a_ref, w_ref, x_ref, gate_ref, g_ref, scale_ref, shift_ref,
                          xo_ref, h_ref):
    r = jnp.dot(a_ref[...], w_ref[...], preferred_element_type=F32)
    x = x_ref[...] + gate_ref[...] * r
    xo_ref[...] = x
    y = x * lax.rsqrt(jnp.mean(x * x, axis=-1, keepdims=True) + EPS)
    y = y * g_ref[...]
    h_ref[...] = (y * (1.0 + scale_ref[...]) + shift_ref[...]).astype(h_ref.dtype)


def _residual_proj_norm(a, w, l, x, gate, g, scale, shift, name):
    m, k = a.shape
    n = w.shape[2]
    tm = _tile(m, 512)
    row = pl.BlockSpec((1, n), lambda i: (0, 0))
    full = pl.BlockSpec((tm, n), lambda i: (i, 0))
    return pl.pallas_call(
        _residual_norm_kernel,
        grid=(m // tm,),
        in_specs=[pl.BlockSpec((tm, k), lambda i: (i, 0)),
                  pl.BlockSpec((k, n), lambda i: (0, 0)),
                  full, row, row, row, row],
        out_specs=[full, full],
        out_shape=[jax.ShapeDtypeStruct((m, n), F32), jax.ShapeDtypeStruct((m, n), BF16)],
        compiler_params=_cparams(1),
        name=name,
    )(a, _to_bf16(w, l), x, gate, g.reshape(1, n), scale, shift)


def _ffn_up_kernel(a_ref, wg_ref, wv_ref, cwg_ref, cwv_ref, cbg_ref, cbv_ref, o_ref,
                   wg_bf_ref, wv_bf_ref, ug_ref, uv_ref, *, tm):
    i = pl.program_id(1)
    halo = SUBLANES

    @pl.when(i == 0)
    def _():
        wg_bf_ref[...] = wg_ref[...].astype(BF16)
        wv_bf_ref[...] = wv_ref[...].astype(BF16)
        ug_ref[0:halo, :] = jnp.zeros((halo, ug_ref.shape[1]), F32)
        uv_ref[0:halo, :] = jnp.zeros((halo, uv_ref.shape[1]), F32)

    a = a_ref[...]

    def conv(u_ref, wbf_ref, cw_ref, cb_ref):
        u = jnp.dot(a, wbf_ref[...], preferred_element_type=F32)
        u_ref[halo:halo + tm, :] = u
        y = cb_ref[...] + cw_ref[CONV_W - 1:CONV_W, :] * u
        top = cb_ref[...] + cw_ref[CONV_W - 1:CONV_W, :] * u[0:halo]
        for t in range(CONV_W - 1):
            shift = CONV_W - 1 - t
            y = y + cw_ref[t:t + 1, :] * pltpu.roll(u, shift, 0)
            top = top + cw_ref[t:t + 1, :] * u_ref[halo - shift:2 * halo - shift, :]
        u_ref[0:halo, :] = u_ref[tm:tm + halo, :]
        return top, y

    top_g, yg = conv(ug_ref, wg_bf_ref, cwg_ref, cbg_ref)
    top_v, yv = conv(uv_ref, wv_bf_ref, cwv_ref, cbv_ref)
    act = yg * jax.nn.sigmoid(yg) * yv
    act_top = top_g * jax.nn.sigmoid(top_g) * top_v
    o_ref[...] = act.astype(o_ref.dtype)
    o_ref[0:2 * halo, :] = jnp.concatenate([act_top, act[halo:2 * halo]], axis=0).astype(o_ref.dtype)


def _ffn_up(h2, w_up, conv_w, conv_b, l):
    m, k = h2.shape
    f = w_up.shape[2] // 2
    tm, tn = _tile(m, 1024), _tile(f, 512)
    nj = f // tn
    wspec = lambda half: pl.BlockSpec((None, k, tn), lambda j, i, half=half: (l, 0, half * nj + j))
    cwspec = lambda half: pl.BlockSpec((None, CONV_W, tn), lambda j, i, half=half: (l, 0, half * nj + j))
    cbspec = lambda half: pl.BlockSpec((None, 1, tn), lambda j, i, half=half: (l, 0, half * nj + j))
    return pl.pallas_call(
        functools.partial(_ffn_up_kernel, tm=tm),
        grid=(nj, m // tm),
        in_specs=[pl.BlockSpec((tm, k), lambda j, i: (i, 0)),
                  wspec(0), wspec(1), cwspec(0), cwspec(1), cbspec(0), cbspec(1)],
        out_specs=pl.BlockSpec((tm, tn), lambda j, i: (i, j)),
        out_shape=jax.ShapeDtypeStruct((m, f), BF16),
        scratch_shapes=[pltpu.VMEM((k, tn), BF16), pltpu.VMEM((k, tn), BF16),
                        pltpu.VMEM((tm + SUBLANES, tn), F32), pltpu.VMEM((tm + SUBLANES, tn), F32)],
        compiler_params=_cparams(2),
        name="ffn_up",
    )(h2, w_up, w_up, conv_w, conv_w, conv_b[:, None, :], conv_b[:, None, :])


def kernel(x, c, positions, w_ada, b_ada, norm_mix, w_in, w_gate, ret_gn_g, ret_gn_b, diff_qn, diff_kn, diff_on, lam_q1, lam_k1, lam_q2, lam_k2, w_branch, w_out, norm_ffn, w_up, conv_w, conv_b, w_down):
    batch, s, d = x.shape
    depth = w_ada.shape[0]
    x2d = x.reshape(batch * s, d)
    outs = []
    for b in range(batch):
        xb = x2d[b * s:(b + 1) * s]
        pos = positions[b]
        mod = _adaln(c[b:b + 1], w_ada, b_ada)
        tables = _rope_tables(pos)
        for l in range(depth):
            shift1, scale1, gate1, shift2, scale2, gate2 = [
                mod[l, :, t * d:(t + 1) * d] for t in range(6)]
            h = _norm_mod(xb, norm_mix[l], scale1, shift1)
            qkv = _proj(h, w_in, l, name="in_proj")
            gates = _proj(h, w_gate, l, act="sigmoid", name="gate_proj")
            rq, rk, dq, dk, sb_vt, d_vt = _prep(qkv, tables, diff_qn[l], diff_kn[l])
            y_sb = _sb_attention(qkv, sb_vt)
            y_ret = _retention(qkv, rq, rk, ret_gn_g[l], ret_gn_b[l])
            lam_init = 0.8 - 0.6 * float(np.exp(-0.3 * l))
            y_diff = _diff_attention(dq, dk, d_vt, lam_q1[l], lam_k1[l], lam_q2[l], lam_k2[l],
                                     diff_on[l], lam_init)
            merged = _merge(y_sb, y_ret, y_diff, gates, w_branch, l)
            xb, h2 = _residual_proj_norm(merged, w_out, l, xb, gate1, norm_ffn[l], scale2, shift2,
                                         name="out_proj")
            act = _ffn_up(h2, w_up, conv_w, conv_b, l)
            xb = _residual_proj_deep(act, w_down, l, xb, gate2, name="ffn_down")
        outs.append(xb)
    return jnp.concatenate(outs, axis=0).reshape(batch, s, d)
```

```python
import functools

import jax
import jax.numpy as jnp
import numpy as np
from jax import lax
from jax.experimental import pallas as pl
from jax.experimental.pallas import tpu as pltpu

F32 = jnp.float32
BF16 = jnp.bfloat16

CHUNK = 64
CHUNK_SHIFT = CHUNK.bit_length() - 1
assert 1 << CHUNK_SHIFT == CHUNK
ROPE_THETA = 10000.0
LOG2_E = float(np.log2(np.e))
EPS = 1e-6
SB_HEADS, SB_HD = 4, 128
RET_HEADS, RET_DK, RET_DV = 4, 128, 256
DIFF_HEADS, DIFF_HD = 4, 64
CONV_W = 3

SB_W = SB_HEADS * SB_HD
RET_QK = RET_HEADS * RET_DK
RET_V = RET_HEADS * RET_DV
DIFF_QK = DIFF_HEADS * 2 * DIFF_HD
DIFF_V = DIFF_HEADS * 2 * DIFF_HD
OFF_SB_Q = 0
OFF_SB_K = OFF_SB_Q + SB_W
OFF_SB_V = OFF_SB_K + SB_W
OFF_R_Q = OFF_SB_V + SB_W
OFF_R_K = OFF_R_Q + RET_QK
OFF_R_V = OFF_R_K + RET_QK
OFF_R_G = OFF_R_V + RET_V
OFF_D_Q = OFF_R_G + RET_V
OFF_D_K = OFF_D_Q + DIFF_QK
OFF_D_V = OFF_D_K + DIFF_QK
D_IN = OFF_D_V + DIFF_V

V7X_VMEM_BYTES = 64 * 1024 * 1024
VMEM_LIMIT = V7X_VMEM_BYTES - 8 * 1024 * 1024
LANES = 128
ATT_TK = 512
ATT_STRIP = 256
ATT_UNROLL = 4
SB_PROBE_BLOCKS = 2
SB_ZERO_LOG_WEIGHT = -104.0
SUBLANES = 8


def _cparams(n_axes, vmem=VMEM_LIMIT):
    return pltpu.CompilerParams(dimension_semantics=("arbitrary",) * n_axes,
                                vmem_limit_bytes=vmem)


def _tile(n, t):
    t = min(n, t)
    assert n % t == 0, (n, t)
    return t


def _adaln_kernel(c_ref, w_ref, b_ref, o_ref):
    c = c_ref[...]
    ca = c * jax.nn.sigmoid(c)
    o_ref[0] = jnp.sum(w_ref[0] * ca, axis=0, keepdims=True) + b_ref[0]


def _adaln(c, w_ada, b_ada):
    depth, d, n = w_ada.shape
    tn = _tile(n, 1024)
    return pl.pallas_call(
        _adaln_kernel,
        grid=(depth, n // tn),
        in_specs=[pl.BlockSpec((d, 1), lambda l, j: (0, 0)),
                  pl.BlockSpec((1, d, tn), lambda l, j: (l, 0, j)),
                  pl.BlockSpec((1, 1, tn), lambda l, j: (l, 0, j))],
        out_specs=pl.BlockSpec((1, 1, tn), lambda l, j: (l, 0, j)),
        out_shape=jax.ShapeDtypeStruct((depth, 1, n), F32),
        compiler_params=_cparams(2),
        name="adaln",
    )(c.reshape(d, 1), w_ada, b_ada.reshape(depth, 1, n))


def _norm_mod_kernel(x_ref, g_ref, scale_ref, shift_ref, o_ref):
    x = x_ref[...]
    y = x * lax.rsqrt(jnp.mean(x * x, axis=-1, keepdims=True) + EPS)
    y = y * g_ref[...]
    o_ref[...] = (y * (1.0 + scale_ref[...]) + shift_ref[...]).astype(o_ref.dtype)


def _norm_mod(x, g, scale, shift):
    s, d = x.shape
    tm = _tile(s, 1024)
    row = pl.BlockSpec((1, d), lambda i: (0, 0))
    return pl.pallas_call(
        _norm_mod_kernel,
        grid=(s // tm,),
        in_specs=[pl.BlockSpec((tm, d), lambda i: (i, 0)), row, row, row],
        out_specs=pl.BlockSpec((tm, d), lambda i: (i, 0)),
        out_shape=jax.ShapeDtypeStruct((s, d), BF16),
        compiler_params=_cparams(1),
        name="norm_mod",
    )(x, g.reshape(1, d), scale, shift)


def _proj_kernel(a_ref, w_ref, o_ref, wbf_ref, *, act):
    @pl.when(pl.program_id(1) == 0)
    def _():
        wbf_ref[...] = w_ref[...].astype(BF16)

    r = jnp.dot(a_ref[...], wbf_ref[...], preferred_element_type=F32)
    if act == "sigmoid":
        r = jax.nn.sigmoid(r)
    o_ref[...] = r.astype(o_ref.dtype)


def _proj(a, w, l, act=None, name="proj"):
    m, k = a.shape
    n = w.shape[2]
    tm, tn = _tile(m, 2048), _tile(n, 1024)
    return pl.pallas_call(
        functools.partial(_proj_kernel, act=act),
        grid=(n // tn, m // tm),
        in_specs=[pl.BlockSpec((tm, k), lambda j, i: (i, 0)),
                  pl.BlockSpec((None, k, tn), lambda j, i: (l, 0, j))],
        out_specs=pl.BlockSpec((tm, tn), lambda j, i: (i, j)),
        out_shape=jax.ShapeDtypeStruct((m, n), BF16),
        scratch_shapes=[pltpu.VMEM((k, tn), BF16)],
        compiler_params=_cparams(2),
        name=name,
    )(a, w)


def _rope_tables_kernel(pos_ref, inv_ref, cos_r_ref, sin_r_ref, cos_d_ref, sin_d_ref):
    pos = pos_ref[...].astype(F32)
    lane = lax.broadcasted_iota(jnp.int32, (pos.shape[0], LANES), 1)
    ang = pos * inv_ref[...]
    c, s = jnp.cos(ang), jnp.sin(ang)
    half_r, half_d = RET_DK // 2, DIFF_HD // 2

    def ret_table(t):
        return jnp.where(lane < half_r, t, pltpu.roll(t, half_r, 1))

    def diff_table(t):
        d = pltpu.roll(t, LANES - half_r, 1)
        e = jnp.where(lane < half_d, d, pltpu.roll(d, half_d, 1))
        return jnp.where(lane < 2 * half_d, e, pltpu.roll(e, 2 * half_d, 1))

    cos_r_ref[...] = ret_table(c)
    sin_r_ref[...] = jnp.where(lane < half_r, -1.0, 1.0) * ret_table(s)
    cos_d_ref[...] = diff_table(c)
    sin_d_ref[...] = jnp.where((lane & (DIFF_HD - 1)) < half_d, -1.0, 1.0) * diff_table(s)


def _rope_tables(positions):
    s = positions.shape[0]
    tm = _tile(s, 512)
    assert RET_DK == LANES and 2 * DIFF_HD == LANES
    inv_r = ROPE_THETA ** (-jnp.arange(0, RET_DK, 2, dtype=F32) / RET_DK)
    inv_d = ROPE_THETA ** (-jnp.arange(0, DIFF_HD, 2, dtype=F32) / DIFF_HD)
    pad = jnp.zeros((LANES - RET_DK // 2 - DIFF_HD // 2,), F32)
    inv = jnp.concatenate([inv_r, inv_d, pad]).reshape(1, LANES)
    table = pl.BlockSpec((tm, LANES), lambda i: (i, 0))
    return pl.pallas_call(
        _rope_tables_kernel,
        grid=(s // tm,),
        in_specs=[pl.BlockSpec((tm, 1), lambda i: (i, 0)), pl.BlockSpec((1, LANES), lambda i: (0, 0))],
        out_specs=[table] * 4,
        out_shape=[jax.ShapeDtypeStruct((s, LANES), F32)] * 4,
        compiler_params=_cparams(1),
        name="rope_tables",
    )(positions.reshape(s, 1), inv)


def _prep_kernel(cos_r_ref, sin_r_ref, cos_d_ref, sin_d_ref, qn_ref, kn_ref,
                 rq_ref, rk_ref, dq_ref, dk_ref, sbv_ref, dv_ref,
                 orq_ref, ork_ref, odq_ref, odk_ref, osbvt_ref, odvt_ref):
    tm = rq_ref.shape[0]
    lane = lax.broadcasted_iota(jnp.int32, (tm, LANES), 1)

    for v_ref, ovt_ref in ((sbv_ref, osbvt_ref), (dv_ref, odvt_ref)):
        for h in range(ovt_ref.shape[0]):
            vt = v_ref[:, h * LANES:(h + 1) * LANES].astype(F32).T
            for cb in range(tm // ATT_TK):
                ovt_ref[h, cb] = vt[:, cb * ATT_TK:(cb + 1) * ATT_TK].astype(ovt_ref.dtype)

    cos_r, sin_r = cos_r_ref[...], sin_r_ref[...]
    k_scale = RET_DK ** -0.5
    for h in range(RET_HEADS):
        sl = slice(h * RET_DK, (h + 1) * RET_DK)
        xq = rq_ref[:, sl].astype(F32)
        xk = rk_ref[:, sl].astype(F32)
        orq_ref[:, sl] = (xq * cos_r + pltpu.roll(xq, RET_DK // 2, 1) * sin_r).astype(orq_ref.dtype)
        ork_ref[:, sl] = ((xk * cos_r + pltpu.roll(xk, RET_DK // 2, 1) * sin_r) * k_scale).astype(ork_ref.dtype)

    first_half = (lane & (DIFF_HD - 1)) < DIFF_HD // 2
    cos_d, sin_d = cos_d_ref[...], sin_d_ref[...]
    low_map = lane < DIFF_HD
    q_scale = DIFF_HD ** -0.5 * LOG2_E

    def qk_norm_rope(x, gain):
        x2 = x * x
        ms_lo = jnp.sum(jnp.where(low_map, x2, 0.0), axis=-1, keepdims=True) * (1.0 / DIFF_HD)
        ms_hi = jnp.sum(jnp.where(low_map, 0.0, x2), axis=-1, keepdims=True) * (1.0 / DIFF_HD)
        r = jnp.where(low_map, lax.rsqrt(ms_lo + EPS), lax.rsqrt(ms_hi + EPS))
        y = x * r * gain
        partner = jnp.where(first_half, pltpu.roll(y, LANES - DIFF_HD // 2, 1),
                            pltpu.roll(y, DIFF_HD // 2, 1))
        return y * cos_d + partner * sin_d

    for h in range(DIFF_HEADS):
        sl = slice(h * 2 * DIFF_HD, (h + 1) * 2 * DIFF_HD)
        odq_ref[:, sl] = (qk_norm_rope(dq_ref[:, sl].astype(F32), qn_ref[...]) * q_scale).astype(odq_ref.dtype)
        odk_ref[:, sl] = qk_norm_rope(dk_ref[:, sl].astype(F32), kn_ref[...]).astype(odk_ref.dtype)


def _prep(qkv, tables, qn, kn):
    s = qkv.shape[0]
    tm = _tile(s, 512)
    w = 512
    assert RET_QK == w and DIFF_QK == w
    col = lambda off: pl.BlockSpec((tm, w), lambda i, off=off: (i, off // w))
    row = pl.BlockSpec((1, LANES), lambda i: (0, 0))
    table = pl.BlockSpec((tm, LANES), lambda i: (i, 0))
    out_spec = pl.BlockSpec((tm, w), lambda i: (i, 0))
    out = jax.ShapeDtypeStruct((s, w), BF16)
    assert SB_HD == LANES and 2 * DIFF_HD == LANES and SB_W == w and DIFF_V == w
    nkb = tm // ATT_TK
    vt_spec = pl.BlockSpec((SB_HEADS, nkb, LANES, ATT_TK), lambda i: (0, i, 0, 0))
    vt_out = jax.ShapeDtypeStruct((SB_HEADS, s // ATT_TK, LANES, ATT_TK), BF16)
    return pl.pallas_call(
        _prep_kernel,
        grid=(s // tm,),
        in_specs=[table, table, table, table, row, row,
                  col(OFF_R_Q), col(OFF_R_K), col(OFF_D_Q), col(OFF_D_K),
                  col(OFF_SB_V), col(OFF_D_V)],
        out_specs=[out_spec] * 4 + [vt_spec] * 2,
        out_shape=[out] * 4 + [vt_out] * 2,
        compiler_params=_cparams(1),
        name="rope_prep",
    )(*tables, jnp.tile(qn, 2).reshape(1, LANES), jnp.tile(kn, 2).reshape(1, LANES),
      qkv, qkv, qkv, qkv, qkv, qkv)


def _sb_kernel(q_ref, k_ref, vt_ref, o_ref, acc_ref, csum_ref, z_ref, lb_ref, later_ref, lf0_ref,
               w_ref, *, tq, tk, scale):
    i = pl.program_id(1)
    st = ATT_STRIP
    n_strip, n_half = tq // st, tk // st
    assert tq == tk
    acc_ref[...] = jnp.zeros_like(acc_ref)
    csum_ref[...] = jnp.zeros_like(csum_ref)
    row = lax.broadcasted_iota(jnp.int32, (st, st), 0)
    col = lax.broadcasted_iota(jnp.int32, (st, st), 1)
    tri = jnp.where(col > row, 1.0, 0.0).astype(BF16)

    def scores(c, kb, hf):
        ks = pl.multiple_of(kb * tk, tk)
        k = k_ref[pl.ds(ks + hf * st, st), :]
        return lax.dot_general(k, q_ref[c * st:(c + 1) * st, :], (((1,), (1,)), ((), ())),
                               preferred_element_type=F32)

    def log_probs(raw, masked):
        z = raw * scale
        sp = jnp.log(1.0 + jnp.exp(-jnp.abs(z)))
        log_beta = jnp.minimum(z, 0.0) - sp
        log_fail = log_beta - z
        if masked:
            log_fail = jnp.where(row < col, log_fail, 0.0)
        return log_beta, log_fail.astype(BF16), log_fail[0:1, :]

    def later_sum(lf_bf):
        return jnp.dot(tri, lf_bf, preferred_element_type=F32)

    def weights(c, hf, log_beta, later, lf0, masked):
        lanes = slice(c * st, (c + 1) * st)
        csum = csum_ref[:, lanes]
        w = jnp.exp(log_beta + later + csum)
        if masked:
            w = jnp.where(row < col, w, 0.0)
        w_ref[hf * st:(hf + 1) * st, :] = w.astype(BF16)
        csum_ref[:, lanes] = csum + later[0:1, :] + lf0

    def accumulate(c, kb, n_live):
        lanes = slice(c * st, (c + 1) * st)
        acc_ref[:, lanes] += jnp.dot(vt_ref[kb, :, 0:n_live * st], w_ref[0:n_live * st, :],
                                     preferred_element_type=F32)

    diag_steps = [(c, hf, hf == c, c + 1 if hf == 0 else None)
                  for c in range(n_strip) for hf in reversed(range(c + 1))]
    block_steps = [(c, hf, False, n_half if hf == 0 else None)
                   for c in range(n_strip) for hf in reversed(range(n_half))]
    assert len(block_steps) % 2 == 0

    def stage23(p, raw, masked):
        log_beta, lf_bf, lf0 = log_probs(raw, masked)
        lb_ref[p], lf0_ref[p] = log_beta, lf0
        later_ref[p] = later_sum(lf_bf)

    def run(section, parity):
        for n in range(len(section) - 2):
            p = (parity + n) % 2
            ((c, hf, masked, n_live), kb), ((_, _, masked1, _), _), ((c2, hf2, _, _), kb2) = section[n:n + 3]
            z_ref[p] = scores(c2, kb2, hf2)
            stage23(1 - p, z_ref[1 - p], masked1)
            weights(c, hf, lb_ref[p], later_ref[p], lf0_ref[p], masked)
            if n_live:
                accumulate(c, kb, n_live)

    def at(steps, kb):
        return [(step, kb) for step in steps]

    (c0, hf0, masked0, _), (c1, hf1, _, _) = diag_steps[:2]
    stage23(0, scores(c0, i, hf0), masked0)
    z_ref[1] = scores(c1, i, hf1)
    run(at(diag_steps, i) + at(block_steps[:2], jnp.maximum(i - 1, 0)), 0)

    def sweep(first_kb, n_blocks):
        section = []
        for b in range(n_blocks):
            section += at(block_steps, first_kb - b)
        run(section + at(block_steps[:2], jnp.maximum(first_kb - n_blocks, 0)), len(diag_steps))

    def live():
        return (jnp.max(csum_ref[...]) >= SB_ZERO_LOG_WEIGHT).astype(jnp.int32)

    def sweep_while(alive, first_kb, n_trips, n_blocks):
        def cond(state):
            t, alive = state
            return jnp.logical_and(t < n_trips, alive > 0)

        def body(state):
            t, _ = state
            sweep(first_kb - n_blocks * t, n_blocks)
            return t + 1, live()

        return lax.while_loop(cond, body, (jnp.int32(0), alive))[1]

    n_probe = jnp.minimum(i, SB_PROBE_BLOCKS)
    n_rest = i - n_probe
    alive = sweep_while(jnp.int32(1), i - 1, n_probe, 1)
    alive = sweep_while(alive, n_rest - 1, n_rest // ATT_UNROLL, ATT_UNROLL)
    sweep_while(alive, n_rest % ATT_UNROLL - 1, n_rest % ATT_UNROLL, 1)

    o_ref[...] = acc_ref[...].T.astype(o_ref.dtype)


def _sb_attention(qkv, vt):
    s = qkv.shape[0]
    d = SB_HD
    tq = tk = ATT_TK
    return pl.pallas_call(
        functools.partial(_sb_kernel, tq=tq, tk=tk, scale=d ** -0.5),
        grid=(SB_HEADS, s // tq),
        in_specs=[pl.BlockSpec((tq, d), lambda h, i: (i, OFF_SB_Q // d + h)),
                  pl.BlockSpec((s, d), lambda h, i: (0, OFF_SB_K // d + h)),
                  pl.BlockSpec((None, s // tk, d, tk), lambda h, i: (h, 0, 0, 0))],
        out_specs=pl.BlockSpec((tq, d), lambda h, i: (i, h)),
        out_shape=jax.ShapeDtypeStruct((s, SB_W), BF16),
        scratch_shapes=[pltpu.VMEM((d, tq), F32), pltpu.VMEM((1, tq), F32),
                        pltpu.VMEM((2, ATT_STRIP, ATT_STRIP), F32),
                        pltpu.VMEM((2, ATT_STRIP, ATT_STRIP), F32),
                        pltpu.VMEM((2, ATT_STRIP, ATT_STRIP), F32),
                        pltpu.VMEM((2, 1, ATT_STRIP), F32),
                        pltpu.VMEM((tk, ATT_STRIP), BF16)],
        compiler_params=_cparams(2),
        name="sb_attention",
    )(qkv, qkv, vt)


def _ret_kernel(lg_ref, q_ref, k_ref, v0_ref, v1_ref, g0_ref, g1_ref, gn_g_ref, gn_b_ref, o_ref,
                state_ref, decay_ref, *, tl):
    heads = range(RET_HEADS)
    per_block = RET_HEADS // 2
    r = lax.broadcasted_iota(jnp.int32, (tl, 1), 0).astype(F32)

    def wide(refs, h):
        c0 = (h % per_block) * RET_DV
        return refs[h // per_block][:, c0:c0 + RET_DV]

    @pl.when(pl.program_id(0) == 0)
    def _():
        state_ref[...] = jnp.zeros_like(state_ref)
        n = lax.broadcasted_iota(jnp.int32, (tl, tl), 0)
        m = lax.broadcasted_iota(jnp.int32, (tl, tl), 1)
        dist = jnp.abs(n - m).astype(F32)
        visible = (m >> CHUNK_SHIFT) <= (n >> CHUNK_SHIFT)
        for h in heads:
            decay_ref[h] = jnp.where(visible, jnp.exp(dist * lg_ref[h]), 0.0)

    qs = [q_ref[:, h * RET_DK:(h + 1) * RET_DK] for h in heads]
    ks = [k_ref[:, h * RET_DK:(h + 1) * RET_DK] for h in heads]
    vs = [wide((v0_ref, v1_ref), h) for h in heads]
    scores = [lax.dot_general(qs[h], ks[h], (((1,), (1,)), ((), ())), preferred_element_type=F32)
              for h in heads]
    cross = [jnp.dot(qs[h], state_ref[h].astype(BF16), preferred_element_type=F32) for h in heads]
    pushed = []
    for h in heads:
        zeta = jnp.exp((tl - 1.0 - r) * lg_ref[h])
        kz = (ks[h].astype(F32) * zeta).astype(BF16)
        pushed.append(lax.dot_general(kz, vs[h], (((0,), (0,)), ((), ())),
                                      preferred_element_type=F32))
    intra = [jnp.dot((scores[h] * decay_ref[h]).astype(BF16), vs[h], preferred_element_type=F32)
             for h in heads]
    for h in heads:
        lg = lg_ref[h]
        block_decay = jnp.exp(jnp.full((1, 1), tl, F32) * lg)
        state_ref[h] = state_ref[h] * block_decay + pushed[h]
        y = intra[h] + cross[h] * jnp.exp((r + 1.0) * lg)
        mu = jnp.mean(y, axis=-1, keepdims=True)
        yc = y - mu
        var = jnp.mean(yc * yc, axis=-1, keepdims=True)
        cols = slice(h * RET_DV, (h + 1) * RET_DV)
        yn = yc * lax.rsqrt(var + EPS) * gn_g_ref[:, cols] + gn_b_ref[:, cols]
        gate = wide((g0_ref, g1_ref), h).astype(F32)
        o_ref[:, cols] = (yn * (gate * jax.nn.sigmoid(gate))).astype(o_ref.dtype)


def _retention(qkv, rq, rk, gn_g, gn_b):
    s = qkv.shape[0]
    tl = _tile(s, 256)
    log_gamma = jnp.log(1.0 - 2.0 ** (-5.0 - jnp.arange(RET_HEADS, dtype=F32)))
    half = RET_V // 2
    assert RET_QK == half and OFF_R_V % half == 0 and OFF_R_G % half == 0 and RET_HEADS % 2 == 0
    col = lambda off: pl.BlockSpec((tl, half), lambda b, off=off: (b, off // half))
    row = pl.BlockSpec((1, RET_V), lambda b: (0, 0))
    return pl.pallas_call(
        functools.partial(_ret_kernel, tl=tl),
        grid=(s // tl,),
        in_specs=[pl.BlockSpec(memory_space=pltpu.SMEM), col(0), col(0),
                  col(OFF_R_V), col(OFF_R_V + half), col(OFF_R_G), col(OFF_R_G + half), row, row],
        out_specs=pl.BlockSpec((tl, RET_V), lambda b: (b, 0)),
        out_shape=jax.ShapeDtypeStruct((s, RET_V), BF16),
        scratch_shapes=[pltpu.VMEM((RET_HEADS, RET_DK, RET_DV), F32),
                        pltpu.VMEM((RET_HEADS, tl, tl), F32)],
        compiler_params=_cparams(1),
        name="retention",
    )(log_gamma, rq, rk, qkv, qkv, qkv, qkv, gn_g.reshape(1, RET_V), gn_b.reshape(1, RET_V))


def _diff_kernel(lq1_ref, lk1_ref, lq2_ref, lk2_ref, on_ref, q_ref, k_ref, vt_ref, o_ref,
                 q2_ref, m_ref, l_ref, acc_ref, s0_ref, s1_ref, *, tq, tk, lam_init):
    i = pl.program_id(1)
    st = ATT_STRIP
    assert tq == tk and tq == 2 * st
    q = q_ref[...]
    lane = lax.broadcasted_iota(jnp.int32, q.shape, 1)
    zero = jnp.zeros_like(q)
    q2_ref[0:tq, :] = jnp.where(lane < DIFF_HD, q, zero)
    q2_ref[tq:2 * tq, :] = jnp.where(lane < DIFF_HD, zero, q)
    m_ref[...] = jnp.full_like(m_ref, -jnp.inf)
    l_ref[...] = jnp.zeros_like(l_ref)
    acc_ref[...] = jnp.zeros_like(acc_ref)

    n_strip = 2 * tq // st
    s_bufs = (s0_ref, s1_ref)

    def scores(c, kb, nk):
        ks = pl.multiple_of(kb * tk, tk)
        qc = q2_ref[c * st:(c + 1) * st, :]
        parts = [lax.dot_general(k_ref[pl.ds(ks + r, st), :], qc, (((1,), (1,)), ((), ())),
                                 preferred_element_type=F32) for r in range(0, nk, st)]
        return parts[0] if len(parts) == 1 else jnp.concatenate(parts, axis=0)

    def softmax_pv(c, kb, nk, s, diagonal):
        lanes = slice(c * st, (c + 1) * st)
        if diagonal:
            kchunk = lax.broadcasted_iota(jnp.int32, (nk, st), 0) >> CHUNK_SHIFT
            qchunk = ((c % 2) * st + lax.broadcasted_iota(jnp.int32, (nk, st), 1)) >> CHUNK_SHIFT
            s = jnp.where(kchunk <= qchunk, s, -jnp.inf)
        m_old = m_ref[:, lanes]
        m_new = jnp.maximum(m_old, jnp.max(s, axis=0, keepdims=True))
        alpha = jnp.exp2(m_old - m_new)
        p = jnp.exp2(s - m_new)
        l_ref[:, lanes] = alpha * l_ref[:, lanes] + jnp.sum(p, axis=0, keepdims=True)
        acc_ref[:, lanes] = alpha * acc_ref[:, lanes] + jnp.dot(
            vt_ref[kb, :, 0:nk], p.astype(BF16), preferred_element_type=F32)
        m_ref[:, lanes] = m_new

    def key_block(kb, diagonal):
        for c in range(n_strip):
            nk = (c % 2 + 1) * st if diagonal else tk
            if c + 1 < n_strip:
                s_bufs[(c + 1) % 2][...] = scores(c + 1, kb, tk)
            elif not diagonal:
                s_bufs[0][...] = scores(0, kb + 1, tk)
            softmax_pv(c, kb, nk, s_bufs[c % 2][0:nk, :], diagonal)

    assert n_strip % 2 == 0
    s_bufs[0][...] = scores(0, 0, tk)

    def body(t, carry):
        for b in range(ATT_UNROLL):
            key_block(ATT_UNROLL * t + b, False)
        return carry

    def tail_body(kb, carry):
        key_block(kb, False)
        return carry

    n_main = i // ATT_UNROLL
    lax.fori_loop(0, n_main, body, 0)
    lax.fori_loop(n_main * ATT_UNROLL, i, tail_body, 0)
    key_block(i, True)

    lam = (jnp.exp(jnp.sum(lq1_ref[...] * lk1_ref[...], axis=-1, keepdims=True))
           - jnp.exp(jnp.sum(lq2_ref[...] * lk2_ref[...], axis=-1, keepdims=True)) + lam_init)
    o = acc_ref[...] / l_ref[...]
    a = o[:, 0:tq] - lam * o[:, tq:2 * tq]
    a = a * lax.rsqrt(jnp.mean(a * a, axis=0, keepdims=True) + EPS)
    o_ref[...] = (a.T * on_ref[...] * (1.0 - lam_init)).astype(o_ref.dtype)


def _diff_attention(dq, dk, vt, lq1, lk1, lq2, lk2, on, lam_init):
    s = dq.shape[0]
    dv = 2 * DIFF_HD
    tq = tk = ATT_TK
    lam_spec = pl.BlockSpec((1, DIFF_HD), lambda h, i: (0, 0))
    return pl.pallas_call(
        functools.partial(_diff_kernel, tq=tq, tk=tk, lam_init=lam_init),
        grid=(DIFF_HEADS, s // tq),
        in_specs=[lam_spec, lam_spec, lam_spec, lam_spec,
                  pl.BlockSpec((1, dv), lambda h, i: (0, 0)),
                  pl.BlockSpec((tq, dv), lambda h, i: (i, h)),
                  pl.BlockSpec((s, dv), lambda h, i: (0, h)),
                  pl.BlockSpec((None, s // tk, dv, tk), lambda h, i: (h, 0, 0, 0))],
        out_specs=pl.BlockSpec((tq, dv), lambda h, i: (i, h)),
        out_shape=jax.ShapeDtypeStruct((s, DIFF_V), BF16),
        scratch_shapes=[pltpu.VMEM((2 * tq, dv), BF16), pltpu.VMEM((1, 2 * tq), F32),
                        pltpu.VMEM((1, 2 * tq), F32), pltpu.VMEM((dv, 2 * tq), F32),
                        pltpu.VMEM((tk, ATT_STRIP), F32), pltpu.VMEM((tk, ATT_STRIP), F32)],
        compiler_params=_cparams(2),
        name="diff_attention",
    )(lq1.reshape(1, DIFF_HD), lk1.reshape(1, DIFF_HD), lq2.reshape(1, DIFF_HD),
      lk2.reshape(1, DIFF_HD), on.reshape(1, dv), dq, dk, vt)


def _merge_kernel(ysb_ref, yret_ref, ydiff_ref, gsb_ref, gret_ref, gdiff_ref, w_ref, o_ref, wbf_ref):
    @pl.when(pl.program_id(1) == 0)
    def _():
        wbf_ref[...] = w_ref[...].astype(BF16)

    r_sb = jnp.dot(ysb_ref[...], wbf_ref[0:SB_W, :], preferred_element_type=F32)
    r_ret = jnp.dot(yret_ref[...], wbf_ref[SB_W:SB_W + RET_V, :], preferred_element_type=F32)
    r_diff = jnp.dot(ydiff_ref[...], wbf_ref[SB_W + RET_V:SB_W + RET_V + DIFF_V, :],
                     preferred_element_type=F32)
    merged = (gsb_ref[...].astype(F32) * r_sb + gret_ref[...].astype(F32) * r_ret
              + gdiff_ref[...].astype(F32) * r_diff)
    o_ref[...] = merged.astype(o_ref.dtype)


def _merge(y_sb, y_ret, y_diff, gates, w_branch, l):
    s = y_sb.shape[0]
    _, kw, n = w_branch.shape
    tm, tn = _tile(s, 1024), _tile(n, 1024)
    nj = n // tn
    gate = lambda b: pl.BlockSpec((tm, tn), lambda j, i, b=b: (i, b * nj + j))
    return pl.pallas_call(
        _merge_kernel,
        grid=(nj, s // tm),
        in_specs=[pl.BlockSpec((tm, SB_W), lambda j, i: (i, 0)),
                  pl.BlockSpec((tm, RET_V), lambda j, i: (i, 0)),
                  pl.BlockSpec((tm, DIFF_V), lambda j, i: (i, 0)),
                  gate(0), gate(1), gate(2),
                  pl.BlockSpec((None, kw, tn), lambda j, i: (l, 0, j))],
        out_specs=pl.BlockSpec((tm, tn), lambda j, i: (i, j)),
        out_shape=jax.ShapeDtypeStruct((s, n), BF16),
        scratch_shapes=[pltpu.VMEM((kw, tn), BF16)],
        compiler_params=_cparams(2),
        name="merge",
    )(y_sb, y_ret, y_diff, gates, gates, gates, w_branch)


def _cast_kernel(w_ref, o_ref):
    o_ref[...] = w_ref[...].astype(o_ref.dtype)


def _to_bf16(w, l):
    _, k, n = w.shape
    tk = _tile(k, 512)
    return pl.pallas_call(
        _cast_kernel,
        grid=(k // tk,),
        in_specs=[pl.BlockSpec((None, tk, n), lambda i: (l, i, 0))],
        out_specs=pl.BlockSpec((tk, n), lambda i: (i, 0)),
        out_shape=jax.ShapeDtypeStruct((k, n), BF16),
        compiler_params=_cparams(1),
        name="to_bf16",
    )(w)


def _residual_bf16_kernel(a_ref, w_ref, x_ref, gate_ref, o_ref):
    r = jnp.dot(a_ref[...], w_ref[...], preferred_element_type=F32)
    o_ref[...] = x_ref[...] + gate_ref[...] * r


def _residual_proj_deep(a, w, l, x, gate, name):
    m, k = a.shape
    n = w.shape[2]
    tm, tn = _tile(m, 512), _tile(n, 1024)
    return pl.pallas_call(
        _residual_bf16_kernel,
        grid=(n // tn, m // tm),
        in_specs=[pl.BlockSpec((tm, k), lambda j, i: (i, 0)),
                  pl.BlockSpec((k, tn), lambda j, i: (0, j)),
                  pl.BlockSpec((tm, tn), lambda j, i: (i, j)),
                  pl.BlockSpec((1, tn), lambda j, i: (0, j))],
        out_specs=pl.BlockSpec((tm, tn), lambda j, i: (i, j)),
        out_shape=jax.ShapeDtypeStruct((m, n), F32),
        compiler_params=_cparams(2),
        name=name,
    )(a, _to_bf16(w, l), x, gate)


def _residual_norm_kernel(a_ref, w_ref, x_ref, gate_ref, g_ref, scale_ref, shift_ref,
                          xo_ref, h_ref):
    r = jnp.dot(a_ref[...], w_ref[...], preferred_element_type=F32)
    x = x_ref[...] + gate_ref[...] * r
    xo_ref[...] = x
    y = x * lax.rsqrt(jnp.mean(x * x, axis=-1, keepdims=True) + EPS)
    y = y * g_ref[...]
    h_ref[...] = (y * (1.0 + scale_ref[...]) + shift_ref[...]).astype(h_ref.dtype)


def _residual_proj_norm(a, w, l, x, gate, g, scale, shift, name):
    m, k = a.shape
    n = w.shape[2]
    tm = _tile(m, 512)
    row = pl.BlockSpec((1, n), lambda i: (0, 0))
    full = pl.BlockSpec((tm, n), lambda i: (i, 0))
    return pl.pallas_call(
        _residual_norm_kernel,
        grid=(m // tm,),
        in_specs=[pl.BlockSpec((tm, k), lambda i: (i, 0)),
                  pl.BlockSpec((k, n), lambda i: (0, 0)),
                  full, row, row, row, row],
        out_specs=[full, full],
        out_shape=[jax.ShapeDtypeStruct((m, n), F32), jax.ShapeDtypeStruct((m, n), BF16)],
        compiler_params=_cparams(1),
        name=name,
    )(a, _to_bf16(w, l), x, gate, g.reshape(1, n), scale, shift)


def _ffn_up_kernel(a_ref, wg_ref, wv_ref, cwg_ref, cwv_ref, cbg_ref, cbv_ref, o_ref,
                   wg_bf_ref, wv_bf_ref, ug_ref, uv_ref, *, tm):
    i = pl.program_id(1)
    halo = SUBLANES

    @pl.when(i == 0)
    def _():
        wg_bf_ref[...] = wg_ref[...].astype(BF16)
        wv_bf_ref[...] = wv_ref[...].astype(BF16)
        ug_ref[0:halo, :] = jnp.zeros((halo, ug_ref.shape[1]), F32)
        uv_ref[0:halo, :] = jnp.zeros((halo, uv_ref.shape[1]), F32)

    a = a_ref[...]

    def conv(u_ref, wbf_ref, cw_ref, cb_ref):
        u = jnp.dot(a, wbf_ref[...], preferred_element_type=F32)
        u_ref[halo:halo + tm, :] = u
        y = cb_ref[...] + cw_ref[CONV_W - 1:CONV_W, :] * u
        top = cb_ref[...] + cw_ref[CONV_W - 1:CONV_W, :] * u[0:halo]
        for t in range(CONV_W - 1):
            shift = CONV_W - 1 - t
            y = y + cw_ref[t:t + 1, :] * pltpu.roll(u, shift, 0)
            top = top + cw_ref[t:t + 1, :] * u_ref[halo - shift:2 * halo - shift, :]
        u_ref[0:halo, :] = u_ref[tm:tm + halo, :]
        return top, y

    top_g, yg = conv(ug_ref, wg_bf_ref, cwg_ref, cbg_ref)
    top_v, yv = conv(uv_ref, wv_bf_ref, cwv_ref, cbv_ref)
    act = yg * jax.nn.sigmoid(yg) * yv
    act_top = top_g * jax.nn.sigmoid(top_g) * top_v
    o_ref[...] = act.astype(o_ref.dtype)
    o_ref[0:2 * halo, :] = jnp.concatenate([act_top, act[halo:2 * halo]], axis=0).astype(o_ref.dtype)


def _ffn_up(h2, w_up, conv_w, conv_b, l):
    m, k = h2.shape
    f = w_up.shape[2] // 2
    tm, tn = _tile(m, 1024), _tile(f, 512)
    nj = f // tn
    wspec = lambda half: pl.BlockSpec((None, k, tn), lambda j, i, half=half: (l, 0, half * nj + j))
    cwspec = lambda half: pl.BlockSpec((None, CONV_W, tn), lambda j, i, half=half: (l, 0, half * nj + j))
    cbspec = lambda half: pl.BlockSpec((None, 1, tn), lambda j, i, half=half: (l, 0, half * nj + j))
    return pl.pallas_call(
        functools.partial(_ffn_up_kernel, tm=tm),
        grid=(nj, m // tm),
        in_specs=[pl.BlockSpec((tm, k), lambda j, i: (i, 0)),
                  wspec(0), wspec(1), cwspec(0), cwspec(1), cbspec(0), cbspec(1)],
        out_specs=pl.BlockSpec((tm, tn), lambda j, i: (i, j)),
        out_shape=jax.ShapeDtypeStruct((m, f), BF16),
        scratch_shapes=[pltpu.VMEM((k, tn), BF16), pltpu.VMEM((k, tn), BF16),
                        pltpu.VMEM((tm + SUBLANES, tn), F32), pltpu.VMEM((tm + SUBLANES, tn), F32)],
        compiler_params=_cparams(2),
        name="ffn_up",
    )(h2, w_up, w_up, conv_w, conv_w, conv_b[:, None, :], conv_b[:, None, :])


def kernel(x, c, positions, w_ada, b_ada, norm_mix, w_in, w_gate, ret_gn_g, ret_gn_b, diff_qn, diff_kn, diff_on, lam_q1, lam_k1, lam_q2, lam_k2, w_branch, w_out, norm_ffn, w_up, conv_w, conv_b, w_down):
    batch, s, d = x.shape
    depth = w_ada.shape[0]
    x2d = x.reshape(batch * s, d)
    outs = []
    for b in range(batch):
        xb = x2d[b * s:(b + 1) * s]
        pos = positions[b]
        mod = _adaln(c[b:b + 1], w_ada, b_ada)
        tables = _rope_tables(pos)
        for l in range(depth):
            shift1, scale1, gate1, shift2, scale2, gate2 = [
                mod[l, :, t * d:(t + 1) * d] for t in range(6)]
            h = _norm_mod(xb, norm_mix[l], scale1, shift1)
            qkv = _proj(h, w_in, l, name="in_proj")
            gates = _proj(h, w_gate, l, act="sigmoid", name="gate_proj")
            rq, rk, dq, dk, sb_vt, d_vt = _prep(qkv, tables, diff_qn[l], diff_kn[l])
            y_sb = _sb_attention(qkv, sb_vt)
            y_ret = _retention(qkv, rq, rk, ret_gn_g[l], ret_gn_b[l])
            lam_init = 0.8 - 0.6 * float(np.exp(-0.3 * l))
            y_diff = _diff_attention(dq, dk, d_vt, lam_q1[l], lam_k1[l], lam_q2[l], lam_k2[l],
                                     diff_on[l], lam_init)
            merged = _merge(y_sb, y_ret, y_diff, gates, w_branch, l)
            xb, h2 = _residual_proj_norm(merged, w_out, l, xb, gate1, norm_ffn[l], scale2, shift2,
                                         name="out_proj")
            act = _ffn_up(h2, w_up, conv_w, conv_b, l)
            xb = _residual_proj_deep(act, w_down, l, xb, gate2, name="ffn_down")
        outs.append(xb)
    return jnp.concatenate(outs, axis=0).reshape(batch, s, d)
```

```python
import functools

import jax
import jax.numpy as jnp
import numpy as np
from jax import lax
from jax.experimental import pallas as pl
from jax.experimental.pallas import tpu as pltpu

F32 = jnp.float32
BF16 = jnp.bfloat16

CHUNK = 64
CHUNK_SHIFT = CHUNK.bit_length() - 1
assert 1 << CHUNK_SHIFT == CHUNK
ROPE_THETA = 10000.0
EPS = 1e-6
SB_HEADS, SB_HD = 4, 128
RET_HEADS, RET_DK, RET_DV = 4, 128, 256
DIFF_HEADS, DIFF_HD = 4, 64
CONV_W = 3

SB_W = SB_HEADS * SB_HD
RET_QK = RET_HEADS * RET_DK
RET_V = RET_HEADS * RET_DV
DIFF_QK = DIFF_HEADS * 2 * DIFF_HD
DIFF_V = DIFF_HEADS * 2 * DIFF_HD
OFF_SB_Q = 0
OFF_SB_K = OFF_SB_Q + SB_W
OFF_SB_V = OFF_SB_K + SB_W
OFF_R_Q = OFF_SB_V + SB_W
OFF_R_K = OFF_R_Q + RET_QK
OFF_R_V = OFF_R_K + RET_QK
OFF_R_G = OFF_R_V + RET_V
OFF_D_Q = OFF_R_G + RET_V
OFF_D_K = OFF_D_Q + DIFF_QK
OFF_D_V = OFF_D_K + DIFF_QK
D_IN = OFF_D_V + DIFF_V

V7X_VMEM_BYTES = 64 * 1024 * 1024
VMEM_LIMIT = V7X_VMEM_BYTES - 8 * 1024 * 1024
LANES = 128
ATT_TK = 512
ATT_STRIP = 256
ATT_UNROLL = 4
SB_PROBE_BLOCKS = 2
SB_ZERO_LOG_WEIGHT = -104.0
SUBLANES = 8


def _cparams(n_axes, vmem=VMEM_LIMIT):
    return pltpu.CompilerParams(dimension_semantics=("arbitrary",) * n_axes,
                                vmem_limit_bytes=vmem)


def _tile(n, t):
    t = min(n, t)
    assert n % t == 0, (n, t)
    return t


def _adaln_kernel(c_ref, w_ref, b_ref, o_ref):
    c = c_ref[...]
    ca = c * jax.nn.sigmoid(c)
    o_ref[0] = jnp.sum(w_ref[0] * ca, axis=0, keepdims=True) + b_ref[0]


def _adaln(c, w_ada, b_ada):
    depth, d, n = w_ada.shape
    tn = _tile(n, 1024)
    return pl.pallas_call(
        _adaln_kernel,
        grid=(depth, n // tn),
        in_specs=[pl.BlockSpec((d, 1), lambda l, j: (0, 0)),
                  pl.BlockSpec((1, d, tn), lambda l, j: (l, 0, j)),
                  pl.BlockSpec((1, 1, tn), lambda l, j: (l, 0, j))],
        out_specs=pl.BlockSpec((1, 1, tn), lambda l, j: (l, 0, j)),
        out_shape=jax.ShapeDtypeStruct((depth, 1, n), F32),
        compiler_params=_cparams(2),
        name="adaln",
    )(c.reshape(d, 1), w_ada, b_ada.reshape(depth, 1, n))


def _norm_mod_kernel(x_ref, g_ref, scale_ref, shift_ref, o_ref):
    x = x_ref[...]
    y = x * lax.rsqrt(jnp.mean(x * x, axis=-1, keepdims=True) + EPS)
    y = y * g_ref[...]
    o_ref[...] = (y * (1.0 + scale_ref[...]) + shift_ref[...]).astype(o_ref.dtype)


def _norm_mod(x, g, scale, shift):
    s, d = x.shape
    tm = _tile(s, 1024)
    row = pl.BlockSpec((1, d), lambda i: (0, 0))
    return pl.pallas_call(
        _norm_mod_kernel,
        grid=(s // tm,),
        in_specs=[pl.BlockSpec((tm, d), lambda i: (i, 0)), row, row, row],
        out_specs=pl.BlockSpec((tm, d), lambda i: (i, 0)),
        out_shape=jax.ShapeDtypeStruct((s, d), BF16),
        compiler_params=_cparams(1),
        name="norm_mod",
    )(x, g.reshape(1, d), scale, shift)


def _proj_kernel(a_ref, w_ref, o_ref, wbf_ref, *, act):
    @pl.when(pl.program_id(1) == 0)
    def _():
        wbf_ref[...] = w_ref[...].astype(BF16)

    r = jnp.dot(a_ref[...], wbf_ref[...], preferred_element_type=F32)
    if act == "sigmoid":
        r = jax.nn.sigmoid(r)
    o_ref[...] = r.astype(o_ref.dtype)


def _proj(a, w, l, act=None, name="proj"):
    m, k = a.shape
    n = w.shape[2]
    tm, tn = _tile(m, 2048), _tile(n, 1024)
    return pl.pallas_call(
        functools.partial(_proj_kernel, act=act),
        grid=(n // tn, m // tm),
        in_specs=[pl.BlockSpec((tm, k), lambda j, i: (i, 0)),
                  pl.BlockSpec((None, k, tn), lambda j, i: (l, 0, j))],
        out_specs=pl.BlockSpec((tm, tn), lambda j, i: (i, j)),
        out_shape=jax.ShapeDtypeStruct((m, n), BF16),
        scratch_shapes=[pltpu.VMEM((k, tn), BF16)],
        compiler_params=_cparams(2),
        name=name,
    )(a, w)


def _rope_tables_kernel(pos_ref, inv_ref, cos_r_ref, sin_r_ref, cos_d_ref, sin_d_ref):
    pos = pos_ref[...].astype(F32)
    lane = lax.broadcasted_iota(jnp.int32, (pos.shape[0], LANES), 1)
    ang = pos * inv_ref[...]
    c, s = jnp.cos(ang), jnp.sin(ang)
    half_r, half_d = RET_DK // 2, DIFF_HD // 2

    def ret_table(t):
        return jnp.where(lane < half_r, t, pltpu.roll(t, half_r, 1))

    def diff_table(t):
        d = pltpu.roll(t, LANES - half_r, 1)
        e = jnp.where(lane < half_d, d, pltpu.roll(d, half_d, 1))
        return jnp.where(lane < 2 * half_d, e, pltpu.roll(e, 2 * half_d, 1))

    cos_r_ref[...] = ret_table(c)
    sin_r_ref[...] = jnp.where(lane < half_r, -1.0, 1.0) * ret_table(s)
    cos_d_ref[...] = diff_table(c)
    sin_d_ref[...] = jnp.where((lane & (DIFF_HD - 1)) < half_d, -1.0, 1.0) * diff_table(s)


def _rope_tables(positions):
    s = positions.shape[0]
    tm = _tile(s, 512)
    assert RET_DK == LANES and 2 * DIFF_HD == LANES
    inv_r = ROPE_THETA ** (-jnp.arange(0, RET_DK, 2, dtype=F32) / RET_DK)
    inv_d = ROPE_THETA ** (-jnp.arange(0, DIFF_HD, 2, dtype=F32) / DIFF_HD)
    pad = jnp.zeros((LANES - RET_DK // 2 - DIFF_HD // 2,), F32)
    inv = jnp.concatenate([inv_r, inv_d, pad]).reshape(1, LANES)
    table = pl.BlockSpec((tm, LANES), lambda i: (i, 0))
    return pl.pallas_call(
        _rope_tables_kernel,
        grid=(s // tm,),
        in_specs=[pl.BlockSpec((tm, 1), lambda i: (i, 0)), pl.BlockSpec((1, LANES), lambda i: (0, 0))],
        out_specs=[table] * 4,
        out_shape=[jax.ShapeDtypeStruct((s, LANES), F32)] * 4,
        compiler_params=_cparams(1),
        name="rope_tables",
    )(positions.reshape(s, 1), inv)


def _prep_kernel(cos_r_ref, sin_r_ref, cos_d_ref, sin_d_ref, qn_ref, kn_ref,
                 rq_ref, rk_ref, dq_ref, dk_ref, sbv_ref, dv_ref,
                 orq_ref, ork_ref, odq_ref, odk_ref, osbvt_ref, odvt_ref):
    tm = rq_ref.shape[0]
    lane = lax.broadcasted_iota(jnp.int32, (tm, LANES), 1)

    for v_ref, ovt_ref in ((sbv_ref, osbvt_ref), (dv_ref, odvt_ref)):
        for h in range(ovt_ref.shape[0]):
            vt = v_ref[:, h * LANES:(h + 1) * LANES].astype(F32).T
            for cb in range(tm // ATT_TK):
                ovt_ref[h, cb] = vt[:, cb * ATT_TK:(cb + 1) * ATT_TK].astype(ovt_ref.dtype)

    cos_r, sin_r = cos_r_ref[...], sin_r_ref[...]
    k_scale = RET_DK ** -0.5
    for h in range(RET_HEADS):
        sl = slice(h * RET_DK, (h + 1) * RET_DK)
        xq = rq_ref[:, sl].astype(F32)
        xk = rk_ref[:, sl].astype(F32)
        orq_ref[:, sl] = (xq * cos_r + pltpu.roll(xq, RET_DK // 2, 1) * sin_r).astype(orq_ref.dtype)
        ork_ref[:, sl] = ((xk * cos_r + pltpu.roll(xk, RET_DK // 2, 1) * sin_r) * k_scale).astype(ork_ref.dtype)

    first_half = (lane & (DIFF_HD - 1)) < DIFF_HD // 2
    cos_d, sin_d = cos_d_ref[...], sin_d_ref[...]
    low_map = lane < DIFF_HD
    q_scale = DIFF_HD ** -0.5

    def qk_norm_rope(x, gain):
        x2 = x * x
        ms_lo = jnp.sum(jnp.where(low_map, x2, 0.0), axis=-1, keepdims=True) * (1.0 / DIFF_HD)
        ms_hi = jnp.sum(jnp.where(low_map, 0.0, x2), axis=-1, keepdims=True) * (1.0 / DIFF_HD)
        r = jnp.where(low_map, lax.rsqrt(ms_lo + EPS), lax.rsqrt(ms_hi + EPS))
        y = x * r * gain
        partner = jnp.where(first_half, pltpu.roll(y, LANES - DIFF_HD // 2, 1),
                            pltpu.roll(y, DIFF_HD // 2, 1))
        return y * cos_d + partner * sin_d

    for h in range(DIFF_HEADS):
        sl = slice(h * 2 * DIFF_HD, (h + 1) * 2 * DIFF_HD)
        odq_ref[:, sl] = (qk_norm_rope(dq_ref[:, sl].astype(F32), qn_ref[...]) * q_scale).astype(odq_ref.dtype)
        odk_ref[:, sl] = qk_norm_rope(dk_ref[:, sl].astype(F32), kn_ref[...]).astype(odk_ref.dtype)


def _prep(qkv, tables, qn, kn):
    s = qkv.shape[0]
    tm = _tile(s, 512)
    w = 512
    assert RET_QK == w and DIFF_QK == w
    col = lambda off: pl.BlockSpec((tm, w), lambda i, off=off: (i, off // w))
    row = pl.BlockSpec((1, LANES), lambda i: (0, 0))
    table = pl.BlockSpec((tm, LANES), lambda i: (i, 0))
    out_spec = pl.BlockSpec((tm, w), lambda i: (i, 0))
    out = jax.ShapeDtypeStruct((s, w), BF16)
    assert SB_HD == LANES and 2 * DIFF_HD == LANES and SB_W == w and DIFF_V == w
    nkb = tm // ATT_TK
    vt_spec = pl.BlockSpec((SB_HEADS, nkb, LANES, ATT_TK), lambda i: (0, i, 0, 0))
    vt_out = jax.ShapeDtypeStruct((SB_HEADS, s // ATT_TK, LANES, ATT_TK), BF16)
    return pl.pallas_call(
        _prep_kernel,
        grid=(s // tm,),
        in_specs=[table, table, table, table, row, row,
                  col(OFF_R_Q), col(OFF_R_K), col(OFF_D_Q), col(OFF_D_K),
                  col(OFF_SB_V), col(OFF_D_V)],
        out_specs=[out_spec] * 4 + [vt_spec] * 2,
        out_shape=[out] * 4 + [vt_out] * 2,
        compiler_params=_cparams(1),
        name="rope_prep",
    )(*tables, jnp.tile(qn, 2).reshape(1, LANES), jnp.tile(kn, 2).reshape(1, LANES),
      qkv, qkv, qkv, qkv, qkv, qkv)


def _sb_kernel(q_ref, k_ref, vt_ref, o_ref, acc_ref, csum_ref, z_ref, lb_ref, later_ref, lf0_ref,
               w_ref, *, tq, tk, scale):
    i = pl.program_id(1)
    st = ATT_STRIP
    n_strip, n_half = tq // st, tk // st
    assert tq == tk
    acc_ref[...] = jnp.zeros_like(acc_ref)
    csum_ref[...] = jnp.zeros_like(csum_ref)
    row = lax.broadcasted_iota(jnp.int32, (st, st), 0)
    col = lax.broadcasted_iota(jnp.int32, (st, st), 1)
    tri = jnp.where(col > row, 1.0, 0.0).astype(BF16)

    def scores(c, kb, hf):
        ks = pl.multiple_of(kb * tk, tk)
        k = k_ref[pl.ds(ks + hf * st, st), :]
        return lax.dot_general(k, q_ref[c * st:(c + 1) * st, :], (((1,), (1,)), ((), ())),
                               preferred_element_type=F32)

    def log_probs(raw, masked):
        z = raw * scale
        sp = jnp.log(1.0 + jnp.exp(-jnp.abs(z)))
        log_beta = jnp.minimum(z, 0.0) - sp
        log_fail = log_beta - z
        if masked:
            log_fail = jnp.where(row < col, log_fail, 0.0)
        return log_beta, log_fail.astype(BF16), log_fail[0:1, :]

    def later_sum(lf_bf):
        return jnp.dot(tri, lf_bf, preferred_element_type=F32)

    def weights(c, hf, log_beta, later, lf0, masked):
        lanes = slice(c * st, (c + 1) * st)
        csum = csum_ref[:, lanes]
        w = jnp.exp(log_beta + later + csum)
        if masked:
            w = jnp.where(row < col, w, 0.0)
        w_ref[hf * st:(hf + 1) * st, :] = w.astype(BF16)
        csum_ref[:, lanes] = csum + later[0:1, :] + lf0

    def accumulate(c, kb, n_live):
        lanes = slice(c * st, (c + 1) * st)
        acc_ref[:, lanes] += jnp.dot(vt_ref[kb, :, 0:n_live * st], w_ref[0:n_live * st, :],
                                     preferred_element_type=F32)

    diag_steps = [(c, hf, hf == c, c + 1 if hf == 0 else None)
                  for c in range(n_strip) for hf in reversed(range(c + 1))]
    block_steps = [(c, hf, False, n_half if hf == 0 else None)
                   for c in range(n_strip) for hf in reversed(range(n_half))]
    assert len(block_steps) % 2 == 0

    def stage23(p, raw, masked):
        log_beta, lf_bf, lf0 = log_probs(raw, masked)
        lb_ref[p], lf0_ref[p] = log_beta, lf0
        later_ref[p] = later_sum(lf_bf)

    def run(section, parity):
        for n in range(len(section) - 2):
            p = (parity + n) % 2
            ((c, hf, masked, n_live), kb), ((_, _, masked1, _), _), ((c2, hf2, _, _), kb2) = section[n:n + 3]
            z_ref[p] = scores(c2, kb2, hf2)
            stage23(1 - p, z_ref[1 - p], masked1)
            weights(c, hf, lb_ref[p], later_ref[p], lf0_ref[p], masked)
            if n_live:
                accumulate(c, kb, n_live)

    def at(steps, kb):
        return [(step, kb) for step in steps]

    (c0, hf0, masked0, _), (c1, hf1, _, _) = diag_steps[:2]
    stage23(0, scores(c0, i, hf0), masked0)
    z_ref[1] = scores(c1, i, hf1)
    run(at(diag_steps, i) + at(block_steps[:2], jnp.maximum(i - 1, 0)), 0)

    def sweep(first_kb, n_blocks):
        section = []
        for b in range(n_blocks):
            section += at(block_steps, first_kb - b)
        run(section + at(block_steps[:2], jnp.maximum(first_kb - n_blocks, 0)), len(diag_steps))

    def live():
        return (jnp.max(csum_ref[...]) >= SB_ZERO_LOG_WEIGHT).astype(jnp.int32)

    def sweep_while(alive, first_kb, n_trips, n_blocks):
        def cond(state):
            t, alive = state
            return jnp.logical_and(t < n_trips, alive > 0)

        def body(state):
            t, _ = state
            sweep(first_kb - n_blocks * t, n_blocks)
            return t + 1, live()

        return lax.while_loop(cond, body, (jnp.int32(0), alive))[1]

    n_probe = jnp.minimum(i, SB_PROBE_BLOCKS)
    n_rest = i - n_probe
    alive = sweep_while(jnp.int32(1), i - 1, n_probe, 1)
    alive = sweep_while(alive, n_rest - 1, n_rest // ATT_UNROLL, ATT_UNROLL)
    sweep_while(alive, n_rest % ATT_UNROLL - 1, n_rest % ATT_UNROLL, 1)

    o_ref[...] = acc_ref[...].T.astype(o_ref.dtype)


def _sb_attention(qkv, vt):
    s = qkv.shape[0]
    d = SB_HD
    tq = tk = ATT_TK
    return pl.pallas_call(
        functools.partial(_sb_kernel, tq=tq, tk=tk, scale=d ** -0.5),
        grid=(SB_HEADS, s // tq),
        in_specs=[pl.BlockSpec((tq, d), lambda h, i: (i, OFF_SB_Q // d + h)),
                  pl.BlockSpec((s, d), lambda h, i: (0, OFF_SB_K // d + h)),
                  pl.BlockSpec((None, s // tk, d, tk), lambda h, i: (h, 0, 0, 0))],
        out_specs=pl.BlockSpec((tq, d), lambda h, i: (i, h)),
        out_shape=jax.ShapeDtypeStruct((s, SB_W), BF16),
        scratch_shapes=[pltpu.VMEM((d, tq), F32), pltpu.VMEM((1, tq), F32),
                        pltpu.VMEM((2, ATT_STRIP, ATT_STRIP), F32),
                        pltpu.VMEM((2, ATT_STRIP, ATT_STRIP), F32),
                        pltpu.VMEM((2, ATT_STRIP, ATT_STRIP), F32),
                        pltpu.VMEM((2, 1, ATT_STRIP), F32),
                        pltpu.VMEM((tk, ATT_STRIP), BF16)],
        compiler_params=_cparams(2),
        name="sb_attention",
    )(qkv, qkv, vt)


def _ret_kernel(lg_ref, q_ref, k_ref, v0_ref, v1_ref, g0_ref, g1_ref, gn_g_ref, gn_b_ref, o_ref,
                state_ref, decay_ref, *, tl):
    heads = range(RET_HEADS)
    per_block = RET_HEADS // 2
    r = lax.broadcasted_iota(jnp.int32, (tl, 1), 0).astype(F32)

    def wide(refs, h):
        c0 = (h % per_block) * RET_DV
        return refs[h // per_block][:, c0:c0 + RET_DV]

    @pl.when(pl.program_id(0) == 0)
    def _():
        state_ref[...] = jnp.zeros_like(state_ref)
        n = lax.broadcasted_iota(jnp.int32, (tl, tl), 0)
        m = lax.broadcasted_iota(jnp.int32, (tl, tl), 1)
        dist = jnp.abs(n - m).astype(F32)
        visible = (m >> CHUNK_SHIFT) <= (n >> CHUNK_SHIFT)
        for h in heads:
            decay_ref[h] = jnp.where(visible, jnp.exp(dist * lg_ref[h]), 0.0)

    qs = [q_ref[:, h * RET_DK:(h + 1) * RET_DK] for h in heads]
    ks = [k_ref[:, h * RET_DK:(h + 1) * RET_DK] for h in heads]
    vs = [wide((v0_ref, v1_ref), h) for h in heads]
    scores = [lax.dot_general(qs[h], ks[h], (((1,), (1,)), ((), ())), preferred_element_type=F32)
              for h in heads]
    cross = [jnp.dot(qs[h], state_ref[h].astype(BF16), preferred_element_type=F32) for h in heads]
    pushed = []
    for h in heads:
        zeta = jnp.exp((tl - 1.0 - r) * lg_ref[h])
        kz = (ks[h].astype(F32) * zeta).astype(BF16)
        pushed.append(lax.dot_general(kz, vs[h], (((0,), (0,)), ((), ())),
                                      preferred_element_type=F32))
    intra = [jnp.dot((scores[h] * decay_ref[h]).astype(BF16), vs[h], preferred_element_type=F32)
             for h in heads]
    for h in heads:
        lg = lg_ref[h]
        block_decay = jnp.exp(jnp.full((1, 1), tl, F32) * lg)
        state_ref[h] = state_ref[h] * block_decay + pushed[h]
        y = intra[h] + cross[h] * jnp.exp((r + 1.0) * lg)
        mu = jnp.mean(y, axis=-1, keepdims=True)
        yc = y - mu
        var = jnp.mean(yc * yc, axis=-1, keepdims=True)
        cols = slice(h * RET_DV, (h + 1) * RET_DV)
        yn = yc * lax.rsqrt(var + EPS) * gn_g_ref[:, cols] + gn_b_ref[:, cols]
        gate = wide((g0_ref, g1_ref), h).astype(F32)
        o_ref[:, cols] = (yn * (gate * jax.nn.sigmoid(gate))).astype(o_ref.dtype)


def _retention(qkv, rq, rk, gn_g, gn_b):
    s = qkv.shape[0]
    tl = _tile(s, 256)
    log_gamma = jnp.log(1.0 - 2.0 ** (-5.0 - jnp.arange(RET_HEADS, dtype=F32)))
    half = RET_V // 2
    assert RET_QK == half and OFF_R_V % half == 0 and OFF_R_G % half == 0 and RET_HEADS % 2 == 0
    col = lambda off: pl.BlockSpec((tl, half), lambda b, off=off: (b, off // half))
    row = pl.BlockSpec((1, RET_V), lambda b: (0, 0))
    return pl.pallas_call(
        functools.partial(_ret_kernel, tl=tl),
        grid=(s // tl,),
        in_specs=[pl.BlockSpec(memory_space=pltpu.SMEM), col(0), col(0),
                  col(OFF_R_V), col(OFF_R_V + half), col(OFF_R_G), col(OFF_R_G + half), row, row],
        out_specs=pl.BlockSpec((tl, RET_V), lambda b: (b, 0)),
        out_shape=jax.ShapeDtypeStruct((s, RET_V), BF16),
        scratch_shapes=[pltpu.VMEM((RET_HEADS, RET_DK, RET_DV), F32),
                        pltpu.VMEM((RET_HEADS, tl, tl), F32)],
        compiler_params=_cparams(1),
        name="retention",
    )(log_gamma, rq, rk, qkv, qkv, qkv, qkv, gn_g.reshape(1, RET_V), gn_b.reshape(1, RET_V))


def _diff_kernel(lq1_ref, lk1_ref, lq2_ref, lk2_ref, on_ref, q_ref, k_ref, vt_ref, o_ref,
                 q2_ref, m_ref, l_ref, acc_ref, s0_ref, s1_ref, smax_ref, *, tq, tk, lam_init):
    i = pl.program_id(1)
    st = ATT_STRIP
    assert tq == tk and tq == 2 * st
    q = q_ref[...]
    lane = lax.broadcasted_iota(jnp.int32, q.shape, 1)
    zero = jnp.zeros_like(q)
    q2_ref[0:tq, :] = jnp.where(lane < DIFF_HD, q, zero)
    q2_ref[tq:2 * tq, :] = jnp.where(lane < DIFF_HD, zero, q)
    m_ref[...] = jnp.full_like(m_ref, -jnp.inf)
    l_ref[...] = jnp.zeros_like(l_ref)
    acc_ref[...] = jnp.zeros_like(acc_ref)

    n_strip = 2 * tq // st
    s_bufs = (s0_ref, s1_ref)

    def scores(c, kb, nk):
        ks = pl.multiple_of(kb * tk, tk)
        qc = q2_ref[c * st:(c + 1) * st, :]
        parts = [lax.dot_general(k_ref[pl.ds(ks + r, st), :], qc, (((1,), (1,)), ((), ())),
                                 preferred_element_type=F32) for r in range(0, nk, st)]
        return parts[0] if len(parts) == 1 else jnp.concatenate(parts, axis=0)

    def stash(slot, s):
        s_bufs[slot][...] = s
        smax_ref[slot] = jnp.max(s, axis=0, keepdims=True)

    def softmax_pv(c, kb, nk, slot, diagonal):
        lanes = slice(c * st, (c + 1) * st)
        s = s_bufs[slot][0:nk, :]
        if diagonal:
            kchunk = lax.broadcasted_iota(jnp.int32, (nk, st), 0) >> CHUNK_SHIFT
            qchunk = ((c % 2) * st + lax.broadcasted_iota(jnp.int32, (nk, st), 1)) >> CHUNK_SHIFT
            s = jnp.where(kchunk <= qchunk, s, -jnp.inf)
            s_max = jnp.max(s, axis=0, keepdims=True)
        else:
            s_max = smax_ref[slot]
        m_old = m_ref[:, lanes]
        m_new = jnp.maximum(m_old, s_max)
        alpha = jnp.exp(m_old - m_new)
        p = jnp.exp(s - m_new)
        l_ref[:, lanes] = alpha * l_ref[:, lanes] + jnp.sum(p, axis=0, keepdims=True)
        acc_ref[:, lanes] = alpha * acc_ref[:, lanes] + jnp.dot(
            vt_ref[kb, :, 0:nk], p.astype(BF16), preferred_element_type=F32)
        m_ref[:, lanes] = m_new

    def key_block(kb, diagonal):
        for c in range(n_strip):
            nk = (c % 2 + 1) * st if diagonal else tk
            if c + 1 < n_strip:
                stash((c + 1) % 2, scores(c + 1, kb, tk))
            elif not diagonal:
                stash(0, scores(0, kb + 1, tk))
            softmax_pv(c, kb, nk, c % 2, diagonal)

    assert n_strip % 2 == 0
    stash(0, scores(0, 0, tk))

    def body(t, carry):
        for b in range(ATT_UNROLL):
            key_block(ATT_UNROLL * t + b, False)
        return carry

    def tail_body(kb, carry):
        key_block(kb, False)
        return carry

    n_main = i // ATT_UNROLL
    lax.fori_loop(0, n_main, body, 0)
    lax.fori_loop(n_main * ATT_UNROLL, i, tail_body, 0)
    key_block(i, True)

    lam = (jnp.exp(jnp.sum(lq1_ref[...] * lk1_ref[...], axis=-1, keepdims=True))
           - jnp.exp(jnp.sum(lq2_ref[...] * lk2_ref[...], axis=-1, keepdims=True)) + lam_init)
    o = acc_ref[...] / l_ref[...]
    a = o[:, 0:tq] - lam * o[:, tq:2 * tq]
    a = a * lax.rsqrt(jnp.mean(a * a, axis=0, keepdims=True) + EPS)
    o_ref[...] = (a.T * on_ref[...] * (1.0 - lam_init)).astype(o_ref.dtype)


def _diff_attention(dq, dk, vt, lq1, lk1, lq2, lk2, on, lam_init):
    s = dq.shape[0]
    dv = 2 * DIFF_HD
    tq = tk = ATT_TK
    lam_spec = pl.BlockSpec((1, DIFF_HD), lambda h, i: (0, 0))
    return pl.pallas_call(
        functools.partial(_diff_kernel, tq=tq, tk=tk, lam_init=lam_init),
        grid=(DIFF_HEADS, s // tq),
        in_specs=[lam_spec, lam_spec, lam_spec, lam_spec,
                  pl.BlockSpec((1, dv), lambda h, i: (0, 0)),
                  pl.BlockSpec((tq, dv), lambda h, i: (i, h)),
                  pl.BlockSpec((s, dv), lambda h, i: (0, h)),
                  pl.BlockSpec((None, s // tk, dv, tk), lambda h, i: (h, 0, 0, 0))],
        out_specs=pl.BlockSpec((tq, dv), lambda h, i: (i, h)),
        out_shape=jax.ShapeDtypeStruct((s, DIFF_V), BF16),
        scratch_shapes=[pltpu.VMEM((2 * tq, dv), BF16), pltpu.VMEM((1, 2 * tq), F32),
                        pltpu.VMEM((1, 2 * tq), F32), pltpu.VMEM((dv, 2 * tq), F32),
                        pltpu.VMEM((tk, ATT_STRIP), F32), pltpu.VMEM((tk, ATT_STRIP), F32),
                        pltpu.VMEM((2, 1, ATT_STRIP), F32)],
        compiler_params=_cparams(2),
        name="diff_attention",
    )(lq1.reshape(1, DIFF_HD), lk1.reshape(1, DIFF_HD), lq2.reshape(1, DIFF_HD),
      lk2.reshape(1, DIFF_HD), on.reshape(1, dv), dq, dk, vt)


def _merge_kernel(ysb_ref, yret_ref, ydiff_ref, gsb_ref, gret_ref, gdiff_ref, w_ref, o_ref, wbf_ref):
    @pl.when(pl.program_id(1) == 0)
    def _():
        wbf_ref[...] = w_ref[...].astype(BF16)

    r_sb = jnp.dot(ysb_ref[...], wbf_ref[0:SB_W, :], preferred_element_type=F32)
    r_ret = jnp.dot(yret_ref[...], wbf_ref[SB_W:SB_W + RET_V, :], preferred_element_type=F32)
    r_diff = jnp.dot(ydiff_ref[...], wbf_ref[SB_W + RET_V:SB_W + RET_V + DIFF_V, :],
                     preferred_element_type=F32)
    merged = (gsb_ref[...].astype(F32) * r_sb + gret_ref[...].astype(F32) * r_ret
              + gdiff_ref[...].astype(F32) * r_diff)
    o_ref[...] = merged.astype(o_ref.dtype)


def _merge(y_sb, y_ret, y_diff, gates, w_branch, l):
    s = y_sb.shape[0]
    _, kw, n = w_branch.shape
    tm, tn = _tile(s, 1024), _tile(n, 1024)
    nj = n // tn
    gate = lambda b: pl.BlockSpec((tm, tn), lambda j, i, b=b: (i, b * nj + j))
    return pl.pallas_call(
        _merge_kernel,
        grid=(nj, s // tm),
        in_specs=[pl.BlockSpec((tm, SB_W), lambda j, i: (i, 0)),
                  pl.BlockSpec((tm, RET_V), lambda j, i: (i, 0)),
                  pl.BlockSpec((tm, DIFF_V), lambda j, i: (i, 0)),
                  gate(0), gate(1), gate(2),
                  pl.BlockSpec((None, kw, tn), lambda j, i: (l, 0, j))],
        out_specs=pl.BlockSpec((tm, tn), lambda j, i: (i, j)),
        out_shape=jax.ShapeDtypeStruct((s, n), BF16),
        scratch_shapes=[pltpu.VMEM((kw, tn), BF16)],
        compiler_params=_cparams(2),
        name="merge",
    )(y_sb, y_ret, y_diff, gates, gates, gates, w_branch)


def _cast_kernel(w_ref, o_ref):
    o_ref[...] = w_ref[...].astype(o_ref.dtype)


def _to_bf16(w, l):
    _, k, n = w.shape
    tk = _tile(k, 512)
    return pl.pallas_call(
        _cast_kernel,
        grid=(k // tk,),
        in_specs=[pl.BlockSpec((None, tk, n), lambda i: (l, i, 0))],
        out_specs=pl.BlockSpec((tk, n), lambda i: (i, 0)),
        out_shape=jax.ShapeDtypeStruct((k, n), BF16),
        compiler_params=_cparams(1),
        name="to_bf16",
    )(w)


def _residual_bf16_kernel(a_ref, w_ref, x_ref, gate_ref, o_ref):
    r = jnp.dot(a_ref[...], w_ref[...], preferred_element_type=F32)
    o_ref[...] = x_ref[...] + gate_ref[...] * r


def _residual_proj_deep(a, w, l, x, gate, name):
    m, k = a.shape
    n = w.shape[2]
    tm, tn = _tile(m, 512), _tile(n, 1024)
    return pl.pallas_call(
        _residual_bf16_kernel,
        grid=(n // tn, m // tm),
        in_specs=[pl.BlockSpec((tm, k), lambda j, i: (i, 0)),
                  pl.BlockSpec((k, tn), lambda j, i: (0, j)),
                  pl.BlockSpec((tm, tn), lambda j, i: (i, j)),
                  pl.BlockSpec((1, tn), lambda j, i: (0, j))],
        out_specs=pl.BlockSpec((tm, tn), lambda j, i: (i, j)),
        out_shape=jax.ShapeDtypeStruct((m, n), F32),
        compiler_params=_cparams(2),
        name=name,
    )(a, _to_bf16(w, l), x, gate)


def _residual_norm_kernel(a_ref, w_ref, x_ref, gate_ref, g_ref, scale_ref, shift_ref,
                          xo_ref, h_ref):
    r = jnp.dot(a_ref[...], w_ref[...], preferred_element_type=F32)
    x = x_ref[...] + gate_ref[...] * r
    xo_ref[...] = x
    y = x * lax.rsqrt(jnp.mean(x * x, axis=-1, keepdims=True) + EPS)
    y = y * g_ref[...]
    h_ref[...] = (y * (1.0 + scale_ref[...]) + shift_ref[...]).astype(h_ref.dtype)


def _residual_proj_norm(a, w, l, x, gate, g, scale, shift, name):
    m, k = a.shape
    n = w.shape[2]
    tm = _tile(m, 512)
    row = pl.BlockSpec((1, n), lambda i: (0, 0))
    full = pl.BlockSpec((tm, n), lambda i: (i, 0))
    return pl.pallas_call(
        _residual_norm_kernel,
        grid=(m // tm,),
        in_specs=[pl.BlockSpec((tm, k), lambda i: (i, 0)),
                  pl.BlockSpec((k, n), lambda i: (0, 0)),
                  full, row, row, row, row],
        out_specs=[full, full],
        out_shape=[jax.ShapeDtypeStruct((m, n), F32), jax.ShapeDtypeStruct((m, n), BF16)],
        compiler_params=_cparams(1),
        name=name,
    )(a, _to_bf16(w, l), x, gate, g.reshape(1, n), scale, shift)


def _ffn_up_kernel(a_ref, wg_ref, wv_ref, cwg_ref, cwv_ref, cbg_ref, cbv_ref, o_ref,
                   wg_bf_ref, wv_bf_ref, ug_ref, uv_ref, *, tm):
    i = pl.program_id(1)
    halo = SUBLANES

    @pl.when(i == 0)
    def _():
        wg_bf_ref[...] = wg_ref[...].astype(BF16)
        wv_bf_ref[...] = wv_ref[...].astype(BF16)
        ug_ref[0:halo, :] = jnp.zeros((halo, ug_ref.shape[1]), F32)
        uv_ref[0:halo, :] = jnp.zeros((halo, uv_ref.shape[1]), F32)

    a = a_ref[...]

    def conv(u_ref, wbf_ref, cw_ref, cb_ref):
        u = jnp.dot(a, wbf_ref[...], preferred_element_type=F32)
        u_ref[halo:halo + tm, :] = u
        y = cb_ref[...] + cw_ref[CONV_W - 1:CONV_W, :] * u
        top = cb_ref[...] + cw_ref[CONV_W - 1:CONV_W, :] * u[0:halo]
        for t in range(CONV_W - 1):
            shift = CONV_W - 1 - t
            y = y + cw_ref[t:t + 1, :] * pltpu.roll(u, shift, 0)
            top = top + cw_ref[t:t + 1, :] * u_ref[halo - shift:2 * halo - shift, :]
        u_ref[0:halo, :] = u_ref[tm:tm + halo, :]
        return top, y

    top_g, yg = conv(ug_ref, wg_bf_ref, cwg_ref, cbg_ref)
    top_v, yv = conv(uv_ref, wv_bf_ref, cwv_ref, cbv_ref)
    act = yg * jax.nn.sigmoid(yg) * yv
    act_top = top_g * jax.nn.sigmoid(top_g) * top_v
    o_ref[...] = act.astype(o_ref.dtype)
    o_ref[0:2 * halo, :] = jnp.concatenate([act_top, act[halo:2 * halo]], axis=0).astype(o_ref.dtype)


def _ffn_up(h2, w_up, conv_w, conv_b, l):
    m, k = h2.shape
    f = w_up.shape[2] // 2
    tm, tn = _tile(m, 1024), _tile(f, 512)
    nj = f // tn
    wspec = lambda half: pl.BlockSpec((None, k, tn), lambda j, i, half=half: (l, 0, half * nj + j))
    cwspec = lambda half: pl.BlockSpec((None, CONV_W, tn), lambda j, i, half=half: (l, 0, half * nj + j))
    cbspec = lambda half: pl.BlockSpec((None, 1, tn), lambda j, i, half=half: (l, 0, half * nj + j))
    return pl.pallas_call(
        functools.partial(_ffn_up_kernel, tm=tm),
        grid=(nj, m // tm),
        in_specs=[pl.BlockSpec((tm, k), lambda j, i: (i, 0)),
                  wspec(0), wspec(1), cwspec(0), cwspec(1), cbspec(0), cbspec(1)],
        out_specs=pl.BlockSpec((tm, tn), lambda j, i: (i, j)),
        out_shape=jax.ShapeDtypeStruct((m, f), BF16),
        scratch_shapes=[pltpu.VMEM((k, tn), BF16), pltpu.VMEM((k, tn), BF16),
                        pltpu.VMEM((tm + SUBLANES, tn), F32), pltpu.VMEM((tm + SUBLANES, tn), F32)],
        compiler_params=_cparams(2),
        name="ffn_up",
    )(h2, w_up, w_up, conv_w, conv_w, conv_b[:, None, :], conv_b[:, None, :])


def kernel(x, c, positions, w_ada, b_ada, norm_mix, w_in, w_gate, ret_gn_g, ret_gn_b, diff_qn, diff_kn, diff_on, lam_q1, lam_k1, lam_q2, lam_k2, w_branch, w_out, norm_ffn, w_up, conv_w, conv_b, w_down):
    batch, s, d = x.shape
    depth = w_ada.shape[0]
    x2d = x.reshape(batch * s, d)
    outs = []
    for b in range(batch):
        xb = x2d[b * s:(b + 1) * s]
        pos = positions[b]
        mod = _adaln(c[b:b + 1], w_ada, b_ada)
        tables = _rope_tables(pos)
        for l in range(depth):
            shift1, scale1, gate1, shift2, scale2, gate2 = [
                mod[l, :, t * d:(t + 1) * d] for t in range(6)]
            h = _norm_mod(xb, norm_mix[l], scale1, shift1)
            qkv = _proj(h, w_in, l, name="in_proj")
            gates = _proj(h, w_gate, l, act="sigmoid", name="gate_proj")
            rq, rk, dq, dk, sb_vt, d_vt = _prep(qkv, tables, diff_qn[l], diff_kn[l])
            y_sb = _sb_attention(qkv, sb_vt)
            y_ret = _retention(qkv, rq, rk, ret_gn_g[l], ret_gn_b[l])
            lam_init = 0.8 - 0.6 * float(np.exp(-0.3 * l))
            y_diff = _diff_attention(dq, dk, d_vt, lam_q1[l], lam_k1[l], lam_q2[l], lam_k2[l],
                                     diff_on[l], lam_init)
            merged = _merge(y_sb, y_ret, y_diff, gates, w_branch, l)
            xb, h2 = _residual_proj_norm(merged, w_out, l, xb, gate1, norm_ffn[l], scale2, shift2,
                                         name="out_proj")
            act = _ffn_up(h2, w_up, conv_w, conv_b, l)
            xb = _residual_proj_deep(act, w_down, l, xb, gate2, name="ffn_down")
        outs.append(xb)
    return jnp.concatenate(outs, axis=0).reshape(batch, s, d)
```

```python
import functools

import jax
import jax.numpy as jnp
import numpy as np
from jax import lax
from jax.experimental import pallas as pl
from jax.experimental.pallas import tpu as pltpu

F32 = jnp.float32
BF16 = jnp.bfloat16

CHUNK = 64
CHUNK_SHIFT = CHUNK.bit_length() - 1
assert 1 << CHUNK_SHIFT == CHUNK
ROPE_THETA = 10000.0
EPS = 1e-6
SB_HEADS, SB_HD = 4, 128
RET_HEADS, RET_DK, RET_DV = 4, 128, 256
DIFF_HEADS, DIFF_HD = 4, 64
CONV_W = 3

SB_W = SB_HEADS * SB_HD
RET_QK = RET_HEADS * RET_DK
RET_V = RET_HEADS * RET_DV
DIFF_QK = DIFF_HEADS * 2 * DIFF_HD
DIFF_V = DIFF_HEADS * 2 * DIFF_HD
OFF_SB_Q = 0
OFF_SB_K = OFF_SB_Q + SB_W
OFF_SB_V = OFF_SB_K + SB_W
OFF_R_Q = OFF_SB_V + SB_W
OFF_R_K = OFF_R_Q + RET_QK
OFF_R_V = OFF_R_K + RET_QK
OFF_R_G = OFF_R_V + RET_V
OFF_D_Q = OFF_R_G + RET_V
OFF_D_K = OFF_D_Q + DIFF_QK
OFF_D_V = OFF_D_K + DIFF_QK
D_IN = OFF_D_V + DIFF_V

V7X_VMEM_BYTES = 64 * 1024 * 1024
VMEM_LIMIT = V7X_VMEM_BYTES - 8 * 1024 * 1024
LANES = 128
ATT_TK = 512
ATT_STRIP = 256
ATT_UNROLL = 4
PROJ_ROW_CHUNK = 512
SB_PROBE_BLOCKS = 2
SB_ZERO_LOG_WEIGHT = -104.0
SUBLANES = 8


def _cparams(n_axes, vmem=VMEM_LIMIT):
    return pltpu.CompilerParams(dimension_semantics=("arbitrary",) * n_axes,
                                vmem_limit_bytes=vmem)


def _tile(n, t):
    t = min(n, t)
    assert n % t == 0, (n, t)
    return t


def _adaln_kernel(c_ref, w_ref, b_ref, o_ref):
    c = c_ref[...]
    ca = c * jax.nn.sigmoid(c)
    o_ref[0] = jnp.sum(w_ref[0] * ca, axis=0, keepdims=True) + b_ref[0]


def _adaln(c, w_ada, b_ada):
    depth, d, n = w_ada.shape
    tn = _tile(n, 1024)
    return pl.pallas_call(
        _adaln_kernel,
        grid=(depth, n // tn),
        in_specs=[pl.BlockSpec((d, 1), lambda l, j: (0, 0)),
                  pl.BlockSpec((1, d, tn), lambda l, j: (l, 0, j)),
                  pl.BlockSpec((1, 1, tn), lambda l, j: (l, 0, j))],
        out_specs=pl.BlockSpec((1, 1, tn), lambda l, j: (l, 0, j)),
        out_shape=jax.ShapeDtypeStruct((depth, 1, n), F32),
        compiler_params=_cparams(2),
        name="adaln",
    )(c.reshape(d, 1), w_ada, b_ada.reshape(depth, 1, n))


def _norm_mod_kernel(x_ref, g_ref, scale_ref, shift_ref, o_ref):
    x = x_ref[...]
    y = x * lax.rsqrt(jnp.mean(x * x, axis=-1, keepdims=True) + EPS)
    y = y * g_ref[...]
    o_ref[...] = (y * (1.0 + scale_ref[...]) + shift_ref[...]).astype(o_ref.dtype)


def _norm_mod(x, g, scale, shift):
    s, d = x.shape
    tm = _tile(s, 1024)
    row = pl.BlockSpec((1, d), lambda i: (0, 0))
    return pl.pallas_call(
        _norm_mod_kernel,
        grid=(s // tm,),
        in_specs=[pl.BlockSpec((tm, d), lambda i: (i, 0)), row, row, row],
        out_specs=pl.BlockSpec((tm, d), lambda i: (i, 0)),
        out_shape=jax.ShapeDtypeStruct((s, d), BF16),
        compiler_params=_cparams(1),
        name="norm_mod",
    )(x, g.reshape(1, d), scale, shift)


def _proj_kernel(a_ref, w_ref, o_ref, wbf_ref, *, act):
    @pl.when(pl.program_id(1) == 0)
    def _():
        wbf_ref[...] = w_ref[...].astype(BF16)

    rows = min(a_ref.shape[0], PROJ_ROW_CHUNK)
    for r0 in range(0, a_ref.shape[0], rows):
        r = jnp.dot(a_ref[r0:r0 + rows, :], wbf_ref[...], preferred_element_type=F32)
        if act == "sigmoid":
            r = jax.nn.sigmoid(r)
        o_ref[r0:r0 + rows, :] = r.astype(o_ref.dtype)


def _proj(a, w, l, act=None, name="proj"):
    m, k = a.shape
    n = w.shape[2]
    tm, tn = _tile(m, 2048), _tile(n, 1024)
    return pl.pallas_call(
        functools.partial(_proj_kernel, act=act),
        grid=(n // tn, m // tm),
        in_specs=[pl.BlockSpec((tm, k), lambda j, i: (i, 0)),
                  pl.BlockSpec((None, k, tn), lambda j, i: (l, 0, j))],
        out_specs=pl.BlockSpec((tm, tn), lambda j, i: (i, j)),
        out_shape=jax.ShapeDtypeStruct((m, n), BF16),
        scratch_shapes=[pltpu.VMEM((k, tn), BF16)],
        compiler_params=_cparams(2),
        name=name,
    )(a, w)


def _rope_tables_kernel(pos_ref, inv_ref, cos_r_ref, sin_r_ref, cos_d_ref, sin_d_ref):
    pos = pos_ref[...].astype(F32)
    lane = lax.broadcasted_iota(jnp.int32, (pos.shape[0], LANES), 1)
    ang = pos * inv_ref[...]
    c, s = jnp.cos(ang), jnp.sin(ang)
    half_r, half_d = RET_DK // 2, DIFF_HD // 2

    def ret_table(t):
        return jnp.where(lane < half_r, t, pltpu.roll(t, half_r, 1))

    def diff_table(t):
        d = pltpu.roll(t, LANES - half_r, 1)
        e = jnp.where(lane < half_d, d, pltpu.roll(d, half_d, 1))
        return jnp.where(lane < 2 * half_d, e, pltpu.roll(e, 2 * half_d, 1))

    cos_r_ref[...] = ret_table(c)
    sin_r_ref[...] = jnp.where(lane < half_r, -1.0, 1.0) * ret_table(s)
    cos_d_ref[...] = diff_table(c)
    sin_d_ref[...] = jnp.where((lane & (DIFF_HD - 1)) < half_d, -1.0, 1.0) * diff_table(s)


def _rope_tables(positions):
    s = positions.shape[0]
    tm = _tile(s, 512)
    assert RET_DK == LANES and 2 * DIFF_HD == LANES
    inv_r = ROPE_THETA ** (-jnp.arange(0, RET_DK, 2, dtype=F32) / RET_DK)
    inv_d = ROPE_THETA ** (-jnp.arange(0, DIFF_HD, 2, dtype=F32) / DIFF_HD)
    pad = jnp.zeros((LANES - RET_DK // 2 - DIFF_HD // 2,), F32)
    inv = jnp.concatenate([inv_r, inv_d, pad]).reshape(1, LANES)
    table = pl.BlockSpec((tm, LANES), lambda i: (i, 0))
    return pl.pallas_call(
        _rope_tables_kernel,
        grid=(s // tm,),
        in_specs=[pl.BlockSpec((tm, 1), lambda i: (i, 0)), pl.BlockSpec((1, LANES), lambda i: (0, 0))],
        out_specs=[table] * 4,
        out_shape=[jax.ShapeDtypeStruct((s, LANES), F32)] * 4,
        compiler_params=_cparams(1),
        name="rope_tables",
    )(positions.reshape(s, 1), inv)


def _prep_kernel(cos_r_ref, sin_r_ref, cos_d_ref, sin_d_ref, qn_ref, kn_ref,
                 rq_ref, rk_ref, dq_ref, dk_ref, sbv_ref, dv_ref,
                 orq_ref, ork_ref, odq_ref, odk_ref, osbvt_ref, odvt_ref):
    tm = rq_ref.shape[0]
    lane = lax.broadcasted_iota(jnp.int32, (tm, LANES), 1)

    for v_ref, ovt_ref in ((sbv_ref, osbvt_ref), (dv_ref, odvt_ref)):
        for h in range(ovt_ref.shape[0]):
            vt = v_ref[:, h * LANES:(h + 1) * LANES].astype(F32).T
            for cb in range(tm // ATT_TK):
                ovt_ref[h, cb] = vt[:, cb * ATT_TK:(cb + 1) * ATT_TK].astype(ovt_ref.dtype)

    cos_r, sin_r = cos_r_ref[...], sin_r_ref[...]
    k_scale = RET_DK ** -0.5
    for h in range(RET_HEADS):
        sl = slice(h * RET_DK, (h + 1) * RET_DK)
        xq = rq_ref[:, sl].astype(F32)
        xk = rk_ref[:, sl].astype(F32)
        orq_ref[:, sl] = (xq * cos_r + pltpu.roll(xq, RET_DK // 2, 1) * sin_r).astype(orq_ref.dtype)
        ork_ref[:, sl] = ((xk * cos_r + pltpu.roll(xk, RET_DK // 2, 1) * sin_r) * k_scale).astype(ork_ref.dtype)

    first_half = (lane & (DIFF_HD - 1)) < DIFF_HD // 2
    cos_d, sin_d = cos_d_ref[...], sin_d_ref[...]
    low_map = lane < DIFF_HD
    q_scale = DIFF_HD ** -0.5

    def qk_norm_rope(x, gain):
        x2 = x * x
        ms_lo = jnp.sum(jnp.where(low_map, x2, 0.0), axis=-1, keepdims=True) * (1.0 / DIFF_HD)
        ms_hi = jnp.sum(jnp.where(low_map, 0.0, x2), axis=-1, keepdims=True) * (1.0 / DIFF_HD)
        r = jnp.where(low_map, lax.rsqrt(ms_lo + EPS), lax.rsqrt(ms_hi + EPS))
        y = x * r * gain
        partner = jnp.where(first_half, pltpu.roll(y, LANES - DIFF_HD // 2, 1),
                            pltpu.roll(y, DIFF_HD // 2, 1))
        return y * cos_d + partner * sin_d

    for h in range(DIFF_HEADS):
        sl = slice(h * 2 * DIFF_HD, (h + 1) * 2 * DIFF_HD)
        odq_ref[:, sl] = (qk_norm_rope(dq_ref[:, sl].astype(F32), qn_ref[...]) * q_scale).astype(odq_ref.dtype)
        odk_ref[:, sl] = qk_norm_rope(dk_ref[:, sl].astype(F32), kn_ref[...]).astype(odk_ref.dtype)


def _prep(qkv, tables, qn, kn):
    s = qkv.shape[0]
    tm = _tile(s, 512)
    w = 512
    assert RET_QK == w and DIFF_QK == w
    col = lambda off: pl.BlockSpec((tm, w), lambda i, off=off: (i, off // w))
    row = pl.BlockSpec((1, LANES), lambda i: (0, 0))
    table = pl.BlockSpec((tm, LANES), lambda i: (i, 0))
    out_spec = pl.BlockSpec((tm, w), lambda i: (i, 0))
    out = jax.ShapeDtypeStruct((s, w), BF16)
    assert SB_HD == LANES and 2 * DIFF_HD == LANES and SB_W == w and DIFF_V == w
    nkb = tm // ATT_TK
    vt_spec = pl.BlockSpec((SB_HEADS, nkb, LANES, ATT_TK), lambda i: (0, i, 0, 0))
    vt_out = jax.ShapeDtypeStruct((SB_HEADS, s // ATT_TK, LANES, ATT_TK), BF16)
    return pl.pallas_call(
        _prep_kernel,
        grid=(s // tm,),
        in_specs=[table, table, table, table, row, row,
                  col(OFF_R_Q), col(OFF_R_K), col(OFF_D_Q), col(OFF_D_K),
                  col(OFF_SB_V), col(OFF_D_V)],
        out_specs=[out_spec] * 4 + [vt_spec] * 2,
        out_shape=[out] * 4 + [vt_out] * 2,
        compiler_params=_cparams(1),
        name="rope_prep",
    )(*tables, jnp.tile(qn, 2).reshape(1, LANES), jnp.tile(kn, 2).reshape(1, LANES),
      qkv, qkv, qkv, qkv, qkv, qkv)


def _sb_kernel(q_ref, k_ref, vt_ref, o_ref, acc_ref, csum_ref, z_ref, lb_ref, later_ref, lf0_ref,
               w_ref, *, tq, tk, scale):
    i = pl.program_id(1)
    st = ATT_STRIP
    n_strip, n_half = tq // st, tk // st
    assert tq == tk
    acc_ref[...] = jnp.zeros_like(acc_ref)
    csum_ref[...] = jnp.zeros_like(csum_ref)
    row = lax.broadcasted_iota(jnp.int32, (st, st), 0)
    col = lax.broadcasted_iota(jnp.int32, (st, st), 1)
    tri = jnp.where(col > row, 1.0, 0.0).astype(BF16)

    def scores(c, kb, hf):
        ks = pl.multiple_of(kb * tk, tk)
        k = k_ref[pl.ds(ks + hf * st, st), :]
        return lax.dot_general(k, q_ref[c * st:(c + 1) * st, :], (((1,), (1,)), ((), ())),
                               preferred_element_type=F32)

    def log_probs(raw, masked):
        z = raw * scale
        sp = jnp.log(1.0 + jnp.exp(-jnp.abs(z)))
        log_beta = jnp.minimum(z, 0.0) - sp
        log_fail = log_beta - z
        if masked:
            log_fail = jnp.where(row < col, log_fail, 0.0)
        return log_beta, log_fail.astype(BF16), log_fail[0:1, :]

    def later_sum(lf_bf):
        return jnp.dot(tri, lf_bf, preferred_element_type=F32)

    def weights(c, hf, log_beta, later, lf0, masked):
        lanes = slice(c * st, (c + 1) * st)
        csum = csum_ref[:, lanes]
        w = jnp.exp(log_beta + later + csum)
        if masked:
            w = jnp.where(row < col, w, 0.0)
        w_ref[hf * st:(hf + 1) * st, :] = w.astype(BF16)
        csum_ref[:, lanes] = csum + later[0:1, :] + lf0

    def accumulate(c, kb, n_live):
        lanes = slice(c * st, (c + 1) * st)
        acc_ref[:, lanes] += jnp.dot(vt_ref[kb, :, 0:n_live * st], w_ref[0:n_live * st, :],
                                     preferred_element_type=F32)

    diag_steps = [(c, hf, hf == c, c + 1 if hf == 0 else None)
                  for c in range(n_strip) for hf in reversed(range(c + 1))]
    block_steps = [(c, hf, False, n_half if hf == 0 else None)
                   for c in range(n_strip) for hf in reversed(range(n_half))]
    assert len(block_steps) % 2 == 0

    def stage23(p, raw, masked):
        log_beta, lf_bf, lf0 = log_probs(raw, masked)
        lb_ref[p], lf0_ref[p] = log_beta, lf0
        later_ref[p] = later_sum(lf_bf)

    def run(section, parity):
        for n in range(len(section) - 2):
            p = (parity + n) % 2
            ((c, hf, masked, n_live), kb), ((_, _, masked1, _), _), ((c2, hf2, _, _), kb2) = section[n:n + 3]
            z_ref[p] = scores(c2, kb2, hf2)
            stage23(1 - p, z_ref[1 - p], masked1)
            weights(c, hf, lb_ref[p], later_ref[p], lf0_ref[p], masked)
            if n_live:
                accumulate(c, kb, n_live)

    def at(steps, kb):
        return [(step, kb) for step in steps]

    (c0, hf0, masked0, _), (c1, hf1, _, _) = diag_steps[:2]
    stage23(0, scores(c0, i, hf0), masked0)
    z_ref[1] = scores(c1, i, hf1)
    run(at(diag_steps, i) + at(block_steps[:2], jnp.maximum(i - 1, 0)), 0)

    def sweep(first_kb, n_blocks):
        section = []
        for b in range(n_blocks):
            section += at(block_steps, first_kb - b)
        run(section + at(block_steps[:2], jnp.maximum(first_kb - n_blocks, 0)), len(diag_steps))

    def live():
        return (jnp.max(csum_ref[...]) >= SB_ZERO_LOG_WEIGHT).astype(jnp.int32)

    def sweep_while(alive, first_kb, n_trips, n_blocks):
        def cond(state):
            t, alive = state
            return jnp.logical_and(t < n_trips, alive > 0)

        def body(state):
            t, _ = state
            sweep(first_kb - n_blocks * t, n_blocks)
            return t + 1, live()

        return lax.while_loop(cond, body, (jnp.int32(0), alive))[1]

    n_probe = jnp.minimum(i, SB_PROBE_BLOCKS)
    n_rest = i - n_probe
    alive = sweep_while(jnp.int32(1), i - 1, n_probe, 1)
    alive = sweep_while(alive, n_rest - 1, n_rest // ATT_UNROLL, ATT_UNROLL)
    sweep_while(alive, n_rest % ATT_UNROLL - 1, n_rest % ATT_UNROLL, 1)

    o_ref[...] = acc_ref[...].T.astype(o_ref.dtype)


def _sb_attention(qkv, vt):
    s = qkv.shape[0]
    d = SB_HD
    tq = tk = ATT_TK
    return pl.pallas_call(
        functools.partial(_sb_kernel, tq=tq, tk=tk, scale=d ** -0.5),
        grid=(SB_HEADS, s // tq),
        in_specs=[pl.BlockSpec((tq, d), lambda h, i: (i, OFF_SB_Q // d + h)),
                  pl.BlockSpec((s, d), lambda h, i: (0, OFF_SB_K // d + h)),
                  pl.BlockSpec((None, s // tk, d, tk), lambda h, i: (h, 0, 0, 0))],
        out_specs=pl.BlockSpec((tq, d), lambda h, i: (i, h)),
        out_shape=jax.ShapeDtypeStruct((s, SB_W), BF16),
        scratch_shapes=[pltpu.VMEM((d, tq), F32), pltpu.VMEM((1, tq), F32),
                        pltpu.VMEM((2, ATT_STRIP, ATT_STRIP), F32),
                        pltpu.VMEM((2, ATT_STRIP, ATT_STRIP), F32),
                        pltpu.VMEM((2, ATT_STRIP, ATT_STRIP), F32),
                        pltpu.VMEM((2, 1, ATT_STRIP), F32),
                        pltpu.VMEM((tk, ATT_STRIP), BF16)],
        compiler_params=_cparams(2),
        name="sb_attention",
    )(qkv, qkv, vt)


def _ret_kernel(lg_ref, q_ref, k_ref, v0_ref, v1_ref, g0_ref, g1_ref, gn_g_ref, gn_b_ref, o_ref,
                state_ref, decay_ref, *, tl):
    heads = range(RET_HEADS)
    per_block = RET_HEADS // 2
    r = lax.broadcasted_iota(jnp.int32, (tl, 1), 0).astype(F32)

    def wide(refs, h):
        c0 = (h % per_block) * RET_DV
        return refs[h // per_block][:, c0:c0 + RET_DV]

    @pl.when(pl.program_id(0) == 0)
    def _():
        state_ref[...] = jnp.zeros_like(state_ref)
        n = lax.broadcasted_iota(jnp.int32, (tl, tl), 0)
        m = lax.broadcasted_iota(jnp.int32, (tl, tl), 1)
        dist = jnp.abs(n - m).astype(F32)
        visible = (m >> CHUNK_SHIFT) <= (n >> CHUNK_SHIFT)
        for h in heads:
            decay_ref[h] = jnp.where(visible, jnp.exp(dist * lg_ref[h]), 0.0)

    qs = [q_ref[:, h * RET_DK:(h + 1) * RET_DK] for h in heads]
    ks = [k_ref[:, h * RET_DK:(h + 1) * RET_DK] for h in heads]
    vs = [wide((v0_ref, v1_ref), h) for h in heads]
    scores = [lax.dot_general(qs[h], ks[h], (((1,), (1,)), ((), ())), preferred_element_type=F32)
              for h in heads]
    cross = [jnp.dot(qs[h], state_ref[h].astype(BF16), preferred_element_type=F32) for h in heads]
    pushed = []
    for h in heads:
        zeta = jnp.exp((tl - 1.0 - r) * lg_ref[h])
        kz = (ks[h].astype(F32) * zeta).astype(BF16)
        pushed.append(lax.dot_general(kz, vs[h], (((0,), (0,)), ((), ())),
                                      preferred_element_type=F32))
    intra = [jnp.dot((scores[h] * decay_ref[h]).astype(BF16), vs[h], preferred_element_type=F32)
             for h in heads]
    for h in heads:
        lg = lg_ref[h]
        block_decay = jnp.exp(jnp.full((1, 1), tl, F32) * lg)
        state_ref[h] = state_ref[h] * block_decay + pushed[h]
        y = intra[h] + cross[h] * jnp.exp((r + 1.0) * lg)
        mu = jnp.mean(y, axis=-1, keepdims=True)
        yc = y - mu
        var = jnp.mean(yc * yc, axis=-1, keepdims=True)
        cols = slice(h * RET_DV, (h + 1) * RET_DV)
        yn = yc * lax.rsqrt(var + EPS) * gn_g_ref[:, cols] + gn_b_ref[:, cols]
        gate = wide((g0_ref, g1_ref), h).astype(F32)
        o_ref[:, cols] = (yn * (gate * jax.nn.sigmoid(gate))).astype(o_ref.dtype)


def _retention(qkv, rq, rk, gn_g, gn_b):
    s = qkv.shape[0]
    tl = _tile(s, 256)
    log_gamma = jnp.log(1.0 - 2.0 ** (-5.0 - jnp.arange(RET_HEADS, dtype=F32)))
    half = RET_V // 2
    assert RET_QK == half and OFF_R_V % half == 0 and OFF_R_G % half == 0 and RET_HEADS % 2 == 0
    col = lambda off: pl.BlockSpec((tl, half), lambda b, off=off: (b, off // half))
    row = pl.BlockSpec((1, RET_V), lambda b: (0, 0))
    return pl.pallas_call(
        functools.partial(_ret_kernel, tl=tl),
        grid=(s // tl,),
        in_specs=[pl.BlockSpec(memory_space=pltpu.SMEM), col(0), col(0),
                  col(OFF_R_V), col(OFF_R_V + half), col(OFF_R_G), col(OFF_R_G + half), row, row],
        out_specs=pl.BlockSpec((tl, RET_V), lambda b: (b, 0)),
        out_shape=jax.ShapeDtypeStruct((s, RET_V), BF16),
        scratch_shapes=[pltpu.VMEM((RET_HEADS, RET_DK, RET_DV), F32),
                        pltpu.VMEM((RET_HEADS, tl, tl), F32)],
        compiler_params=_cparams(1),
        name="retention",
    )(log_gamma, rq, rk, qkv, qkv, qkv, qkv, gn_g.reshape(1, RET_V), gn_b.reshape(1, RET_V))


def _diff_kernel(lq1_ref, lk1_ref, lq2_ref, lk2_ref, on_ref, q_ref, k_ref, vt_ref, o_ref,
                 q2_ref, m_ref, l_ref, acc_ref, s0_ref, s1_ref, smax_ref, *, tq, tk, lam_init):
    i = pl.program_id(1)
    st = ATT_STRIP
    assert tq == tk and tq == 2 * st
    q = q_ref[...]
    lane = lax.broadcasted_iota(jnp.int32, q.shape, 1)
    zero = jnp.zeros_like(q)
    q2_ref[0:tq, :] = jnp.where(lane < DIFF_HD, q, zero)
    q2_ref[tq:2 * tq, :] = jnp.where(lane < DIFF_HD, zero, q)
    m_ref[...] = jnp.full_like(m_ref, -jnp.inf)
    l_ref[...] = jnp.zeros_like(l_ref)
    acc_ref[...] = jnp.zeros_like(acc_ref)

    n_strip = 2 * tq // st
    s_bufs = (s0_ref, s1_ref)

    def scores(c, kb, nk):
        ks = pl.multiple_of(kb * tk, tk)
        qc = q2_ref[c * st:(c + 1) * st, :]
        parts = [lax.dot_general(k_ref[pl.ds(ks + r, st), :], qc, (((1,), (1,)), ((), ())),
                                 preferred_element_type=F32) for r in range(0, nk, st)]
        return parts[0] if len(parts) == 1 else jnp.concatenate(parts, axis=0)

    def stash(slot, s):
        s_bufs[slot][...] = s
        smax_ref[slot] = jnp.max(s, axis=0, keepdims=True)

    def softmax_pv(c, kb, nk, slot, diagonal):
        lanes = slice(c * st, (c + 1) * st)
        s = s_bufs[slot][0:nk, :]
        if diagonal:
            kchunk = lax.broadcasted_iota(jnp.int32, (nk, st), 0) >> CHUNK_SHIFT
            qchunk = ((c % 2) * st + lax.broadcasted_iota(jnp.int32, (nk, st), 1)) >> CHUNK_SHIFT
            s = jnp.where(kchunk <= qchunk, s, -jnp.inf)
            s_max = jnp.max(s, axis=0, keepdims=True)
        else:
            s_max = smax_ref[slot]
        m_old = m_ref[:, lanes]
        m_new = jnp.maximum(m_old, s_max)
        alpha = jnp.exp(m_old - m_new)
        p = jnp.exp(s - m_new)
        l_ref[:, lanes] = alpha * l_ref[:, lanes] + jnp.sum(p, axis=0, keepdims=True)
        acc_ref[:, lanes] = alpha * acc_ref[:, lanes] + jnp.dot(
            vt_ref[kb, :, 0:nk], p.astype(BF16), preferred_element_type=F32)
        m_ref[:, lanes] = m_new

    def key_block(kb, diagonal):
        for c in range(n_strip):
            nk = (c % 2 + 1) * st if diagonal else tk
            if c + 1 < n_strip:
                stash((c + 1) % 2, scores(c + 1, kb, tk))
            elif not diagonal:
                stash(0, scores(0, kb + 1, tk))
            softmax_pv(c, kb, nk, c % 2, diagonal)

    assert n_strip % 2 == 0
    stash(0, scores(0, 0, tk))

    def body(t, carry):
        for b in range(ATT_UNROLL):
            key_block(ATT_UNROLL * t + b, False)
        return carry

    def tail_body(kb, carry):
        key_block(kb, False)
        return carry

    n_main = i // ATT_UNROLL
    lax.fori_loop(0, n_main, body, 0)
    lax.fori_loop(n_main * ATT_UNROLL, i, tail_body, 0)
    key_block(i, True)

    lam = (jnp.exp(jnp.sum(lq1_ref[...] * lk1_ref[...], axis=-1, keepdims=True))
           - jnp.exp(jnp.sum(lq2_ref[...] * lk2_ref[...], axis=-1, keepdims=True)) + lam_init)
    o = acc_ref[...] / l_ref[...]
    a = o[:, 0:tq] - lam * o[:, tq:2 * tq]
    a = a * lax.rsqrt(jnp.mean(a * a, axis=0, keepdims=True) + EPS)
    o_ref[...] = (a.T * on_ref[...] * (1.0 - lam_init)).astype(o_ref.dtype)


def _diff_attention(dq, dk, vt, lq1, lk1, lq2, lk2, on, lam_init):
    s = dq.shape[0]
    dv = 2 * DIFF_HD
    tq = tk = ATT_TK
    lam_spec = pl.BlockSpec((1, DIFF_HD), lambda h, i: (0, 0))
    return pl.pallas_call(
        functools.partial(_diff_kernel, tq=tq, tk=tk, lam_init=lam_init),
        grid=(DIFF_HEADS, s // tq),
        in_specs=[lam_spec, lam_spec, lam_spec, lam_spec,
                  pl.BlockSpec((1, dv), lambda h, i: (0, 0)),
                  pl.BlockSpec((tq, dv), lambda h, i: (i, h)),
                  pl.BlockSpec((s, dv), lambda h, i: (0, h)),
                  pl.BlockSpec((None, s // tk, dv, tk), lambda h, i: (h, 0, 0, 0))],
        out_specs=pl.BlockSpec((tq, dv), lambda h, i: (i, h)),
        out_shape=jax.ShapeDtypeStruct((s, DIFF_V), BF16),
        scratch_shapes=[pltpu.VMEM((2 * tq, dv), BF16), pltpu.VMEM((1, 2 * tq), F32),
                        pltpu.VMEM((1, 2 * tq), F32), pltpu.VMEM((dv, 2 * tq), F32),
                        pltpu.VMEM((tk, ATT_STRIP), F32), pltpu.VMEM((tk, ATT_STRIP), F32),
                        pltpu.VMEM((2, 1, ATT_STRIP), F32)],
        compiler_params=_cparams(2),
        name="diff_attention",
    )(lq1.reshape(1, DIFF_HD), lk1.reshape(1, DIFF_HD), lq2.reshape(1, DIFF_HD),
      lk2.reshape(1, DIFF_HD), on.reshape(1, dv), dq, dk, vt)


def _merge_kernel(ysb_ref, yret_ref, ydiff_ref, gsb_ref, gret_ref, gdiff_ref, w_ref, o_ref, wbf_ref):
    @pl.when(pl.program_id(1) == 0)
    def _():
        wbf_ref[...] = w_ref[...].astype(BF16)

    r_sb = jnp.dot(ysb_ref[...], wbf_ref[0:SB_W, :], preferred_element_type=F32)
    r_ret = jnp.dot(yret_ref[...], wbf_ref[SB_W:SB_W + RET_V, :], preferred_element_type=F32)
    r_diff = jnp.dot(ydiff_ref[...], wbf_ref[SB_W + RET_V:SB_W + RET_V + DIFF_V, :],
                     preferred_element_type=F32)
    merged = (gsb_ref[...].astype(F32) * r_sb + gret_ref[...].astype(F32) * r_ret
              + gdiff_ref[...].astype(F32) * r_diff)
    o_ref[...] = merged.astype(o_ref.dtype)


def _merge(y_sb, y_ret, y_diff, gates, w_branch, l):
    s = y_sb.shape[0]
    _, kw, n = w_branch.shape
    tm, tn = _tile(s, 1024), _tile(n, 1024)
    nj = n // tn
    gate = lambda b: pl.BlockSpec((tm, tn), lambda j, i, b=b: (i, b * nj + j))
    return pl.pallas_call(
        _merge_kernel,
        grid=(nj, s // tm),
        in_specs=[pl.BlockSpec((tm, SB_W), lambda j, i: (i, 0)),
                  pl.BlockSpec((tm, RET_V), lambda j, i: (i, 0)),
                  pl.BlockSpec((tm, DIFF_V), lambda j, i: (i, 0)),
                  gate(0), gate(1), gate(2),
                  pl.BlockSpec((None, kw, tn), lambda j, i: (l, 0, j))],
        out_specs=pl.BlockSpec((tm, tn), lambda j, i: (i, j)),
        out_shape=jax.ShapeDtypeStruct((s, n), BF16),
        scratch_shapes=[pltpu.VMEM((kw, tn), BF16)],
        compiler_params=_cparams(2),
        name="merge",
    )(y_sb, y_ret, y_diff, gates, gates, gates, w_branch)


def _cast_kernel(w_ref, o_ref):
    o_ref[...] = w_ref[...].astype(o_ref.dtype)


def _to_bf16(w, l):
    _, k, n = w.shape
    tk = _tile(k, 512)
    return pl.pallas_call(
        _cast_kernel,
        grid=(k // tk,),
        in_specs=[pl.BlockSpec((None, tk, n), lambda i: (l, i, 0))],
        out_specs=pl.BlockSpec((tk, n), lambda i: (i, 0)),
        out_shape=jax.ShapeDtypeStruct((k, n), BF16),
        compiler_params=_cparams(1),
        name="to_bf16",
    )(w)


def _residual_bf16_kernel(a_ref, w_ref, x_ref, gate_ref, o_ref):
    r = jnp.dot(a_ref[...], w_ref[...], preferred_element_type=F32)
    o_ref[...] = x_ref[...] + gate_ref[...] * r


def _residual_proj_deep(a, w, l, x, gate, name):
    m, k = a.shape
    n = w.shape[2]
    tm, tn = _tile(m, 512), _tile(n, 1024)
    return pl.pallas_call(
        _residual_bf16_kernel,
        grid=(n // tn, m // tm),
        in_specs=[pl.BlockSpec((tm, k), lambda j, i: (i, 0)),
                  pl.BlockSpec((k, tn), lambda j, i: (0, j)),
                  pl.BlockSpec((tm, tn), lambda j, i: (i, j)),
                  pl.BlockSpec((1, tn), lambda j, i: (0, j))],
        out_specs=pl.BlockSpec((tm, tn), lambda j, i: (i, j)),
        out_shape=jax.ShapeDtypeStruct((m, n), F32),
        compiler_params=_cparams(2),
        name=name,
    )(a, _to_bf16(w, l), x, gate)


def _residual_norm_kernel(a_ref, w_ref, x_ref, gate_ref, g_ref, scale_ref, shift_ref,
                          xo_ref, h_ref):
    r = jnp.dot(a_ref[...], w_ref[...], preferred_element_type=F32)
    x = x_ref[...] + gate_ref[...] * r
    xo_ref[...] = x
    y = x * lax.rsqrt(jnp.mean(x * x, axis=-1, keepdims=True) + EPS)
    y = y * g_ref[...]
    h_ref[...] = (y * (1.0 + scale_ref[...]) + shift_ref[...]).astype(h_ref.dtype)


def _residual_proj_norm(a, w, l, x, gate, g, scale, shift, name):
    m, k = a.shape
    n = w.shape[2]
    tm = _tile(m, 512)
    row = pl.BlockSpec((1, n), lambda i: (0, 0))
    full = pl.BlockSpec((tm, n), lambda i: (i, 0))
    return pl.pallas_call(
        _residual_norm_kernel,
        grid=(m // tm,),
        in_specs=[pl.BlockSpec((tm, k), lambda i: (i, 0)),
                  pl.BlockSpec((k, n), lambda i: (0, 0)),
                  full, row, row, row, row],
        out_specs=[full, full],
        out_shape=[jax.ShapeDtypeStruct((m, n), F32), jax.ShapeDtypeStruct((m, n), BF16)],
        compiler_params=_cparams(1),
        name=name,
    )(a, _to_bf16(w, l), x, gate, g.reshape(1, n), scale, shift)


def _ffn_up_kernel(a_ref, wg_ref, wv_ref, cwg_ref, cwv_ref, cbg_ref, cbv_ref, o_ref,
                   wg_bf_ref, wv_bf_ref, ug_ref, uv_ref, *, tm):
    i = pl.program_id(1)
    halo = SUBLANES

    @pl.when(i == 0)
    def _():
        wg_bf_ref[...] = wg_ref[...].astype(BF16)
        wv_bf_ref[...] = wv_ref[...].astype(BF16)
        ug_ref[0:halo, :] = jnp.zeros((halo, ug_ref.shape[1]), F32)
        uv_ref[0:halo, :] = jnp.zeros((halo, uv_ref.shape[1]), F32)

    a = a_ref[...]

    def conv(u_ref, wbf_ref, cw_ref, cb_ref):
        u = jnp.dot(a, wbf_ref[...], preferred_element_type=F32)
        u_ref[halo:halo + tm, :] = u
        y = cb_ref[...] + cw_ref[CONV_W - 1:CONV_W, :] * u
        top = cb_ref[...] + cw_ref[CONV_W - 1:CONV_W, :] * u[0:halo]
        for t in range(CONV_W - 1):
            shift = CONV_W - 1 - t
            y = y + cw_ref[t:t + 1, :] * pltpu.roll(u, shift, 0)
            top = top + cw_ref[t:t + 1, :] * u_ref[halo - shift:2 * halo - shift, :]
        u_ref[0:halo, :] = u_ref[tm:tm + halo, :]
        return top, y

    top_g, yg = conv(ug_ref, wg_bf_ref, cwg_ref, cbg_ref)
    top_v, yv = conv(uv_ref, wv_bf_ref, cwv_ref, cbv_ref)
    act = yg * jax.nn.sigmoid(yg) * yv
    act_top = top_g * jax.nn.sigmoid(top_g) * top_v
    o_ref[...] = act.astype(o_ref.dtype)
    o_ref[0:2 * halo, :] = jnp.concatenate([act_top, act[halo:2 * halo]], axis=0).astype(o_ref.dtype)


def _ffn_up(h2, w_up, conv_w, conv_b, l):
    m, k = h2.shape
    f = w_up.shape[2] // 2
    tm, tn = _tile(m, 1024), _tile(f, 512)
    nj = f // tn
    wspec = lambda half: pl.BlockSpec((None, k, tn), lambda j, i, half=half: (l, 0, half * nj + j))
    cwspec = lambda half: pl.BlockSpec((None, CONV_W, tn), lambda j, i, half=half: (l, 0, half * nj + j))
    cbspec = lambda half: pl.BlockSpec((None, 1, tn), lambda j, i, half=half: (l, 0, half * nj + j))
    return pl.pallas_call(
        functools.partial(_ffn_up_kernel, tm=tm),
        grid=(nj, m // tm),
        in_specs=[pl.BlockSpec((tm, k), lambda j, i: (i, 0)),
                  wspec(0), wspec(1), cwspec(0), cwspec(1), cbspec(0), cbspec(1)],
        out_specs=pl.BlockSpec((tm, tn), lambda j, i: (i, j)),
        out_shape=jax.ShapeDtypeStruct((m, f), BF16),
        scratch_shapes=[pltpu.VMEM((k, tn), BF16), pltpu.VMEM((k, tn), BF16),
                        pltpu.VMEM((tm + SUBLANES, tn), F32), pltpu.VMEM((tm + SUBLANES, tn), F32)],
        compiler_params=_cparams(2),
        name="ffn_up",
    )(h2, w_up, w_up, conv_w, conv_w, conv_b[:, None, :], conv_b[:, None, :])


def kernel(x, c, positions, w_ada, b_ada, norm_mix, w_in, w_gate, ret_gn_g, ret_gn_b, diff_qn, diff_kn, diff_on, lam_q1, lam_k1, lam_q2, lam_k2, w_branch, w_out, norm_ffn, w_up, conv_w, conv_b, w_down):
    batch, s, d = x.shape
    depth = w_ada.shape[0]
    x2d = x.reshape(batch * s, d)
    outs = []
    for b in range(batch):
        xb = x2d[b * s:(b + 1) * s]
        pos = positions[b]
        mod = _adaln(c[b:b + 1], w_ada, b_ada)
        tables = _rope_tables(pos)
        for l in range(depth):
            shift1, scale1, gate1, shift2, scale2, gate2 = [
                mod[l, :, t * d:(t + 1) * d] for t in range(6)]
            h = _norm_mod(xb, norm_mix[l], scale1, shift1)
            qkv = _proj(h, w_in, l, name="in_proj")
            gates = _proj(h, w_gate, l, act="sigmoid", name="gate_proj")
            rq, rk, dq, dk, sb_vt, d_vt = _prep(qkv, tables, diff_qn[l], diff_kn[l])
            y_sb = _sb_attention(qkv, sb_vt)
            y_ret = _retention(qkv, rq, rk, ret_gn_g[l], ret_gn_b[l])
            lam_init = 0.8 - 0.6 * float(np.exp(-0.3 * l))
            y_diff = _diff_attention(dq, dk, d_vt, lam_q1[l], lam_k1[l], lam_q2[l], lam_k2[l],
                                     diff_on[l], lam_init)
            merged = _merge(y_sb, y_ret, y_diff, gates, w_branch, l)
            xb, h2 = _residual_proj_norm(merged, w_out, l, xb, gate1, norm_ffn[l], scale2, shift2,
                                         name="out_proj")
            act = _ffn_up(h2, w_up, conv_w, conv_b, l)
            xb = _residual_proj_deep(act, w_down, l, xb, gate2, name="ffn_down")
        outs.append(xb)
    return jnp.concatenate(outs, axis=0).reshape(batch, s, d)
```
